```python
import math
import jax, jax.numpy as jnp
from jax import lax
import numpy as np

D_MODEL = 2048
BATCH = 16
SEQ = 256
DEPTH = 2
DEC_BATCH = 2
DEC_SEQ = 4096
PAST_LEN = 512

GRID_W = 64
GDN_HEADS = 4
GDN_DK = 128
GDN_DV = 128
GDN_W = GDN_HEADS * GDN_DV
CONV_K = 4
CHUNK = 64
DIFF_HEADS = 8
DIFF_DQK = 64
DIFF_DV = 2 * DIFF_DQK
DIFF_QK_W = DIFF_HEADS * 2 * DIFF_DQK
DIFF_W = DIFF_HEADS * DIFF_DV
Q_BLOCK = 128
ROPE_BASE = 10000.0
POOL_WINDOWS = (2, 4, 8, 16)
POOL_GROUPS = len(POOL_WINDOWS)
POOL_GW = 128
POOL_W = POOL_GROUPS * POOL_GW
MIX_W = GDN_W + DIFF_W + POOL_W
D_FF = 5632
N_MOD = 9
EPS = 1e-6
IN_SPLIT_SIZES = (3 * GDN_W, GDN_W, 2 * GDN_HEADS, 2 * GDN_HEADS, DIFF_QK_W, DIFF_QK_W, DIFF_W, POOL_W)
IN_COLS = sum(IN_SPLIT_SIZES)
IN_SPLITS = tuple(int(s) for s in np.cumsum(IN_SPLIT_SIZES)[:-1])

kernel_name = 'hybrid_gdn_diffattn_pool_macaron_prefix_step'


def lambda_init(l):
    return 0.8 - 0.6 * math.exp(-0.3 * l)


def rmsnorm(x, g):
    xf = x.astype(jnp.float32)
    y = xf * lax.rsqrt(jnp.mean(xf * xf, axis=-1, keepdims=True) + EPS)
    return (y * g.astype(jnp.float32)).astype(x.dtype)


def l2norm(x):
    xf = x.astype(jnp.float32)
    return (xf * lax.rsqrt(jnp.sum(xf * xf, axis=-1, keepdims=True) + EPS)).astype(x.dtype)


def modulate(x, shift, scale):
    return x * (1.0 + scale) + shift


def swiglu(h, w_in, w_out):
    gate, up = jnp.split(h @ w_in, 2, axis=-1)
    return (jax.nn.silu(gate) * up) @ w_out


def short_conv(x, w):
    L = x.shape[1]
    left = CONV_K // 2
    xp = jnp.pad(x, ((0, 0), (left, CONV_K - 1 - left), (0, 0)))
    return sum(xp[:, j:j + L] * w[j] for j in range(CONV_K))


def axial_rope(x):
    L = x.shape[1]
    rows = L // GRID_W
    row = jnp.broadcast_to(jnp.arange(rows)[:, None], (rows, GRID_W)).reshape(L).astype(jnp.float32)
    col = jnp.broadcast_to(jnp.arange(GRID_W)[None, :], (rows, GRID_W)).reshape(L).astype(jnp.float32)
    half = DIFF_DQK // 2
    inv_freq = ROPE_BASE ** (-jnp.arange(0, half, 2, dtype=jnp.float32) / half)

    def rot(xa, pos):
        ang = pos[:, None] * inv_freq[None, :]
        cos = jnp.cos(ang)[None, :, None, None, :]
        sin = jnp.sin(ang)[None, :, None, None, :]
        x1, x2 = jnp.split(xa.astype(jnp.float32), 2, axis=-1)
        return jnp.concatenate([x1 * cos - x2 * sin, x1 * sin + x2 * cos], axis=-1)

    out = jnp.concatenate([rot(x[..., :half], row), rot(x[..., half:], col)], axis=-1)
    return out.astype(x.dtype)


def gdn_chunked(q, k, v, g, beta, s0):
    B, L, H, _ = q.shape
    n = L // CHUNK
    f32 = jnp.float32

    def to_chunks(t):
        t = t.astype(f32).reshape((B, n, CHUNK, H) + t.shape[3:])
        return jnp.moveaxis(t, 3, 2)

    qc, kc, vc = to_chunks(q), to_chunks(k), to_chunks(v)
    gc = jnp.cumsum(to_chunks(g), axis=-1)
    bc = to_chunks(beta)
    kb = kc * bc[..., None]
    vb = vc * bc[..., None]
    tri = jnp.tril(jnp.ones((CHUNK, CHUNK), bool))
    strict = jnp.tril(jnp.ones((CHUNK, CHUNK), bool), k=-1)
    decay = jnp.exp(jnp.where(tri, gc[..., :, None] - gc[..., None, :], -jnp.inf))
    lmat = jnp.where(strict, jnp.einsum('bnhid,bnhjd->bnhij', kb, kc) * decay, 0.0)
    eye = jnp.eye(CHUNK, dtype=f32)
    tmat = lax.linalg.triangular_solve(eye + lmat, jnp.broadcast_to(eye, lmat.shape),
                                       left_side=True, lower=True, unit_diagonal=True)
    u = jnp.einsum('bnhij,bnhjv->bnhiv', tmat, vb)
    w = jnp.einsum('bnhij,bnhjk->bnhik', tmat, kb * jnp.exp(gc)[..., None])
    a_intra = jnp.einsum('bnhid,bnhjd->bnhij', qc, kc) * decay
    qg = qc * jnp.exp(gc)[..., None]
    kd = kc * jnp.exp(gc[..., -1:] - gc)[..., None]
    glast = jnp.exp(gc[..., -1])

    def step(S, xs):
        u_i, w_i, qg_i, a_i, kd_i, gl_i = xs
        v_new = u_i - jnp.einsum('bhck,bhkv->bhcv', w_i, S)
        o = jnp.einsum('bhck,bhkv->bhcv', qg_i, S) + jnp.einsum('bhij,bhjv->bhiv', a_i, v_new)
        S = S * gl_i[..., None, None] + jnp.einsum('bhck,bhcv->bhkv', kd_i, v_new)
        return S, o

    xs = tuple(jnp.moveaxis(t, 1, 0) for t in (u, w, qg, a_intra, kd, glast))
    s_fin, o = lax.scan(step, s0.astype(f32), xs)
    o = o.transpose(1, 0, 3, 2, 4).reshape(B, L, H, v.shape[-1])
    return o.astype(v.dtype), s_fin


def diff_attention(q, k, v, lam):
    B, Lq, H = q.shape[:3]
    nb = Lq // Q_BLOCK
    qb = jnp.moveaxis(q.reshape(B, nb, Q_BLOCK, H, 2, DIFF_DQK), 1, 0)
    scale = DIFF_DQK ** -0.5

    def block(qi):
        s = jnp.einsum('bqhmd,bkhmd->bmhqk', qi, k, preferred_element_type=jnp.float32) * scale
        p = jax.nn.softmax(s, axis=-1)
        pd = p[:, 0] - lam * p[:, 1]
        return jnp.einsum('bhqk,bkhv->bqhv', pd.astype(v.dtype), v)

    o = lax.map(block, qb)
    return jnp.moveaxis(o, 0, 1).reshape(B, Lq, H, v.shape[-1])


def centred_mean(x, w):
    B, L, C = x.shape
    a = w // 2
    b = w - a - 1
    cs = jnp.concatenate([jnp.zeros((B, 1, C), jnp.float32), jnp.cumsum(x.astype(jnp.float32), axis=1)], axis=1)
    t = jnp.arange(L)
    lo = jnp.clip(t - a, 0, L)
    hi = jnp.clip(t + b + 1, 0, L)
    cnt = (hi - lo).astype(jnp.float32)
    return ((cs[:, hi] - cs[:, lo]) / cnt[None, :, None]).astype(x.dtype)


def token_mixing(h, lp, l, ctx):
    B, L, _ = h.shape
    proj = h @ lp['w_in']
    qkv, z, a, b, dq, dk, dv, pin = jnp.split(proj, IN_SPLITS, axis=-1)

    qkv = jax.nn.silu(short_conv(qkv, lp['gdn_conv']))
    gq, gk, gv = jnp.split(qkv, 3, axis=-1)
    gq = l2norm(gq.reshape(B, L, GDN_HEADS, GDN_DK)) * (GDN_DK ** -0.5)
    gk = l2norm(gk.reshape(B, L, GDN_HEADS, GDN_DK))
    gv = gv.reshape(B, L, GDN_HEADS, GDN_DV)
    a = a.reshape(B, L, 2, GDN_HEADS).astype(jnp.float32)
    g = -jnp.exp(lp['gdn_a_log'].astype(jnp.float32)) * jax.nn.softplus(a + lp['gdn_dt_bias'].astype(jnp.float32))
    beta = jax.nn.sigmoid(b.reshape(B, L, 2, GDN_HEADS).astype(jnp.float32))
    if ctx is None:
        s0 = jnp.zeros((B, 2, GDN_HEADS, GDN_DK, GDN_DV), jnp.float32)
    else:
        s0 = ctx[2]
    flip = lambda t: jnp.flip(t, axis=1)
    o_f, s_f = gdn_chunked(gq, gk, gv, g[:, :, 0], beta[:, :, 0], s0[:, 0])
    o_b, s_b = gdn_chunked(flip(gq), flip(gk), flip(gv), flip(g[:, :, 1]), flip(beta[:, :, 1]), s0[:, 1])
    o_gdn = rmsnorm(o_f + flip(o_b), lp['gdn_norm']) * jax.nn.silu(z.reshape(B, L, GDN_HEADS, GDN_DV))

    dq = dq.reshape(B, L, DIFF_HEADS, 2, DIFF_DQK)
    dk = dk.reshape(B, L, DIFF_HEADS, 2, DIFF_DQK)
    dv = dv.reshape(B, L, DIFF_HEADS, DIFF_DV)
    if ctx is None:
        keys, vals, queries = dk, dv, dq
    else:
        ck = ctx[0].reshape(B, -1, DIFF_HEADS, 2, DIFF_DQK)
        keys = jnp.concatenate([ck, axial_rope(dk)], axis=1)
        vals = jnp.concatenate([ctx[1], dv], axis=1)
        queries = axial_rope(dq)
    lam_vec = lp['diff_lam'].astype(jnp.float32)
    lam_init = lambda_init(l)
    lam = jnp.exp(jnp.sum(lam_vec[0] * lam_vec[1])) - jnp.exp(jnp.sum(lam_vec[2] * lam_vec[3])) + lam_init
    o_diff = rmsnorm(diff_attention(queries, keys, vals, lam), lp['diff_norm']) * (1.0 - lam_init)

    pg = pin.reshape(B, L, POOL_GROUPS, POOL_GW)
    pooled = jnp.stack([centred_mean(pg[:, :, i], w) - pg[:, :, i] for i, w in enumerate(POOL_WINDOWS)], axis=2)
    o_pool = jnp.einsum('blgc,gcd->blgd', pooled, lp['pool_w']).reshape(B, L, POOL_W) * lp['pool_scale']

    cat = jnp.concatenate([o_gdn.reshape(B, L, GDN_W), o_diff.reshape(B, L, DIFF_W), o_pool], axis=-1)
    out = cat @ lp['w_out']
    if ctx is None:
        ctx_out = (dk.reshape(B, L, DIFF_HEADS, 2 * DIFF_DQK), dv, jnp.stack([s_f, s_b], axis=1).astype(h.dtype))
    else:
        ctx_out = None
    return out, ctx_out


def trunk_layer(x, cond, lp, l, ctx):
    ada = jax.nn.silu(cond) @ lp['w_ada'] + lp['b_ada']
    sh1, sc1, g1, shm, scm, gm, sh2, sc2, g2 = [t[:, None, :] for t in jnp.split(ada, N_MOD, axis=-1)]
    h = modulate(rmsnorm(x, lp['norm_ffn1']), sh1, sc1)
    x = x + 0.5 * g1 * swiglu(h, lp['ffn1_in'], lp['ffn1_out'])
    h = modulate(rmsnorm(x, lp['norm_mix']), shm, scm)
    mixed, ctx_out = token_mixing(h, lp, l, ctx)
    x = x + gm * mixed
    h = modulate(rmsnorm(x, lp['norm_ffn2']), sh2, sc2)
    x = x + 0.5 * g2 * swiglu(h, lp['ffn2_in'], lp['ffn2_out'])
    return x, ctx_out


def setup_inputs(seed: int = 0) -> dict:
    key = jax.random.key(seed)
    ks = jax.random.split(key, 32)
    f32 = jnp.float32
    nrm = lambda k, shape, s: jax.random.normal(k, shape, f32) * s
    gain = lambda k, shape: 1.0 + 0.02 * jax.random.normal(k, shape, f32)
    dt = jnp.exp(jax.random.uniform(ks[15], (DEPTH, 2, GDN_HEADS), f32, math.log(1e-3), math.log(1e-1)))
    return {
        'x_prompt': nrm(ks[0], (BATCH, SEQ, D_MODEL), 1.0),
        'x_sample': nrm(ks[1], (DEC_BATCH, DEC_SEQ, D_MODEL), 1.0),
        'c': nrm(ks[2], (DEC_BATCH, D_MODEL), 1.0),
        'cache_k': nrm(ks[3], (DEC_BATCH, DEPTH, PAST_LEN, DIFF_HEADS, 2 * DIFF_DQK), 1.0),
        'cache_v': nrm(ks[4], (DEC_BATCH, DEPTH, PAST_LEN, DIFF_HEADS, DIFF_DV), 1.0),
        'state_gdn': nrm(ks[5], (DEC_BATCH, DEPTH, 2, GDN_HEADS, GDN_DK, GDN_DV), 0.3),
        'c_ctx': nrm(ks[6], (D_MODEL,), 1.0),
        'w_ada': nrm(ks[7], (DEPTH, D_MODEL, N_MOD * D_MODEL), 0.3 * D_MODEL ** -0.5),
        'b_ada': nrm(ks[8], (DEPTH, N_MOD * D_MODEL), 0.02),
        'norm_ffn1': gain(ks[9], (DEPTH, D_MODEL)),
        'ffn1_in': nrm(ks[10], (DEPTH, D_MODEL, 2 * D_FF), D_MODEL ** -0.5),
        'ffn1_out': nrm(ks[11], (DEPTH, D_FF, D_MODEL), D_FF ** -0.5),
        'norm_mix': gain(ks[12], (DEPTH, D_MODEL)),
        'w_in': nrm(ks[13], (DEPTH, D_MODEL, IN_COLS), D_MODEL ** -0.5),
        'gdn_conv': nrm(ks[14], (DEPTH, CONV_K, 3 * GDN_W), CONV_K ** -0.5),
        'gdn_a_log': jnp.log(jax.random.uniform(ks[16], (DEPTH, 2, GDN_HEADS), f32, 1.0, 16.0)),
        'gdn_dt_bias': dt + jnp.log(-jnp.expm1(-dt)),
        'gdn_norm': gain(ks[17], (DEPTH, GDN_DV)),
        'diff_lam': nrm(ks[18], (DEPTH, 4, DIFF_DQK), 0.1),
        'diff_norm': gain(ks[19], (DEPTH, DIFF_DV)),
        'pool_w': nrm(ks[20], (DEPTH, POOL_GROUPS, POOL_GW, POOL_GW), POOL_GW ** -0.5),
        'pool_scale': gain(ks[21], (DEPTH, POOL_W)),
        'w_out': nrm(ks[22], (DEPTH, MIX_W, D_MODEL), MIX_W ** -0.5),
        'norm_ffn2': gain(ks[23], (DEPTH, D_MODEL)),
        'ffn2_in': nrm(ks[24], (DEPTH, D_MODEL, 2 * D_FF), D_MODEL ** -0.5),
        'ffn2_out': nrm(ks[25], (DEPTH, D_FF, D_MODEL), D_FF ** -0.5),
        'final_norm': gain(ks[26], (D_MODEL,)),
    }


def reference(x_prompt, x_sample, c, cache_k, cache_v, state_gdn, c_ctx, w_ada, b_ada, norm_ffn1, ffn1_in,
              ffn1_out, norm_mix, w_in, gdn_conv, gdn_a_log, gdn_dt_bias, gdn_norm, diff_lam, diff_norm,
              pool_w, pool_scale, w_out, norm_ffn2, ffn2_in, ffn2_out, final_norm):
    lps = [dict(w_ada=w_ada[l], b_ada=b_ada[l], norm_ffn1=norm_ffn1[l], ffn1_in=ffn1_in[l], ffn1_out=ffn1_out[l],
                norm_mix=norm_mix[l], w_in=w_in[l], gdn_conv=gdn_conv[l], gdn_a_log=gdn_a_log[l],
                gdn_dt_bias=gdn_dt_bias[l], gdn_norm=gdn_norm[l], diff_lam=diff_lam[l], diff_norm=diff_norm[l],
                pool_w=pool_w[l], pool_scale=pool_scale[l], w_out=w_out[l], norm_ffn2=norm_ffn2[l],
                ffn2_in=ffn2_in[l], ffn2_out=ffn2_out[l]) for l in range(DEPTH)]

    yp = x_prompt
    ks_, vs_, ss_ = [], [], []
    for l in range(DEPTH):
        yp, (k_l, v_l, s_l) = trunk_layer(yp, c_ctx[None, :], lps[l], l, None)
        ks_.append(k_l)
        vs_.append(v_l)
        ss_.append(s_l)
    y_prompt = rmsnorm(yp, final_norm)
    new_cache_k = jnp.stack(ks_, axis=1)
    new_cache_v = jnp.stack(vs_, axis=1)
    new_state_gdn = jnp.stack(ss_, axis=1)

    ys = x_sample
    for l in range(DEPTH):
        ys, _ = trunk_layer(ys, c, lps[l], l, (cache_k[:, l], cache_v[:, l], state_gdn[:, l]))
    y_sample = rmsnorm(ys, final_norm)
    return (y_prompt, y_sample, new_cache_k, new_cache_v, new_state_gdn)
```

```python
import functools
import math

import jax
import jax.numpy as jnp
from jax import lax
from jax.experimental import pallas as pl
from jax.experimental.pallas import tpu as pltpu

F32 = jnp.float32
BF16 = jnp.bfloat16

GRID_W = 64
GDN_HEADS = 4
GDN_DK = 128
GDN_DV = 128
GDN_W = GDN_HEADS * GDN_DV
CONV_K = 4
CHUNK = 64
DIFF_HEADS = 8
DIFF_DQK = 64
DIFF_DV = 2 * DIFF_DQK
DIFF_W = DIFF_HEADS * DIFF_DV
ROPE_BASE = 10000.0
POOL_WINDOWS = (2, 4, 8, 16)
POOL_GW = 128
POOL_W = len(POOL_WINDOWS) * POOL_GW
MIX_W = GDN_W + DIFF_W + POOL_W
N_MOD = 9
EPS = 1e-6

LANE = 128
GDN_BLK = 4 * CHUNK
VMEM_LIMIT = 56 * 1024 * 1024

CB_Q, CB_K, CB_V, CB_Z = 0, 4, 8, 12
CB_DQ, CB_DK, CB_DV, CB_PIN, CB_AB = 16, 24, 32, 40, 44
PROJ_COLS = 45 * LANE


def _lambda_init(l):
    return 0.8 - 0.6 * math.exp(-0.3 * l)


def _cparams(*sem):
    return pltpu.CompilerParams(dimension_semantics=sem, vmem_limit_bytes=VMEM_LIMIT)


def _dot(a, b):
    return lax.dot_general(a, b, (((1,), (0,)), ((), ())), preferred_element_type=F32)


def _dot_nt(a, b):
    return lax.dot_general(a, b, (((1,), (1,)), ((), ())), preferred_element_type=F32)


def _dot_tn(a, b):
    return lax.dot_general(a, b, (((0,), (0,)), ((), ())), preferred_element_type=F32)


def _split2(a):
    hi = a.astype(BF16)
    lo = (a - hi.astype(F32)).astype(BF16)
    return hi, lo


def _dot3(a, b, dot=_dot):
    ah, al = _split2(a)
    bh, bl = _split2(b)
    return dot(ah, bh) + (dot(ah, bl) + dot(al, bh))


def _dot_exact_lhs(m, b):
    b0 = b.astype(BF16)
    r = b - b0.astype(F32)
    b1 = r.astype(BF16)
    b2 = (r - b1.astype(F32)).astype(BF16)
    return _dot(m, b0) + (_dot(m, b1) + _dot(m, b2))


def _sigmoid(x):
    return 1.0 / (1.0 + jnp.exp(-x))


def _silu(x):
    return x * _sigmoid(x)


def _ada_kernel(cond_ref, w_ref, b_ref, o_ref):
    s = _silu(cond_ref[...]).astype(BF16)
    o_ref[...] = _dot(s, w_ref[...].astype(BF16)) + b_ref[...]


def _ada(cond8, w_ada, b_ada):
    depth, d, n = w_ada.shape
    tn = 1024
    return pl.pallas_call(
        _ada_kernel,
        grid=(depth, n // tn),
        in_specs=[
            pl.BlockSpec((8, d), lambda l, j: (0, 0)),
            pl.BlockSpec((None, d, tn), lambda l, j: (l, 0, j)),
            pl.BlockSpec((None, 1, tn), lambda l, j: (l, 0, j)),
        ],
        out_specs=pl.BlockSpec((None, 8, tn), lambda l, j: (l, 0, j)),
        out_shape=jax.ShapeDtypeStruct((depth, 8, n), F32),
        compiler_params=_cparams("parallel", "parallel"),
        name="ada",
    )(cond8, w_ada, b_ada.reshape(depth, 1, n))


def _norm_mod(x, g, shift, scale):
    y = x * lax.rsqrt(jnp.mean(x * x, axis=-1, keepdims=True) + EPS) * g
    return y * (1.0 + scale) + shift


def _pm_kernel(x_ref, sh_ref, sc_ref, g_ref, w_ref, o_ref, h_ref):
    @pl.when(pl.program_id(1) == 0)
    def _():
        h_ref[...] = _norm_mod(x_ref[...], g_ref[...], sh_ref[0], sc_ref[0]).astype(BF16)

    o_ref[...] = _dot(h_ref[...], w_ref[...]).astype(o_ref.dtype)


def _pm_swiglu_kernel(x_ref, sh_ref, sc_ref, g_ref, wg_ref, wu_ref, o_ref, h_ref):
    @pl.when(pl.program_id(1) == 0)
    def _():
        h_ref[...] = _norm_mod(x_ref[...], g_ref[...], sh_ref[0], sc_ref[0]).astype(BF16)

    h = h_ref[...]
    gate = _dot(h, wg_ref[...])
    up = _dot(h, wu_ref[...])
    o_ref[...] = (_silu(gate) * up).astype(o_ref.dtype)


def _row_tile(groups, cap):
    tm = cap
    while any(g % tm for g in groups):
        tm //= 2
    return tm


def _mod_row(n_ctx, dec_seq, tm):
    def f(i):
        r = i * tm
        return jnp.where(r < n_ctx, 0, 1 + (r - n_ctx) // dec_seq)
    return f


def _pm(x, ada3, mod_cols, g, w, *, layout, tn, swiglu, out_dtype):
    t, d = x.shape
    n_ctx, dec_seq = layout
    tm = _row_tile((n_ctx, dec_seq), 1024)
    row = _mod_row(n_ctx, dec_seq, tm)
    sh_c, sc_c = mod_cols
    n = w.shape[1] // 2 if swiglu else w.shape[1]
    nj = n // tn
    in_specs = [
        pl.BlockSpec((tm, d), lambda i, j: (i, 0)),
        pl.BlockSpec((1, 1, d), lambda i, j: (row(i), 0, sh_c)),
        pl.BlockSpec((1, 1, d), lambda i, j: (row(i), 0, sc_c)),
        pl.BlockSpec((1, d), lambda i, j: (0, 0)),
        pl.BlockSpec((d, tn), lambda i, j: (0, j)),
    ]
    args = [x, ada3, ada3, g.reshape(1, d), w]
    if swiglu:
        in_specs.append(pl.BlockSpec((d, tn), lambda i, j: (0, j + nj)))
        args.append(w)
    return pl.pallas_call(
        _pm_swiglu_kernel if swiglu else _pm_kernel,
        grid=(t // tm, nj),
        in_specs=in_specs,
        out_specs=pl.BlockSpec((tm, tn), lambda i, j: (i, j)),
        out_shape=jax.ShapeDtypeStruct((t, n), out_dtype),
        scratch_shapes=[pltpu.VMEM((tm, d), BF16)],
        compiler_params=_cparams("parallel", "arbitrary"),
        name="ffn_in" if swiglu else "mix_in",
    )(*args)


def _em_kernel(a_ref, w_ref, x_ref, gate_ref, o_ref, *, coef):
    y = _dot(a_ref[...], w_ref[...])
    o_ref[...] = x_ref[...] + (coef * gate_ref[0]) * y


def _em(a, w, x, ada3, gate_col, coef, *, layout, name):
    t, k = a.shape
    d = w.shape[1]
    n_ctx, dec_seq = layout
    tm = _row_tile((n_ctx, dec_seq), 512)
    tn = min(512, d)
    row = _mod_row(n_ctx, dec_seq, tm)
    gpb = d // tn
    return pl.pallas_call(
        functools.partial(_em_kernel, coef=coef),
        grid=(t // tm, d // tn),
        in_specs=[
            pl.BlockSpec((tm, k), lambda i, j: (i, 0)),
            pl.BlockSpec((k, tn), lambda i, j: (0, j)),
            pl.BlockSpec((tm, tn), lambda i, j: (i, j)),
            pl.BlockSpec((1, 1, tn), lambda i, j: (row(i), 0, gate_col * gpb + j)),
        ],
        out_specs=pl.BlockSpec((tm, tn), lambda i, j: (i, j)),
        out_shape=jax.ShapeDtypeStruct((t, d), F32),
        compiler_params=_cparams("parallel", "parallel"),
        name=name,
    )(a, w, x, ada3)


def _final_norm_kernel(x_ref, g_ref, o_ref):
    x = x_ref[...]
    o_ref[...] = x * lax.rsqrt(jnp.mean(x * x, axis=-1, keepdims=True) + EPS) * g_ref[...]


def _final_norm(x, g):
    t, d = x.shape
    tm = 256
    return pl.pallas_call(
        _final_norm_kernel,
        grid=(t // tm,),
        in_specs=[pl.BlockSpec((tm, d), lambda i: (i, 0)), pl.BlockSpec((1, d), lambda i: (0, 0))],
        out_specs=pl.BlockSpec((tm, d), lambda i: (i, 0)),
        out_shape=jax.ShapeDtypeStruct((t, d), F32),
        compiler_params=_cparams("parallel"),
        name="final_norm",
    )(x, g.reshape(1, d))


def _shift_rows(x, off, row):
    if off == 0:
        return x
    n = x.shape[0]
    y = pltpu.roll(x, (-off) % n, 0)
    ok = (row + off >= 0) & (row + off < n)
    return jnp.where(ok, y, 0.0)


def _gdn_prep_kernel(x_ref, w_ref, o_ref):
    x = x_ref[...]
    row = lax.broadcasted_iota(jnp.int32, x.shape, 0)
    left = CONV_K // 2
    acc = jnp.zeros_like(x)
    for j in range(CONV_K):
        acc = acc + _shift_rows(x, j - left, row) * w_ref[j:j + 1, :]
    y = _silu(acc)
    kind = pl.program_id(1) // GDN_HEADS
    inv = lax.rsqrt(jnp.sum(y * y, axis=-1, keepdims=True) + EPS)
    inv = inv * jnp.where(kind == 0, GDN_DK ** -0.5, 1.0)
    o_ref[...] = jnp.where(kind < 2, y * inv, y)


def _gdn_prep(proj, conv_w, row0, nseq, seqlen):
    blk0 = row0 // seqlen
    return pl.pallas_call(
        _gdn_prep_kernel,
        grid=(nseq, 3 * GDN_HEADS),
        in_specs=[
            pl.BlockSpec((seqlen, LANE), lambda s, c: (blk0 + s, CB_Q + c)),
            pl.BlockSpec((CONV_K, LANE), lambda s, c: (0, c)),
        ],
        out_specs=pl.BlockSpec((seqlen, LANE), lambda s, c: (s, c)),
        out_shape=jax.ShapeDtypeStruct((nseq * seqlen, 3 * GDN_W), F32),
        compiler_params=_cparams("parallel", "parallel"),
        name="gdn_prep",
    )(proj, conv_w)


def _unit_tri_inverse(lmat, eye):
    x = eye - lmat
    p = _dot3(lmat, lmat)
    steps = int(math.log2(CHUNK)) - 1
    for s in range(steps):
        x = x + _dot3(x, p)
        if s + 1 < steps:
            p = _dot3(p, p)
    return x


def _gdn_intra_kernel(q_ref, k_ref, v_ref, ab_ref, alog_ref, dt_ref,
                      u_ref, w_ref, qg_ref, kd_ref, a_ref, gl_ref):
    n = GDN_BLK
    h = pl.program_id(1)
    q = q_ref[...]
    k = k_ref[...]
    v = v_ref[...]
    ab = ab_ref[...]
    lane = lax.broadcasted_iota(jnp.int32, ab.shape, 1)
    sp = jnp.maximum(ab + dt_ref[...], 0.0) + jnp.log(1.0 + jnp.exp(-jnp.abs(ab + dt_ref[...])))
    gmat = -jnp.exp(alog_ref[...]) * sp
    bmat = _sigmoid(ab)

    ri = lax.broadcasted_iota(jnp.int32, (n, n), 0)
    ci = lax.broadcasted_iota(jnp.int32, (n, n), 1)
    same = (ri // CHUNK) == (ci // CHUNK)
    eye = (ri == ci).astype(F32)
    ones_blk = same.astype(BF16)
    gtot_all = _dot_exact_lhs(ones_blk, gmat)
    kk = _dot3(k, k, _dot_nt)
    qk = _dot3(q, k, _dot_nt)

    for d in range(2):
        incl = same & ((ci <= ri) if d == 0 else (ci >= ri))
        strict = same & ((ci < ri) if d == 0 else (ci > ri))
        gc_all = _dot_exact_lhs(incl.astype(BF16), gmat)
        sel_g = lane == d * GDN_HEADS + h
        sel_b = lane == 2 * GDN_HEADS + d * GDN_HEADS + h
        gc = jnp.sum(jnp.where(sel_g, gc_all, 0.0), axis=1, keepdims=True)
        gtot = jnp.sum(jnp.where(sel_g, gtot_all, 0.0), axis=1, keepdims=True)
        beta = jnp.sum(jnp.where(sel_b, bmat, 0.0), axis=1, keepdims=True)
        gc_b = jnp.broadcast_to(gc, (n, LANE))
        gc_row = jnp.transpose(gc_b)[0:1, :]
        diff = jnp.where(incl, gc - gc_row, 0.0)
        decay = jnp.where(incl, jnp.exp(diff), 0.0)
        lmat = jnp.where(strict, kk * beta * decay, 0.0)
        tmat = _unit_tri_inverse(lmat, eye)
        eg = jnp.exp(gc_b)
        kb = k * beta
        rhs = jnp.concatenate([v * beta, kb * eg], axis=1)
        uw = _dot3(tmat, rhs)
        u_ref[d] = uw[:, :GDN_DV]
        w_ref[d] = uw[:, GDN_DV:]
        qg_ref[d] = q * eg
        kd_ref[d] = k * jnp.exp(jnp.broadcast_to(gtot - gc, (n, LANE)))
        amat = qk * decay
        a_ref[d] = (amat[:, 0:CHUNK] + amat[:, CHUNK:2 * CHUNK]) + (
            amat[:, 2 * CHUNK:3 * CHUNK] + amat[:, 3 * CHUNK:4 * CHUNK])
        egt = jnp.exp(jnp.broadcast_to(gtot, (n, LANE)))
        r8 = lax.broadcasted_iota(jnp.int32, (8, LANE), 0)
        gl8 = jnp.zeros((8, LANE), F32)
        for c in range(n // CHUNK):
            gl8 = jnp.where(r8 == c, egt[c * CHUNK:c * CHUNK + 8, :], gl8)
        gl_ref[d] = gl8


def _gdn_intra(qkv, proj, alog_row, dt_row, row0):
    t = qkv.shape[0]
    nb = t // GDN_BLK
    pb0 = row0 // GDN_BLK
    big = jax.ShapeDtypeStruct((2, t, GDN_W), F32)
    big_spec = pl.BlockSpec((2, GDN_BLK, LANE), lambda i, h: (0, i, h))
    return pl.pallas_call(
        _gdn_intra_kernel,
        grid=(nb, GDN_HEADS),
        in_specs=[
            pl.BlockSpec((GDN_BLK, LANE), lambda i, h: (i, h)),
            pl.BlockSpec((GDN_BLK, LANE), lambda i, h: (i, GDN_HEADS + h)),
            pl.BlockSpec((GDN_BLK, LANE), lambda i, h: (i, 2 * GDN_HEADS + h)),
            pl.BlockSpec((GDN_BLK, LANE), lambda i, h: (pb0 + i, CB_AB)),
            pl.BlockSpec((1, LANE), lambda i, h: (0, 0)),
            pl.BlockSpec((1, LANE), lambda i, h: (0, 0)),
        ],
        out_specs=[big_spec, big_spec, big_spec, big_spec,
                   pl.BlockSpec((2, None, GDN_BLK, CHUNK), lambda i, h: (0, h, i, 0)),
                   pl.BlockSpec((2, 8, LANE), lambda i, h: (0, i, h))],
        out_shape=[big, big, big, big,
                   jax.ShapeDtypeStruct((2, GDN_HEADS, t, CHUNK), F32),
                   jax.ShapeDtypeStruct((2, nb * 8, GDN_W), F32)],
        compiler_params=_cparams("parallel", "parallel"),
        name="gdn_intra",
    )(qkv, qkv, qkv, proj, alog_row, dt_row)


def _gdn_scan_kernel(*refs, has_init):
    if has_init:
        s0_ref, u_ref, w_ref, qg_ref, kd_ref, a_ref, gl_ref, o_ref, sf_ref, s_ref = refs
    else:
        u_ref, w_ref, qg_ref, kd_ref, a_ref, gl_ref, o_ref, sf_ref, s_ref = refs
    d = pl.program_id(0)
    blk = pl.program_id(2)

    @pl.when(blk == 0)
    def _():
        if has_init:
            s_ref[...] = s0_ref[...]
        else:
            s_ref[...] = jnp.zeros_like(s_ref)

    r8 = lax.broadcasted_iota(jnp.int32, (8, LANE), 0)
    for c in range(GDN_BLK // CHUNK):
        cc = jnp.where(d == 0, c, GDN_BLK // CHUNK - 1 - c)
        rows = pl.ds(pl.multiple_of(cc * CHUNK, CHUNK), CHUNK)
        for h in range(GDN_HEADS):
            cols = slice(h * LANE, (h + 1) * LANE)
            s = s_ref[h]
            wq = jnp.concatenate([w_ref[rows, cols], qg_ref[rows, cols]], axis=0)
            ws = _dot3(wq, s)
            v_new = u_ref[rows, cols] - ws[:CHUNK]
            o_ref[rows, cols] = ws[CHUNK:] + _dot3(a_ref[h, rows, :], v_new)
            gl = jnp.sum(jnp.where(r8 == cc, gl_ref[:, cols], 0.0), axis=0, keepdims=True)
            s_ref[h] = s * gl + _dot3(kd_ref[rows, cols], v_new, _dot_tn)

    @pl.when(blk == pl.num_programs(2) - 1)
    def _():
        sf_ref[...] = s_ref[...]


def _gdn_scan(s0, u, w, qg, kd, a, gl, nseq, seqlen):
    t = u.shape[1]
    nblk = seqlen // GDN_BLK

    def rb(d, s, b):
        return s * nblk + jnp.where(d == 0, b, nblk - 1 - b)

    big_spec = pl.BlockSpec((None, GDN_BLK, GDN_W), lambda d, s, b: (d, rb(d, s, b), 0))
    in_specs = [big_spec, big_spec, big_spec, big_spec,
                pl.BlockSpec((None, GDN_HEADS, GDN_BLK, CHUNK), lambda d, s, b: (d, 0, rb(d, s, b), 0)),
                pl.BlockSpec((None, 8, GDN_W), lambda d, s, b: (d, rb(d, s, b), 0))]
    args = [u, w, qg, kd, a, gl]
    state_spec = pl.BlockSpec((None, None, GDN_HEADS, GDN_DK, GDN_DV), lambda d, s, b: (s, d, 0, 0, 0))
    if s0 is not None:
        in_specs.insert(0, state_spec)
        args.insert(0, s0)
    return pl.pallas_call(
        functools.partial(_gdn_scan_kernel, has_init=s0 is not None),
        grid=(2, nseq, nblk),
        in_specs=in_specs,
        out_specs=[big_spec, state_spec],
        out_shape=[jax.ShapeDtypeStruct((2, t, GDN_W), F32),
                   jax.ShapeDtypeStruct((nseq, 2, GDN_HEADS, GDN_DK, GDN_DV), F32)],
        scratch_shapes=[pltpu.VMEM((GDN_HEADS, GDN_DK, GDN_DV), F32)],
        compiler_params=_cparams("parallel", "parallel", "arbitrary"),
        name="gdn_scan",
    )(*args)


def _gdn_post_kernel(o_ref, z_ref, g_ref, out_ref):
    for h in range(GDN_HEADS):
        cols = slice(h * LANE, (h + 1) * LANE)
        o = o_ref[0, :, cols] + o_ref[1, :, cols]
        y = o * lax.rsqrt(jnp.mean(o * o, axis=-1, keepdims=True) + EPS) * g_ref[...]
        out_ref[:, cols] = (y * _silu(z_ref[:, cols])).astype(out_ref.dtype)


def _gdn_post(o, proj, g, row0):
    t = o.shape[1]
    tm = 256
    pb0 = row0 // tm
    return pl.pallas_call(
        _gdn_post_kernel,
        grid=(t // tm,),
        in_specs=[
            pl.BlockSpec((2, tm, GDN_W), lambda i: (0, i, 0)),
            pl.BlockSpec((tm, GDN_W), lambda i: (pb0 + i, CB_Z // GDN_HEADS)),
            pl.BlockSpec((1, GDN_DV), lambda i: (0, 0)),
        ],
        out_specs=pl.BlockSpec((tm, GDN_W), lambda i: (i, 0)),
        out_shape=jax.ShapeDtypeStruct((t, GDN_W), BF16),
        compiler_params=_cparams("parallel"),
        name="gdn_post",
    )(o, proj, g.reshape(1, GDN_DV))


def _rope_kernel(x_ref, cos_ref, sin_ref, o_ref):
    cos = cos_ref[...]
    sin = sin_ref[...]
    lane = lax.broadcasted_iota(jnp.int32, cos.shape, 1)
    first = (lane % 32) < 16
    for hb in range(DIFF_HEADS):
        cols = slice(hb * LANE, (hb + 1) * LANE)
        x = x_ref[:, cols]
        partner = jnp.where(first, pltpu.roll(x, LANE - 16, 1), pltpu.roll(x, 16, 1))
        o_ref[:, cols] = x * cos + partner * sin


def _rope_tables(seqlen):
    pos = jnp.arange(seqlen)
    rowp = (pos // GRID_W).astype(F32)
    colp = (pos % GRID_W).astype(F32)
    half = DIFF_DQK // 2
    inv_freq = ROPE_BASE ** (-jnp.arange(0, half, 2, dtype=F32) / half)
    c = jnp.arange(LANE)
    c64 = c % DIFF_DQK
    p = jnp.where((c64 // half)[None, :] == 0, rowp[:, None], colp[:, None])
    ang = p * inv_freq[c64 % (half // 2)][None, :]
    sign = jnp.where((c64 % half) < half // 2, -1.0, 1.0).astype(F32)
    return jnp.cos(ang), jnp.sin(ang) * sign[None, :]


def _rope(proj, cos, sin, row0, nrows, seqlen):
    tm = 256
    pb0 = row0 // tm
    nsb = seqlen // tm
    return pl.pallas_call(
        _rope_kernel,
        grid=(2, nrows // tm),
        in_specs=[
            pl.BlockSpec((tm, DIFF_W), lambda j, i: (pb0 + i, CB_DQ // DIFF_HEADS + j)),
            pl.BlockSpec((tm, LANE), lambda j, i: (i % nsb, 0)),
            pl.BlockSpec((tm, LANE), lambda j, i: (i % nsb, 0)),
        ],
        out_specs=pl.BlockSpec((None, tm, DIFF_W), lambda j, i: (j, i, 0)),
        out_shape=jax.ShapeDtypeStruct((2, nrows, DIFF_W), F32),
        compiler_params=_cparams("parallel", "parallel"),
        name="rope",
    )(proj, cos, sin)


def _diff_attn_kernel(*refs, nseg, lam_init):
    q_ref = refs[0]
    k_refs = refs[1:1 + nseg]
    v_refs = refs[1 + nseg:1 + 2 * nseg]
    lam_ref, g_ref, o_ref = refs[1 + 2 * nseg:]
    lv = lam_ref[...]
    lam = (jnp.exp(jnp.sum(lv[0:1] * lv[1:2], axis=1, keepdims=True))
           - jnp.exp(jnp.sum(lv[2:3] * lv[3:4], axis=1, keepdims=True)) + lam_init)
    q = q_ref[...] * (DIFF_DQK ** -0.5)
    lane = lax.broadcasted_iota(jnp.int32, q.shape, 1)
    ks = [k_ref[...].astype(BF16) for k_ref in k_refs]
    vs = [v_ref[...].astype(BF16) for v_ref in v_refs]
    outs = []
    for m in range(2):
        qm = jnp.where((lane // DIFF_DQK) == m, q, 0.0).astype(BF16)
        ss = [_dot_nt(qm, kseg) for kseg in ks]
        mx = functools.reduce(jnp.maximum, [jnp.max(s, axis=1, keepdims=True) for s in ss])
        es = [jnp.exp(s - mx) for s in ss]
        den = functools.reduce(lambda a, b: a + b, [jnp.sum(e, axis=1, keepdims=True) for e in es])
        acc = functools.reduce(lambda a, b: a + b, [_dot(e.astype(BF16), vseg) for e, vseg in zip(es, vs)])
        outs.append(acc / den)
    o = outs[0] - lam * outs[1]
    y = o * lax.rsqrt(jnp.mean(o * o, axis=-1, keepdims=True) + EPS) * g_ref[...]
    o_ref[...] = (y * (1.0 - lam_init)).astype(o_ref.dtype)


def _diff_attn(q_arr, q_spec, k_arrs, k_specs, v_arrs, v_specs, lam_vec, g, nseq, seqlen, tq, lam_init):
    nseg = len(k_arrs)
    nqb = seqlen // tq
    return pl.pallas_call(
        functools.partial(_diff_attn_kernel, nseg=nseg, lam_init=lam_init),
        grid=(nseq, DIFF_HEADS, nqb),
        in_specs=[q_spec, *k_specs, *v_specs,
                  pl.BlockSpec((4, DIFF_DQK), lambda b, h, i: (0, 0)),
                  pl.BlockSpec((1, DIFF_DV), lambda b, h, i: (0, 0))],
        out_specs=pl.BlockSpec((tq, DIFF_DV), lambda b, h, i: (b * nqb + i, h)),
        out_shape=jax.ShapeDtypeStruct((nseq * seqlen, DIFF_W), BF16),
        compiler_params=_cparams("parallel", "parallel", "arbitrary"),
        name="diff_attn",
    )(q_arr, *k_arrs, *v_arrs, lam_vec, g.reshape(1, DIFF_DV))


def _pool_kernel(x_ref, w_ref, sc_ref, o_ref):
    n = x_ref.shape[0]
    row = lax.broadcasted_iota(jnp.int32, (n, POOL_GW), 0)
    for gi, win in enumerate(POOL_WINDOWS):
        cols = slice(gi * POOL_GW, (gi + 1) * POOL_GW)
        x = x_ref[:, cols]
        a = win // 2
        b = win - a - 1
        acc = x
        for off in range(-a, b + 1):
            if off != 0:
                acc = acc + _shift_rows(x, off, row)
        cnt = (jnp.minimum(row + b + 1, n) - jnp.maximum(row - a, 0)).astype(F32)
        pooled = acc / cnt - x
        y = _dot(pooled.astype(BF16), w_ref[gi].astype(BF16))
        o_ref[:, cols] = (y * sc_ref[:, cols]).astype(o_ref.dtype)


def _pool(proj, pool_w, pool_scale, row0, nseq, seqlen):
    blk0 = row0 // seqlen
    return pl.pallas_call(
        _pool_kernel,
        grid=(nseq,),
        in_specs=[
            pl.BlockSpec((seqlen, POOL_W), lambda s: (blk0 + s, CB_PIN // len(POOL_WINDOWS))),
            pl.BlockSpec((len(POOL_WINDOWS), POOL_GW, POOL_GW), lambda s: (0, 0, 0)),
            pl.BlockSpec((1, POOL_W), lambda s: (0, 0)),
        ],
        out_specs=pl.BlockSpec((seqlen, POOL_W), lambda s: (s, 0)),
        out_shape=jax.ShapeDtypeStruct((nseq * seqlen, POOL_W), BF16),
        compiler_params=_cparams("parallel"),
        name="pool",
    )(proj, pool_w, pool_scale.reshape(1, POOL_W))


def _mix_group(proj, lp, l, row0, nseq, seqlen, ctx):
    nrows = nseq * seqlen
    qkv = _gdn_prep(proj, lp["gdn_conv"], row0, nseq, seqlen)
    u, w, qg, kd, a, gl = _gdn_intra(qkv, proj, lp["alog_row"], lp["dt_row"], row0)
    o, s_fin = _gdn_scan(None if ctx is None else ctx[2], u, w, qg, kd, a, gl, nseq, seqlen)
    o_gdn = _gdn_post(o, proj, lp["gdn_norm"], row0)

    tq = 256
    nqb = seqlen // tq
    qb0 = row0 // tq
    sb0 = row0 // seqlen
    if ctx is None:
        q_arr = proj
        q_spec = pl.BlockSpec((tq, DIFF_DV), lambda b, h, i: (qb0 + b * nqb + i, CB_DQ + h))
        k_arrs = [proj]
        k_specs = [pl.BlockSpec((seqlen, DIFF_DV), lambda b, h, i: (sb0 + b, CB_DK + h))]
        v_arrs = [proj]
        v_specs = [pl.BlockSpec((seqlen, DIFF_DV), lambda b, h, i: (sb0 + b, CB_DV + h))]
    else:
        cache_k, cache_v = ctx[0], ctx[1]
        past = cache_k.shape[2]
        cos, sin = _rope_tables(seqlen)
        qk_rot = _rope(proj, cos, sin, row0, nrows, seqlen)
        q_arr = qk_rot
        q_spec = pl.BlockSpec((None, tq, DIFF_DV), lambda b, h, i: (0, b * nqb + i, h))
        k_arrs = [cache_k, qk_rot]
        k_specs = [pl.BlockSpec((None, None, past, DIFF_DV), lambda b, h, i: (b, l, 0, h)),
                   pl.BlockSpec((None, seqlen, DIFF_DV), lambda b, h, i: (1, b, h))]
        v_arrs = [cache_v, proj]
        v_specs = [pl.BlockSpec((None, None, past, DIFF_DV), lambda b, h, i: (b, l, 0, h)),
                   pl.BlockSpec((seqlen, DIFF_DV), lambda b, h, i: (sb0 + b, CB_DV + h))]
    o_diff = _diff_attn(q_arr, q_spec, k_arrs, k_specs, v_arrs, v_specs, lp["diff_lam"], lp["diff_norm"],
                        nseq, seqlen, tq, _lambda_init(l))

    o_pool = _pool(proj, lp["pool_w"], lp["pool_scale"], row0, nseq, seqlen)
    return jnp.concatenate([o_gdn, o_diff, o_pool], axis=1), s_fin


def _reorder_w_in(w_in):
    d = w_in.shape[0]
    o_ab = 4 * GDN_W
    o_dq = o_ab + 4 * GDN_HEADS
    pad = jnp.zeros((d, LANE - 4 * GDN_HEADS), w_in.dtype)
    return jnp.concatenate([w_in[:, :o_ab], w_in[:, o_dq:], w_in[:, o_ab:o_dq], pad], axis=1)


def kernel(x_prompt, x_sample, c, cache_k, cache_v, state_gdn, c_ctx, w_ada, b_ada, norm_ffn1, ffn1_in,
           ffn1_out, norm_mix, w_in, gdn_conv, gdn_a_log, gdn_dt_bias, gdn_norm, diff_lam, diff_norm,
           pool_w, pool_scale, w_out, norm_ffn2, ffn2_in, ffn2_out, final_norm):
    batch, seq, d = x_prompt.shape
    dec_batch, dec_seq, _ = x_sample.shape
    depth = w_ada.shape[0]
    d_ff = ffn1_out.shape[1]
    n_ctx = batch * seq
    n_dec = dec_batch * dec_seq
    layout = (n_ctx, dec_seq)
    past = cache_k.shape[2]

    x = jnp.concatenate([x_prompt.reshape(n_ctx, d), x_sample.reshape(n_dec, d)], axis=0)
    cond8 = jnp.concatenate([c_ctx[None, :], c, jnp.zeros((8 - 1 - dec_batch, d), F32)], axis=0)
    ada = _ada(cond8, w_ada, b_ada)
    cache_k4 = cache_k.reshape(dec_batch, depth, past, DIFF_W)
    cache_v4 = cache_v.reshape(dec_batch, depth, past, DIFF_W)
    ff_tn = 512 if d_ff % 512 == 0 else d_ff

    ks, vs, ss = [], [], []
    for l in range(depth):
        ada3 = ada[l, :1 + dec_batch].reshape(1 + dec_batch, 1, N_MOD * d)
        lane_pad = jnp.zeros((LANE - 2 * GDN_HEADS,), F32)
        lp = dict(
            gdn_conv=gdn_conv[l], gdn_norm=gdn_norm[l], diff_lam=diff_lam[l], diff_norm=diff_norm[l],
            pool_w=pool_w[l], pool_scale=pool_scale[l],
            alog_row=jnp.concatenate([gdn_a_log[l].reshape(-1), lane_pad]).reshape(1, LANE),
            dt_row=jnp.concatenate([gdn_dt_bias[l].reshape(-1), lane_pad]).reshape(1, LANE),
        )
        act = _pm(x, ada3, (0, 1), norm_ffn1[l], ffn1_in[l].astype(BF16), layout=layout, tn=ff_tn,
                  swiglu=True, out_dtype=BF16)
        x = _em(act, ffn1_out[l].astype(BF16), x, ada3, 2, 0.5, layout=layout, name="ffn_out")

        proj = _pm(x, ada3, (3, 4), norm_mix[l], _reorder_w_in(w_in[l]).astype(BF16), layout=layout,
                   tn=640, swiglu=False, out_dtype=F32)
        cat_c, s_c = _mix_group(proj, lp, l, 0, batch, seq, None)
        cat_d, _ = _mix_group(proj, lp, l, n_ctx, dec_batch, dec_seq,
                              (cache_k4, cache_v4, state_gdn[:, l]))
        cat = jnp.concatenate([cat_c, cat_d], axis=0)
        x = _em(cat, w_out[l].astype(BF16), x, ada3, 5, 1.0, layout=layout, name="mix_out")
        ks.append(proj[:n_ctx, CB_DK * LANE:CB_DK * LANE + DIFF_W].reshape(batch, seq, DIFF_HEADS, DIFF_DV))
        vs.append(proj[:n_ctx, CB_DV * LANE:CB_DV * LANE + DIFF_W].reshape(batch, seq, DIFF_HEADS, DIFF_DV))
        ss.append(s_c)

        act = _pm(x, ada3, (6, 7), norm_ffn2[l], ffn2_in[l].astype(BF16), layout=layout, tn=ff_tn,
                  swiglu=True, out_dtype=BF16)
        x = _em(act, ffn2_out[l].astype(BF16), x, ada3, 8, 0.5, layout=layout, name="ffn_out")

    y = _final_norm(x, final_norm)
    y_prompt = y[:n_ctx].reshape(batch, seq, d)
    y_sample = y[n_ctx:].reshape(dec_batch, dec_seq, d)
    return (y_prompt, y_sample, jnp.stack(ks, axis=1), jnp.stack(vs, axis=1), jnp.stack(ss, axis=1))
```

```python
import functools
import math

import jax
import jax.numpy as jnp
from jax import lax
from jax.experimental import pallas as pl
from jax.experimental.pallas import tpu as pltpu

F32 = jnp.float32
BF16 = jnp.bfloat16

GRID_W = 64
GDN_HEADS = 4
GDN_DK = 128
GDN_DV = 128
GDN_W = GDN_HEADS * GDN_DV
CONV_K = 4
CHUNK = 64
DIFF_HEADS = 8
DIFF_DQK = 64
DIFF_DV = 2 * DIFF_DQK
DIFF_W = DIFF_HEADS * DIFF_DV
ROPE_BASE = 10000.0
POOL_WINDOWS = (2, 4, 8, 16)
POOL_GW = 128
POOL_W = len(POOL_WINDOWS) * POOL_GW
MIX_W = GDN_W + DIFF_W + POOL_W
N_MOD = 9
EPS = 1e-6

LANE = 128
GDN_BLK = 4 * CHUNK
VMEM_LIMIT = 56 * 1024 * 1024

CB_Q, CB_K, CB_V, CB_Z = 0, 4, 8, 12
CB_DQ, CB_DK, CB_DV, CB_PIN, CB_AB = 16, 24, 32, 40, 44
PROJ_COLS = 45 * LANE


def _lambda_init(l):
    return 0.8 - 0.6 * math.exp(-0.3 * l)


def _cparams(*sem):
    return pltpu.CompilerParams(dimension_semantics=sem, vmem_limit_bytes=VMEM_LIMIT)


def _dot(a, b):
    return lax.dot_general(a, b, (((1,), (0,)), ((), ())), preferred_element_type=F32)


def _dot_nt(a, b):
    return lax.dot_general(a, b, (((1,), (1,)), ((), ())), preferred_element_type=F32)


def _dot_tn(a, b):
    return lax.dot_general(a, b, (((0,), (0,)), ((), ())), preferred_element_type=F32)


def _mm(a, b, dot=_dot):
    return dot(a.astype(BF16), b.astype(BF16))


def _dot_exact_lhs(m, b):
    b0 = b.astype(BF16)
    r = b - b0.astype(F32)
    b1 = r.astype(BF16)
    b2 = (r - b1.astype(F32)).astype(BF16)
    return _dot(m, b0) + (_dot(m, b1) + _dot(m, b2))


def _sigmoid(x):
    return 1.0 / (1.0 + jnp.exp(-x))


def _silu(x):
    return x * _sigmoid(x)


def _ada_kernel(cond_ref, w_ref, b_ref, o_ref):
    s = _silu(cond_ref[...]).astype(BF16)
    o_ref[...] = _dot(s, w_ref[...].astype(BF16)) + b_ref[...]


def _ada(cond8, w_ada, b_ada):
    depth, d, n = w_ada.shape
    tn = 1024
    return pl.pallas_call(
        _ada_kernel,
        grid=(depth, n // tn),
        in_specs=[
            pl.BlockSpec((8, d), lambda l, j: (0, 0)),
            pl.BlockSpec((None, d, tn), lambda l, j: (l, 0, j)),
            pl.BlockSpec((None, 1, tn), lambda l, j: (l, 0, j)),
        ],
        out_specs=pl.BlockSpec((None, 8, tn), lambda l, j: (l, 0, j)),
        out_shape=jax.ShapeDtypeStruct((depth, 8, n), F32),
        compiler_params=_cparams("parallel", "parallel"),
        name="ada",
    )(cond8, w_ada, b_ada.reshape(depth, 1, n))


def _norm_mod(x, g, shift, scale):
    y = x * lax.rsqrt(jnp.mean(x * x, axis=-1, keepdims=True) + EPS) * g
    return y * (1.0 + scale) + shift


def _pm_kernel(x_ref, sh_ref, sc_ref, g_ref, w_ref, o_ref, h_ref):
    @pl.when(pl.program_id(1) == 0)
    def _():
        h_ref[...] = _norm_mod(x_ref[...], g_ref[...], sh_ref[0], sc_ref[0]).astype(BF16)

    o_ref[...] = _dot(h_ref[...], w_ref[...].astype(BF16)).astype(o_ref.dtype)


def _pm_swiglu_kernel(x_ref, sh_ref, sc_ref, g_ref, wg_ref, wu_ref, o_ref, h_ref):
    @pl.when(pl.program_id(1) == 0)
    def _():
        h_ref[...] = _norm_mod(x_ref[...], g_ref[...], sh_ref[0], sc_ref[0]).astype(BF16)

    h = h_ref[...]
    gate = _dot(h, wg_ref[...].astype(BF16))
    up = _dot(h, wu_ref[...].astype(BF16))
    o_ref[...] = (_silu(gate) * up).astype(o_ref.dtype)


def _row_tile(groups, cap):
    tm = cap
    while any(g % tm for g in groups):
        tm //= 2
    return tm


def _mod_row(n_ctx, dec_seq, tm, row0=0):
    def f(i):
        r = row0 + i * tm
        return jnp.where(r < n_ctx, 0, 1 + (r - n_ctx) // dec_seq)
    return f


def _pm(x, ada3, mod_cols, g, w, l, *, layout, tn, swiglu, out_dtype):
    t, d = x.shape
    n_ctx, dec_seq = layout
    tm = _row_tile((n_ctx, dec_seq), 1024)
    row = _mod_row(n_ctx, dec_seq, tm)
    sh_c, sc_c = mod_cols
    n = w.shape[2] // 2 if swiglu else w.shape[2]
    nj = n // tn
    in_specs = [
        pl.BlockSpec((tm, d), lambda i, j: (i, 0)),
        pl.BlockSpec((1, 1, d), lambda i, j: (row(i), 0, sh_c)),
        pl.BlockSpec((1, 1, d), lambda i, j: (row(i), 0, sc_c)),
        pl.BlockSpec((1, d), lambda i, j: (0, 0)),
        pl.BlockSpec((None, d, tn), lambda i, j: (l, 0, j)),
    ]
    args = [x, ada3, ada3, g.reshape(1, d), w]
    if swiglu:
        in_specs.append(pl.BlockSpec((None, d, tn), lambda i, j: (l, 0, j + nj)))
        args.append(w)
    return pl.pallas_call(
        _pm_swiglu_kernel if swiglu else _pm_kernel,
        grid=(t // tm, nj),
        in_specs=in_specs,
        out_specs=pl.BlockSpec((tm, tn), lambda i, j: (i, j)),
        out_shape=jax.ShapeDtypeStruct((t, n), out_dtype),
        scratch_shapes=[pltpu.VMEM((tm, d), BF16)],
        compiler_params=_cparams("parallel", "arbitrary"),
        name="ffn_in" if swiglu else "mix_in",
    )(*args)


def _ffn_out_kernel(a_ref, w_ref, x_ref, gate_ref, o_ref):
    y = _dot(a_ref[...], w_ref[...].astype(BF16))
    o_ref[...] = x_ref[...] + (0.5 * gate_ref[0]) * y


def _ffn_out(a, w, l, x, ada3, gate_col, *, layout):
    t, k = a.shape
    d = w.shape[2]
    n_ctx, dec_seq = layout
    tm = _row_tile((n_ctx, dec_seq), 1024)
    tn = min(256, d)
    row = _mod_row(n_ctx, dec_seq, tm)
    gpb = d // tn
    return pl.pallas_call(
        _ffn_out_kernel,
        grid=(t // tm, d // tn),
        in_specs=[
            pl.BlockSpec((tm, k), lambda i, j: (i, 0)),
            pl.BlockSpec((None, k, tn), lambda i, j: (l, 0, j)),
            pl.BlockSpec((tm, tn), lambda i, j: (i, j)),
            pl.BlockSpec((1, 1, tn), lambda i, j: (row(i), 0, gate_col * gpb + j)),
        ],
        out_specs=pl.BlockSpec((tm, tn), lambda i, j: (i, j)),
        out_shape=jax.ShapeDtypeStruct((t, d), F32),
        compiler_params=_cparams("parallel", "parallel"),
        name="ffn_out",
    )(a, w, x, ada3)


def _mix_out_kernel(a1_ref, a2_ref, a3_ref, w_ref, x_ref, gate_ref, o_ref):
    y = (_dot(a1_ref[...], w_ref[0:GDN_W, :].astype(BF16))
         + _dot(a2_ref[...], w_ref[GDN_W:GDN_W + DIFF_W, :].astype(BF16))
         + _dot(a3_ref[...], w_ref[GDN_W + DIFF_W:MIX_W, :].astype(BF16)))
    o_ref[...] = x_ref[...] + gate_ref[0] * y


def _mix_out(acts, w, l, x, ada3, gate_col, row0, *, layout):
    a1, a2, a3 = acts
    n = a1.shape[0]
    t, d = x.shape
    n_ctx, dec_seq = layout
    tm = _row_tile((n_ctx, dec_seq), 512)
    tn = min(512, d)
    row = _mod_row(n_ctx, dec_seq, tm, row0)
    rb0 = row0 // tm
    gpb = d // tn
    return pl.pallas_call(
        _mix_out_kernel,
        grid=(n // tm, d // tn),
        in_specs=[
            pl.BlockSpec((tm, GDN_W), lambda i, j: (i, 0)),
            pl.BlockSpec((tm, DIFF_W), lambda i, j: (i, 0)),
            pl.BlockSpec((tm, POOL_W), lambda i, j: (i, 0)),
            pl.BlockSpec((None, MIX_W, tn), lambda i, j: (l, 0, j)),
            pl.BlockSpec((tm, tn), lambda i, j: (rb0 + i, j)),
            pl.BlockSpec((1, 1, tn), lambda i, j: (row(i), 0, gate_col * gpb + j)),
        ],
        out_specs=pl.BlockSpec((tm, tn), lambda i, j: (rb0 + i, j)),
        out_shape=jax.ShapeDtypeStruct((t, d), F32),
        input_output_aliases={4: 0},
        compiler_params=_cparams("parallel", "parallel"),
        name="mix_out",
    )(a1, a2, a3, w, x, ada3)


def _final_norm_kernel(x_ref, g_ref, o_ref):
    x = x_ref[...]
    o_ref[...] = x * lax.rsqrt(jnp.mean(x * x, axis=-1, keepdims=True) + EPS) * g_ref[...]


def _final_norm(x, g, row0, nrows):
    d = x.shape[1]
    tm = 256
    rb0 = row0 // tm
    return pl.pallas_call(
        _final_norm_kernel,
        grid=(nrows // tm,),
        in_specs=[pl.BlockSpec((tm, d), lambda i: (rb0 + i, 0)), pl.BlockSpec((1, d), lambda i: (0, 0))],
        out_specs=pl.BlockSpec((tm, d), lambda i: (i, 0)),
        out_shape=jax.ShapeDtypeStruct((nrows, d), F32),
        compiler_params=_cparams("parallel"),
        name="final_norm",
    )(x, g.reshape(1, d))


def _shift_rows(x, off, row):
    if off == 0:
        return x
    n = x.shape[0]
    y = pltpu.roll(x, (-off) % n, 0)
    ok = (row + off >= 0) & (row + off < n)
    return jnp.where(ok, y, 0.0)


def _gdn_prep_kernel(x_ref, w_ref, o_ref):
    x = x_ref[...]
    row = lax.broadcasted_iota(jnp.int32, x.shape, 0)
    left = CONV_K // 2
    acc = jnp.zeros_like(x)
    for j in range(CONV_K):
        acc = acc + _shift_rows(x, j - left, row) * w_ref[j:j + 1, :]
    y = _silu(acc)
    kind = pl.program_id(1) // GDN_HEADS
    inv = lax.rsqrt(jnp.sum(y * y, axis=-1, keepdims=True) + EPS)
    inv = inv * jnp.where(kind == 0, GDN_DK ** -0.5, 1.0)
    o_ref[...] = jnp.where(kind < 2, y * inv, y)


def _gdn_prep(proj, conv_w, row0, nseq, seqlen):
    blk0 = row0 // seqlen
    return pl.pallas_call(
        _gdn_prep_kernel,
        grid=(nseq, 3 * GDN_HEADS),
        in_specs=[
            pl.BlockSpec((seqlen, LANE), lambda s, c: (blk0 + s, CB_Q + c)),
            pl.BlockSpec((CONV_K, LANE), lambda s, c: (0, c)),
        ],
        out_specs=pl.BlockSpec((seqlen, LANE), lambda s, c: (s, c)),
        out_shape=jax.ShapeDtypeStruct((nseq * seqlen, 3 * GDN_W), F32),
        compiler_params=_cparams("parallel", "parallel"),
        name="gdn_prep",
    )(proj, conv_w)


def _unit_tri_inverse(lmat, eye):
    x = eye - lmat
    p = _mm(lmat, lmat)
    steps = int(math.log2(CHUNK)) - 1
    for s in range(steps):
        x = x + _mm(x, p)
        if s + 1 < steps:
            p = _mm(p, p)
    return x


def _gdn_intra_kernel(q_ref, k_ref, v_ref, ab_ref, alog_ref, dt_ref,
                      u_ref, w_ref, qg_ref, kd_ref, a_ref, gl_ref):
    n = GDN_BLK
    h = pl.program_id(1)
    q = q_ref[...]
    k = k_ref[...]
    v = v_ref[...]
    ab = ab_ref[...]
    lane = lax.broadcasted_iota(jnp.int32, ab.shape, 1)
    sp = jnp.maximum(ab + dt_ref[...], 0.0) + jnp.log(1.0 + jnp.exp(-jnp.abs(ab + dt_ref[...])))
    gmat = -jnp.exp(alog_ref[...]) * sp
    bmat = _sigmoid(ab)

    ri = lax.broadcasted_iota(jnp.int32, (n, n), 0)
    ci = lax.broadcasted_iota(jnp.int32, (n, n), 1)
    same = (ri // CHUNK) == (ci // CHUNK)
    eye = (ri == ci).astype(F32)
    gtot_all = _dot_exact_lhs(same.astype(BF16), gmat)
    gcf_all = _dot_exact_lhs((same & (ci <= ri)).astype(BF16), gmat)
    kk = _mm(k, k, _dot_nt)
    qk = _mm(q, k, _dot_nt)

    for d in range(2):
        incl = same & ((ci <= ri) if d == 0 else (ci >= ri))
        strict = same & ((ci < ri) if d == 0 else (ci > ri))
        gc_all = gcf_all if d == 0 else (gtot_all - gcf_all) + gmat
        sel_g = lane == d * GDN_HEADS + h
        sel_b = lane == 2 * GDN_HEADS + d * GDN_HEADS + h
        gc = jnp.sum(jnp.where(sel_g, gc_all, 0.0), axis=1, keepdims=True)
        gtot = jnp.sum(jnp.where(sel_g, gtot_all, 0.0), axis=1, keepdims=True)
        beta = jnp.sum(jnp.where(sel_b, bmat, 0.0), axis=1, keepdims=True)
        gc_b = jnp.broadcast_to(gc, (n, LANE))
        gc_row = jnp.transpose(gc_b)[0:1, :]
        diff = jnp.where(incl, gc - gc_row, 0.0)
        decay = jnp.where(incl, jnp.exp(diff), 0.0)
        lmat = jnp.where(strict, kk * beta * decay, 0.0)
        eg = jnp.exp(gc_b)
        kb = k * beta
        rhs = jnp.concatenate([v * beta, kb * eg], axis=1)
        uw = _mm(_unit_tri_inverse(lmat, eye), rhs)
        u_ref[d] = uw[:, :GDN_DV]
        w_ref[d] = uw[:, GDN_DV:]
        qg_ref[d] = q * eg
        kd_ref[d] = k * jnp.exp(jnp.broadcast_to(gtot - gc, (n, LANE)))
        amat = qk * decay
        a_ref[d] = (amat[:, 0:CHUNK] + amat[:, CHUNK:2 * CHUNK]) + (
            amat[:, 2 * CHUNK:3 * CHUNK] + amat[:, 3 * CHUNK:4 * CHUNK])
        egt = jnp.exp(jnp.broadcast_to(gtot, (n, LANE)))
        r8 = lax.broadcasted_iota(jnp.int32, (8, LANE), 0)
        gl8 = jnp.zeros((8, LANE), F32)
        for c in range(n // CHUNK):
            gl8 = jnp.where(r8 == c, egt[c * CHUNK:c * CHUNK + 8, :], gl8)
        gl_ref[d] = gl8


def _gdn_intra(qkv, proj, alog_row, dt_row, row0):
    t = qkv.shape[0]
    nb = t // GDN_BLK
    pb0 = row0 // GDN_BLK
    big = jax.ShapeDtypeStruct((2, t, GDN_W), F32)
    big_spec = pl.BlockSpec((2, GDN_BLK, LANE), lambda i, h: (0, i, h))
    return pl.pallas_call(
        _gdn_intra_kernel,
        grid=(nb, GDN_HEADS),
        in_specs=[
            pl.BlockSpec((GDN_BLK, LANE), lambda i, h: (i, h)),
            pl.BlockSpec((GDN_BLK, LANE), lambda i, h: (i, GDN_HEADS + h)),
            pl.BlockSpec((GDN_BLK, LANE), lambda i, h: (i, 2 * GDN_HEADS + h)),
            pl.BlockSpec((GDN_BLK, LANE), lambda i, h: (pb0 + i, CB_AB)),
            pl.BlockSpec((1, LANE), lambda i, h: (0, 0)),
            pl.BlockSpec((1, LANE), lambda i, h: (0, 0)),
        ],
        out_specs=[big_spec, big_spec, big_spec, big_spec,
                   pl.BlockSpec((2, None, GDN_BLK, CHUNK), lambda i, h: (0, h, i, 0)),
                   pl.BlockSpec((2, 8, LANE), lambda i, h: (0, i, h))],
        out_shape=[big, big, big, big,
                   jax.ShapeDtypeStruct((2, GDN_HEADS, t, CHUNK), F32),
                   jax.ShapeDtypeStruct((2, nb * 8, GDN_W), F32)],
        compiler_params=_cparams("parallel", "parallel"),
        name="gdn_intra",
    )(qkv, qkv, qkv, proj, alog_row, dt_row)


def _gdn_scan_kernel(*refs, has_init):
    if has_init:
        s0_ref, u_ref, w_ref, qg_ref, kd_ref, a_ref, gl_ref, o_ref, sf_ref, s_ref = refs
    else:
        u_ref, w_ref, qg_ref, kd_ref, a_ref, gl_ref, o_ref, sf_ref, s_ref = refs
    d = pl.program_id(0)
    blk = pl.program_id(2)

    @pl.when(blk == 0)
    def _():
        if has_init:
            s_ref[...] = s0_ref[...]
        else:
            s_ref[...] = jnp.zeros_like(s_ref)

    r8 = lax.broadcasted_iota(jnp.int32, (8, LANE), 0)
    for c in range(GDN_BLK // CHUNK):
        cc = jnp.where(d == 0, c, GDN_BLK // CHUNK - 1 - c)
        rows = pl.ds(pl.multiple_of(cc * CHUNK, CHUNK), CHUNK)
        for h in range(GDN_HEADS):
            cols = slice(h * LANE, (h + 1) * LANE)
            s = s_ref[h]
            v_new = u_ref[rows, cols] - _mm(w_ref[rows, cols], s)
            o_ref[rows, cols] = _mm(qg_ref[rows, cols], s) + _mm(a_ref[h, rows, :], v_new)
            gl = jnp.sum(jnp.where(r8 == cc, gl_ref[:, cols], 0.0), axis=0, keepdims=True)
            s_ref[h] = s * gl + _mm(kd_ref[rows, cols], v_new, _dot_tn)

    @pl.when(blk == pl.num_programs(2) - 1)
    def _():
        sf_ref[...] = s_ref[...]


def _gdn_scan(s0, u, w, qg, kd, a, gl, nseq, seqlen):
    t = u.shape[1]
    nblk = seqlen // GDN_BLK

    def rb(d, s, b):
        return s * nblk + jnp.where(d == 0, b, nblk - 1 - b)

    big_spec = pl.BlockSpec((None, GDN_BLK, GDN_W), lambda d, s, b: (d, rb(d, s, b), 0))
    in_specs = [big_spec, big_spec, big_spec, big_spec,
                pl.BlockSpec((None, GDN_HEADS, GDN_BLK, CHUNK), lambda d, s, b: (d, 0, rb(d, s, b), 0)),
                pl.BlockSpec((None, 8, GDN_W), lambda d, s, b: (d, rb(d, s, b), 0))]
    args = [u, w, qg, kd, a, gl]
    state_spec = pl.BlockSpec((None, None, GDN_HEADS, GDN_DK, GDN_DV), lambda d, s, b: (s, d, 0, 0, 0))
    if s0 is not None:
        in_specs.insert(0, state_spec)
        args.insert(0, s0)
    return pl.pallas_call(
        functools.partial(_gdn_scan_kernel, has_init=s0 is not None),
        grid=(2, nseq, nblk),
        in_specs=in_specs,
        out_specs=[big_spec, state_spec],
        out_shape=[jax.ShapeDtypeStruct((2, t, GDN_W), F32),
                   jax.ShapeDtypeStruct((nseq, 2, GDN_HEADS, GDN_DK, GDN_DV), F32)],
        scratch_shapes=[pltpu.VMEM((GDN_HEADS, GDN_DK, GDN_DV), F32)],
        compiler_params=_cparams("parallel", "parallel", "arbitrary"),
        name="gdn_scan",
    )(*args)


def _gdn_post_kernel(o_ref, z_ref, g_ref, out_ref):
    for h in range(GDN_HEADS):
        cols = slice(h * LANE, (h + 1) * LANE)
        o = o_ref[0, :, cols] + o_ref[1, :, cols]
        y = o * lax.rsqrt(jnp.mean(o * o, axis=-1, keepdims=True) + EPS) * g_ref[...]
        out_ref[:, cols] = (y * _silu(z_ref[:, cols])).astype(out_ref.dtype)


def _gdn_post(o, proj, g, row0):
    t = o.shape[1]
    tm = 256
    pb0 = row0 // tm
    return pl.pallas_call(
        _gdn_post_kernel,
        grid=(t // tm,),
        in_specs=[
            pl.BlockSpec((2, tm, GDN_W), lambda i: (0, i, 0)),
            pl.BlockSpec((tm, GDN_W), lambda i: (pb0 + i, CB_Z // GDN_HEADS)),
            pl.BlockSpec((1, GDN_DV), lambda i: (0, 0)),
        ],
        out_specs=pl.BlockSpec((tm, GDN_W), lambda i: (i, 0)),
        out_shape=jax.ShapeDtypeStruct((t, GDN_W), BF16),
        compiler_params=_cparams("parallel"),
        name="gdn_post",
    )(o, proj, g.reshape(1, GDN_DV))


def _rope_kernel(x_ref, cos_ref, sin_ref, o_ref):
    cos = cos_ref[...]
    sin = sin_ref[...]
    lane = lax.broadcasted_iota(jnp.int32, cos.shape, 1)
    first = (lane % 32) < 16
    for hb in range(DIFF_HEADS):
        cols = slice(hb * LANE, (hb + 1) * LANE)
        x = x_ref[:, cols]
        partner = jnp.where(first, pltpu.roll(x, LANE - 16, 1), pltpu.roll(x, 16, 1))
        o_ref[:, cols] = x * cos + partner * sin


def _rope_tables(seqlen):
    pos = jnp.arange(seqlen)
    rowp = (pos // GRID_W).astype(F32)
    colp = (pos % GRID_W).astype(F32)
    half = DIFF_DQK // 2
    inv_freq = ROPE_BASE ** (-jnp.arange(0, half, 2, dtype=F32) / half)
    c = jnp.arange(LANE)
    c64 = c % DIFF_DQK
    p = jnp.where((c64 // half)[None, :] == 0, rowp[:, None], colp[:, None])
    ang = p * inv_freq[c64 % (half // 2)][None, :]
    sign = jnp.where((c64 % half) < half // 2, -1.0, 1.0).astype(F32)
    return jnp.cos(ang), jnp.sin(ang) * sign[None, :]


def _rope(proj, cos, sin, row0, nrows, seqlen):
    tm = 256
    pb0 = row0 // tm
    nsb = seqlen // tm
    return pl.pallas_call(
        _rope_kernel,
        grid=(2, nrows // tm),
        in_specs=[
            pl.BlockSpec((tm, DIFF_W), lambda j, i: (pb0 + i, CB_DQ // DIFF_HEADS + j)),
            pl.BlockSpec((tm, LANE), lambda j, i: (i % nsb, 0)),
            pl.BlockSpec((tm, LANE), lambda j, i: (i % nsb, 0)),
        ],
        out_specs=pl.BlockSpec((None, tm, DIFF_W), lambda j, i: (j, i, 0)),
        out_shape=jax.ShapeDtypeStruct((2, nrows, DIFF_W), F32),
        compiler_params=_cparams("parallel", "parallel"),
        name="rope",
    )(proj, cos, sin)


def _diff_attn_kernel(*refs, seg_lens, lam_init):
    nseg = len(seg_lens)
    q_ref = refs[0]
    k_refs = refs[1:1 + nseg]
    v_refs = refs[1 + nseg:1 + 2 * nseg]
    lam_ref, g_ref, o_ref, kb_ref, vb_ref = refs[1 + 2 * nseg:]
    offs = [sum(seg_lens[:s]) for s in range(nseg)]

    @pl.when(pl.program_id(2) == 0)
    def _():
        for s in range(nseg):
            kb_ref[offs[s]:offs[s] + seg_lens[s], :] = k_refs[s][...].astype(BF16)
            vb_ref[offs[s]:offs[s] + seg_lens[s], :] = v_refs[s][...].astype(BF16)

    lv = lam_ref[...]
    lam = (jnp.exp(jnp.sum(lv[0:1] * lv[1:2], axis=1, keepdims=True))
           - jnp.exp(jnp.sum(lv[2:3] * lv[3:4], axis=1, keepdims=True)) + lam_init)
    q = q_ref[...] * (DIFF_DQK ** -0.5)
    lane = lax.broadcasted_iota(jnp.int32, q.shape, 1)
    kb = kb_ref[...]
    es, rs = [], []
    for m in range(2):
        qm = jnp.where((lane // DIFF_DQK) == m, q, 0.0).astype(BF16)
        s = _dot_nt(qm, kb)
        e = jnp.exp(s - jnp.max(s, axis=1, keepdims=True))
        es.append(e)
        rs.append(1.0 / jnp.sum(e, axis=1, keepdims=True))
    pd = es[0] * rs[0] - es[1] * (lam * rs[1])
    o = _dot(pd.astype(BF16), vb_ref[...])
    y = o * lax.rsqrt(jnp.mean(o * o, axis=-1, keepdims=True) + EPS) * g_ref[...]
    o_ref[...] = (y * (1.0 - lam_init)).astype(o_ref.dtype)


def _diff_attn(q_arr, q_spec, k_arrs, k_specs, v_arrs, v_specs, seg_lens, lam_vec, g, nseq, seqlen, tq,
               lam_init):
    nqb = seqlen // tq
    lk = sum(seg_lens)
    return pl.pallas_call(
        functools.partial(_diff_attn_kernel, seg_lens=tuple(seg_lens), lam_init=lam_init),
        grid=(nseq, DIFF_HEADS, nqb),
        in_specs=[q_spec, *k_specs, *v_specs,
                  pl.BlockSpec((4, DIFF_DQK), lambda b, h, i: (0, 0)),
                  pl.BlockSpec((1, DIFF_DV), lambda b, h, i: (0, 0))],
        out_specs=pl.BlockSpec((tq, DIFF_DV), lambda b, h, i: (b * nqb + i, h)),
        out_shape=jax.ShapeDtypeStruct((nseq * seqlen, DIFF_W), BF16),
        scratch_shapes=[pltpu.VMEM((lk, DIFF_DV), BF16), pltpu.VMEM((lk, DIFF_DV), BF16)],
        compiler_params=_cparams("parallel", "parallel", "arbitrary"),
        name="diff_attn",
    )(q_arr, *k_arrs, *v_arrs, lam_vec, g.reshape(1, DIFF_DV))


def _pool_kernel(x_ref, w_ref, sc_ref, o_ref):
    n = x_ref.shape[0]
    row = lax.broadcasted_iota(jnp.int32, (n, POOL_GW), 0)
    for gi, win in enumerate(POOL_WINDOWS):
        cols = slice(gi * POOL_GW, (gi + 1) * POOL_GW)
        x = x_ref[:, cols]
        a = win // 2
        b = win - a - 1
        acc = x
        for off in range(-a, b + 1):
            if off != 0:
                acc = acc + _shift_rows(x, off, row)
        cnt = (jnp.minimum(row + b + 1, n) - jnp.maximum(row - a, 0)).astype(F32)
        pooled = acc / cnt - x
        y = _dot(pooled.astype(BF16), w_ref[gi].astype(BF16))
        o_ref[:, cols] = (y * sc_ref[:, cols]).astype(o_ref.dtype)


def _pool(proj, pool_w, pool_scale, row0, nseq, seqlen):
    blk0 = row0 // seqlen
    return pl.pallas_call(
        _pool_kernel,
        grid=(nseq,),
        in_specs=[
            pl.BlockSpec((seqlen, POOL_W), lambda s: (blk0 + s, CB_PIN // len(POOL_WINDOWS))),
            pl.BlockSpec((len(POOL_WINDOWS), POOL_GW, POOL_GW), lambda s: (0, 0, 0)),
            pl.BlockSpec((1, POOL_W), lambda s: (0, 0)),
        ],
        out_specs=pl.BlockSpec((seqlen, POOL_W), lambda s: (s, 0)),
        out_shape=jax.ShapeDtypeStruct((nseq * seqlen, POOL_W), BF16),
        compiler_params=_cparams("parallel"),
        name="pool",
    )(proj, pool_w, pool_scale.reshape(1, POOL_W))


def _mix_group(proj, lp, l, row0, nseq, seqlen, ctx):
    nrows = nseq * seqlen
    qkv = _gdn_prep(proj, lp["gdn_conv"], row0, nseq, seqlen)
    u, w, qg, kd, a, gl = _gdn_intra(qkv, proj, lp["alog_row"], lp["dt_row"], row0)
    o, s_fin = _gdn_scan(None if ctx is None else ctx[2], u, w, qg, kd, a, gl, nseq, seqlen)
    o_gdn = _gdn_post(o, proj, lp["gdn_norm"], row0)

    tq = 256
    nqb = seqlen // tq
    qb0 = row0 // tq
    sb0 = row0 // seqlen
    if ctx is None:
        q_arr = proj
        q_spec = pl.BlockSpec((tq, DIFF_DV), lambda b, h, i: (qb0 + b * nqb + i, CB_DQ + h))
        k_arrs = [proj]
        k_specs = [pl.BlockSpec((seqlen, DIFF_DV), lambda b, h, i: (sb0 + b, CB_DK + h))]
        v_arrs = [proj]
        v_specs = [pl.BlockSpec((seqlen, DIFF_DV), lambda b, h, i: (sb0 + b, CB_DV + h))]
        seg_lens = [seqlen]
    else:
        cache_k, cache_v = ctx[0], ctx[1]
        past = cache_k.shape[2]
        cos, sin = _rope_tables(seqlen)
        qk_rot = _rope(proj, cos, sin, row0, nrows, seqlen)
        q_arr = qk_rot
        q_spec = pl.BlockSpec((None, tq, DIFF_DV), lambda b, h, i: (0, b * nqb + i, h))
        k_arrs = [cache_k, qk_rot]
        k_specs = [pl.BlockSpec((None, None, past, DIFF_DV), lambda b, h, i: (b, l, 0, h)),
                   pl.BlockSpec((None, seqlen, DIFF_DV), lambda b, h, i: (1, b, h))]
        v_arrs = [cache_v, proj]
        v_specs = [pl.BlockSpec((None, None, past, DIFF_DV), lambda b, h, i: (b, l, 0, h)),
                   pl.BlockSpec((seqlen, DIFF_DV), lambda b, h, i: (sb0 + b, CB_DV + h))]
        seg_lens = [past, seqlen]
    o_diff = _diff_attn(q_arr, q_spec, k_arrs, k_specs, v_arrs, v_specs, seg_lens, lp["diff_lam"],
                        lp["diff_norm"], nseq, seqlen, tq, _lambda_init(l))

    o_pool = _pool(proj, lp["pool_w"], lp["pool_scale"], row0, nseq, seqlen)
    return (o_gdn, o_diff, o_pool), s_fin


def _reorder_w_in(w_in):
    depth, d, _ = w_in.shape
    o_ab = 4 * GDN_W
    o_dq = o_ab + 4 * GDN_HEADS
    pad = jnp.zeros((depth, d, LANE - 4 * GDN_HEADS), w_in.dtype)
    return jnp.concatenate([w_in[..., :o_ab], w_in[..., o_dq:], w_in[..., o_ab:o_dq], pad], axis=-1)


def kernel(x_prompt, x_sample, c, cache_k, cache_v, state_gdn, c_ctx, w_ada, b_ada, norm_ffn1, ffn1_in,
           ffn1_out, norm_mix, w_in, gdn_conv, gdn_a_log, gdn_dt_bias, gdn_norm, diff_lam, diff_norm,
           pool_w, pool_scale, w_out, norm_ffn2, ffn2_in, ffn2_out, final_norm):
    batch, seq, d = x_prompt.shape
    dec_batch, dec_seq, _ = x_sample.shape
    depth = w_ada.shape[0]
    d_ff = ffn1_out.shape[1]
    n_ctx = batch * seq
    n_dec = dec_batch * dec_seq
    layout = (n_ctx, dec_seq)
    past = cache_k.shape[2]

    x = jnp.concatenate([x_prompt.reshape(n_ctx, d), x_sample.reshape(n_dec, d)], axis=0)
    cond8 = jnp.concatenate([c_ctx[None, :], c, jnp.zeros((8 - 1 - dec_batch, d), F32)], axis=0)
    ada = _ada(cond8, w_ada, b_ada)
    cache_k4 = cache_k.reshape(dec_batch, depth, past, DIFF_W)
    cache_v4 = cache_v.reshape(dec_batch, depth, past, DIFF_W)
    w_in_r = _reorder_w_in(w_in).astype(BF16)
    ff_tn = 512 if d_ff % 512 == 0 else d_ff

    ks, vs, ss = [], [], []
    for l in range(depth):
        ada3 = ada[l, :1 + dec_batch].reshape(1 + dec_batch, 1, N_MOD * d)
        lane_pad = jnp.zeros((LANE - 2 * GDN_HEADS,), F32)
        lp = dict(
            gdn_conv=gdn_conv[l], gdn_norm=gdn_norm[l], diff_lam=diff_lam[l], diff_norm=diff_norm[l],
            pool_w=pool_w[l], pool_scale=pool_scale[l],
            alog_row=jnp.concatenate([gdn_a_log[l].reshape(-1), lane_pad]).reshape(1, LANE),
            dt_row=jnp.concatenate([gdn_dt_bias[l].reshape(-1), lane_pad]).reshape(1, LANE),
        )
        act = _pm(x, ada3, (0, 1), norm_ffn1[l], ffn1_in, l, layout=layout, tn=ff_tn, swiglu=True,
                  out_dtype=BF16)
        x = _ffn_out(act, ffn1_out, l, x, ada3, 2, layout=layout)

        proj = _pm(x, ada3, (3, 4), norm_mix[l], w_in_r, l, layout=layout, tn=640, swiglu=False,
                   out_dtype=F32)
        acts_c, s_c = _mix_group(proj, lp, l, 0, batch, seq, None)
        acts_d, _ = _mix_group(proj, lp, l, n_ctx, dec_batch, dec_seq,
                               (cache_k4, cache_v4, state_gdn[:, l]))
        x = _mix_out(acts_c, w_out, l, x, ada3, 5, 0, layout=layout)
        x = _mix_out(acts_d, w_out, l, x, ada3, 5, n_ctx, layout=layout)
        ks.append(proj[:n_ctx, CB_DK * LANE:CB_DK * LANE + DIFF_W].reshape(batch, seq, DIFF_HEADS, DIFF_DV))
        vs.append(proj[:n_ctx, CB_DV * LANE:CB_DV * LANE + DIFF_W].reshape(batch, seq, DIFF_HEADS, DIFF_DV))
        ss.append(s_c)

        act = _pm(x, ada3, (6, 7), norm_ffn2[l], ffn2_in, l, layout=layout, tn=ff_tn, swiglu=True,
                  out_dtype=BF16)
        x = _ffn_out(act, ffn2_out, l, x, ada3, 8, layout=layout)

    y_prompt = _final_norm(x, final_norm, 0, n_ctx).reshape(batch, seq, d)
    y_sample = _final_norm(x, final_norm, n_ctx, n_dec).reshape(dec_batch, dec_seq, d)
    return (y_prompt, y_sample, jnp.stack(ks, axis=1), jnp.stack(vs, axis=1), jnp.stack(ss, axis=1))
```

```python
import functools
import math

import jax
import jax.numpy as jnp
from jax import lax
from jax.experimental import pallas as pl
from jax.experimental.pallas import tpu as pltpu

F32 = jnp.float32
BF16 = jnp.bfloat16

GRID_W = 64
GDN_HEADS = 4
GDN_DK = 128
GDN_DV = 128
GDN_W = GDN_HEADS * GDN_DV
CONV_K = 4
CHUNK = 64
DIFF_HEADS = 8
DIFF_DQK = 64
DIFF_DV = 2 * DIFF_DQK
DIFF_W = DIFF_HEADS * DIFF_DV
ROPE_BASE = 10000.0
POOL_WINDOWS = (2, 4, 8, 16)
POOL_GW = 128
POOL_W = len(POOL_WINDOWS) * POOL_GW
MIX_W = GDN_W + DIFF_W + POOL_W
N_MOD = 9
EPS = 1e-6

LANE = 128
GDN_BLK = 4 * CHUNK
ATTN_SUB = 256
VMEM_LIMIT = 56 * 1024 * 1024

CB_Q, CB_K, CB_V, CB_Z = 0, 4, 8, 12
CB_DQ, CB_DK, CB_DV, CB_PIN = 16, 24, 32, 40
PROJ_MAIN = 44 * LANE


def _lambda_init(l):
    return 0.8 - 0.6 * math.exp(-0.3 * l)


def _cparams(*sem):
    return pltpu.CompilerParams(dimension_semantics=sem, vmem_limit_bytes=VMEM_LIMIT)


def _dot(a, b):
    return lax.dot_general(a, b, (((1,), (0,)), ((), ())), preferred_element_type=F32)


def _dot_nt(a, b):
    return lax.dot_general(a, b, (((1,), (1,)), ((), ())), preferred_element_type=F32)


def _dot_tn(a, b):
    return lax.dot_general(a, b, (((0,), (0,)), ((), ())), preferred_element_type=F32)


def _mm(a, b, dot=_dot):
    return dot(a.astype(BF16), b.astype(BF16))


def _dot_exact_lhs(m, b):
    b0 = b.astype(BF16)
    r = b - b0.astype(F32)
    b1 = r.astype(BF16)
    b2 = (r - b1.astype(F32)).astype(BF16)
    return _dot(m, b0) + (_dot(m, b1) + _dot(m, b2))


def _sigmoid(x):
    return 0.5 * (1.0 + jnp.tanh(0.5 * x))


def _silu(x):
    return x * _sigmoid(x)


def _ada_kernel(cond_ref, w_ref, b_ref, o_ref):
    s = _silu(cond_ref[...]).astype(BF16)
    o_ref[...] = _dot(s, w_ref[...].astype(BF16)) + b_ref[...]


def _ada(cond8, w_ada, b_ada):
    depth, d, n = w_ada.shape
    tn = 1024
    return pl.pallas_call(
        _ada_kernel,
        grid=(depth, n // tn),
        in_specs=[
            pl.BlockSpec((8, d), lambda l, j: (0, 0)),
            pl.BlockSpec((None, d, tn), lambda l, j: (l, 0, j)),
            pl.BlockSpec((None, 1, tn), lambda l, j: (l, 0, j)),
        ],
        out_specs=pl.BlockSpec((None, 8, tn), lambda l, j: (l, 0, j)),
        out_shape=jax.ShapeDtypeStruct((depth, 8, n), F32),
        compiler_params=_cparams("parallel", "parallel"),
        name="ada",
    )(cond8, w_ada, b_ada.reshape(depth, 1, n))


def _norm_mod(x, g, shift, scale):
    y = x * lax.rsqrt(jnp.mean(x * x, axis=-1, keepdims=True) + EPS) * g
    return y * (1.0 + scale) + shift


def _mix_in_kernel(x_ref, sh_ref, sc_ref, g_ref, w_ref, wab_ref, o_ref, ab_ref, h_ref):
    @pl.when(pl.program_id(1) == 0)
    def _():
        h = _norm_mod(x_ref[...], g_ref[...], sh_ref[0], sc_ref[0]).astype(BF16)
        h_ref[...] = h
        ab_ref[...] = _dot(h, wab_ref[...])

    o_ref[...] = _dot(h_ref[...], w_ref[...])


def _ffn_in_kernel(x_ref, sh_ref, sc_ref, g_ref, wg_ref, wu_ref, o_ref, h_ref):
    @pl.when(pl.program_id(1) == 0)
    def _():
        h_ref[...] = _norm_mod(x_ref[...], g_ref[...], sh_ref[0], sc_ref[0]).astype(BF16)

    h = h_ref[...]
    gate = _dot(h, wg_ref[...].astype(BF16))
    up = _dot(h, wu_ref[...].astype(BF16))
    o_ref[...] = (_silu(gate) * up).astype(o_ref.dtype)


def _row_tile(groups, cap):
    tm = cap
    while any(g % tm for g in groups):
        tm //= 2
    return tm


def _mod_row(n_ctx, dec_seq, tm, row0=0):
    def f(i):
        r = row0 + i * tm
        return jnp.where(r < n_ctx, 0, 1 + (r - n_ctx) // dec_seq)
    return f


def _pm_specs(d, tm, row, mod_cols):
    sh_c, sc_c = mod_cols
    return [
        pl.BlockSpec((tm, d), lambda i, j: (i, 0)),
        pl.BlockSpec((1, 1, d), lambda i, j: (row(i), 0, sh_c)),
        pl.BlockSpec((1, 1, d), lambda i, j: (row(i), 0, sc_c)),
        pl.BlockSpec((1, d), lambda i, j: (0, 0)),
    ]


def _ffn_in(x, ada3, mod_cols, g, w, l, *, layout):
    t, d = x.shape
    n_ctx, dec_seq = layout
    tm = _row_tile((n_ctx, dec_seq), 1024)
    row = _mod_row(n_ctx, dec_seq, tm)
    n = w.shape[2] // 2
    tn = 512 if n % 512 == 0 else n
    nj = n // tn
    return pl.pallas_call(
        _ffn_in_kernel,
        grid=(t // tm, nj),
        in_specs=_pm_specs(d, tm, row, mod_cols) + [
            pl.BlockSpec((None, d, tn), lambda i, j: (l, 0, j)),
            pl.BlockSpec((None, d, tn), lambda i, j: (l, 0, j + nj)),
        ],
        out_specs=pl.BlockSpec((tm, tn), lambda i, j: (i, j)),
        out_shape=jax.ShapeDtypeStruct((t, n), BF16),
        scratch_shapes=[pltpu.VMEM((tm, d), BF16)],
        compiler_params=_cparams("parallel", "arbitrary"),
        name="ffn_in",
    )(x, ada3, ada3, g.reshape(1, d), w, w)


def _mix_in(x, ada3, mod_cols, g, w_main, w_ab, l, *, layout):
    t, d = x.shape
    n_ctx, dec_seq = layout
    tm = _row_tile((n_ctx, dec_seq), 1024)
    row = _mod_row(n_ctx, dec_seq, tm)
    tn = 512
    return pl.pallas_call(
        _mix_in_kernel,
        grid=(t // tm, PROJ_MAIN // tn),
        in_specs=_pm_specs(d, tm, row, mod_cols) + [
            pl.BlockSpec((None, d, tn), lambda i, j: (l, 0, j)),
            pl.BlockSpec((None, d, LANE), lambda i, j: (l, 0, 0)),
        ],
        out_specs=[pl.BlockSpec((tm, tn), lambda i, j: (i, j)),
                   pl.BlockSpec((tm, LANE), lambda i, j: (i, 0))],
        out_shape=[jax.ShapeDtypeStruct((t, PROJ_MAIN), F32), jax.ShapeDtypeStruct((t, LANE), F32)],
        scratch_shapes=[pltpu.VMEM((tm, d), BF16)],
        compiler_params=_cparams("parallel", "arbitrary"),
        name="mix_in",
    )(x, ada3, ada3, g.reshape(1, d), w_main, w_ab)


def _ffn_out_kernel(a_ref, w_ref, x_ref, gate_ref, o_ref):
    y = _dot(a_ref[...], w_ref[...].astype(BF16))
    o_ref[...] = x_ref[...] + (0.5 * gate_ref[0]) * y


def _ffn_out(a, w, l, x, ada3, gate_col, *, layout):
    t, k = a.shape
    d = w.shape[2]
    n_ctx, dec_seq = layout
    tm = _row_tile((n_ctx, dec_seq), 1024)
    tn = min(256, d)
    row = _mod_row(n_ctx, dec_seq, tm)
    gpb = d // tn
    return pl.pallas_call(
        _ffn_out_kernel,
        grid=(t // tm, d // tn),
        in_specs=[
            pl.BlockSpec((tm, k), lambda i, j: (i, 0)),
            pl.BlockSpec((None, k, tn), lambda i, j: (l, 0, j)),
            pl.BlockSpec((tm, tn), lambda i, j: (i, j)),
            pl.BlockSpec((1, 1, tn), lambda i, j: (row(i), 0, gate_col * gpb + j)),
        ],
        out_specs=pl.BlockSpec((tm, tn), lambda i, j: (i, j)),
        out_shape=jax.ShapeDtypeStruct((t, d), F32),
        compiler_params=_cparams("parallel", "parallel"),
        name="ffn_out",
    )(a, w, x, ada3)


def _mix_out_kernel(a1_ref, a2_ref, a3_ref, w_ref, x_ref, gate_ref, o_ref):
    y = (_dot(a1_ref[...], w_ref[0:GDN_W, :])
         + _dot(a2_ref[...], w_ref[GDN_W:GDN_W + DIFF_W, :])
         + _dot(a3_ref[...], w_ref[GDN_W + DIFF_W:MIX_W, :]))
    o_ref[...] = x_ref[...] + gate_ref[0] * y


def _mix_out(acts, w, l, x, ada3, gate_col, row0, *, layout):
    a1, a2, a3 = acts
    n = a1.shape[0]
    t, d = x.shape
    n_ctx, dec_seq = layout
    tm = _row_tile((n_ctx, dec_seq), 512)
    tn = min(512, d)
    row = _mod_row(n_ctx, dec_seq, tm, row0)
    rb0 = row0 // tm
    gpb = d // tn
    return pl.pallas_call(
        _mix_out_kernel,
        grid=(n // tm, d // tn),
        in_specs=[
            pl.BlockSpec((tm, GDN_W), lambda i, j: (i, 0)),
            pl.BlockSpec((tm, DIFF_W), lambda i, j: (i, 0)),
            pl.BlockSpec((tm, POOL_W), lambda i, j: (i, 0)),
            pl.BlockSpec((None, MIX_W, tn), lambda i, j: (l, 0, j)),
            pl.BlockSpec((tm, tn), lambda i, j: (rb0 + i, j)),
            pl.BlockSpec((1, 1, tn), lambda i, j: (row(i), 0, gate_col * gpb + j)),
        ],
        out_specs=pl.BlockSpec((tm, tn), lambda i, j: (rb0 + i, j)),
        out_shape=jax.ShapeDtypeStruct((t, d), F32),
        input_output_aliases={4: 0},
        compiler_params=_cparams("parallel", "parallel"),
        name="mix_out",
    )(a1, a2, a3, w, x, ada3)


def _final_norm_kernel(x_ref, g_ref, o_ref):
    x = x_ref[...]
    o_ref[...] = x * lax.rsqrt(jnp.mean(x * x, axis=-1, keepdims=True) + EPS) * g_ref[...]


def _final_norm(x, g, row0, nrows):
    d = x.shape[1]
    tm = 256
    rb0 = row0 // tm
    return pl.pallas_call(
        _final_norm_kernel,
        grid=(nrows // tm,),
        in_specs=[pl.BlockSpec((tm, d), lambda i: (rb0 + i, 0)), pl.BlockSpec((1, d), lambda i: (0, 0))],
        out_specs=pl.BlockSpec((tm, d), lambda i: (i, 0)),
        out_shape=jax.ShapeDtypeStruct((nrows, d), F32),
        compiler_params=_cparams("parallel"),
        name="final_norm",
    )(x, g.reshape(1, d))


def _shift_rows(x, off, row):
    if off == 0:
        return x
    n = x.shape[0]
    y = pltpu.roll(x, (-off) % n, 0)
    ok = (row + off >= 0) & (row + off < n)
    return jnp.where(ok, y, 0.0)


def _gdn_prep_kernel(x_ref, w_ref, o_ref):
    x = x_ref[...]
    n = x.shape[0]
    left = CONV_K // 2
    edge = 16

    def conv(xb, shift):
        acc = jnp.zeros_like(xb)
        for j in range(CONV_K):
            acc = acc + shift(xb, j - left) * w_ref[j:j + 1, :]
        return acc

    acc = conv(x, lambda xb, off: xb if off == 0 else pltpu.roll(xb, (-off) % n, 0))
    row = lax.broadcasted_iota(jnp.int32, (edge, LANE), 0)
    head = conv(x[0:edge], lambda xb, off: _shift_rows(xb, off, row))
    tail = conv(x[n - edge:n], lambda xb, off: _shift_rows(xb, off, row))
    acc = jnp.concatenate([head[0:8], acc[8:n - 8], tail[edge - 8:edge]], axis=0)
    y = _silu(acc)
    kind = pl.program_id(1) // GDN_HEADS
    inv = lax.rsqrt(jnp.sum(y * y, axis=-1, keepdims=True) + EPS)
    inv = inv * jnp.where(kind == 0, GDN_DK ** -0.5, 1.0)
    o_ref[...] = jnp.where(kind < 2, y * inv, y)


def _gdn_prep(proj, conv_w, row0, nseq, seqlen):
    blk0 = row0 // seqlen
    return pl.pallas_call(
        _gdn_prep_kernel,
        grid=(nseq, 3 * GDN_HEADS),
        in_specs=[
            pl.BlockSpec((seqlen, LANE), lambda s, c: (blk0 + s, CB_Q + c)),
            pl.BlockSpec((CONV_K, LANE), lambda s, c: (0, c)),
        ],
        out_specs=pl.BlockSpec((seqlen, LANE), lambda s, c: (s, c)),
        out_shape=jax.ShapeDtypeStruct((nseq * seqlen, 3 * GDN_W), F32),
        compiler_params=_cparams("parallel", "parallel"),
        name="gdn_prep",
    )(proj, conv_w)


def _unit_tri_solve_many(lmats, rhss, eye):
    xs = [eye - l for l in lmats]
    ps = [_mm(l, l) for l in lmats]
    steps = int(math.log2(CHUNK)) - 1
    for s in range(steps):
        xs = [x + _mm(x, p) for x, p in zip(xs, ps)]
        if s + 1 < steps:
            ps = [_mm(p, p) for p in ps]
    return [_mm(x, r) for x, r in zip(xs, rhss)]


def _gdn_intra_kernel(q_ref, k_ref, v_ref, ab_ref, alog_ref, dt_ref,
                      u_ref, w_ref, qg_ref, kd_ref, a_ref, gl_ref):
    n = GDN_BLK
    ab = ab_ref[...]
    lane = lax.broadcasted_iota(jnp.int32, ab.shape, 1)
    sp = jnp.maximum(ab + dt_ref[...], 0.0) + jnp.log(1.0 + jnp.exp(-jnp.abs(ab + dt_ref[...])))
    gmat = -jnp.exp(alog_ref[...]) * sp
    bmat = _sigmoid(ab)

    ri = lax.broadcasted_iota(jnp.int32, (n, n), 0)
    ci = lax.broadcasted_iota(jnp.int32, (n, n), 1)
    same = (ri // CHUNK) == (ci // CHUNK)
    eye = (ri == ci).astype(F32)
    r8 = lax.broadcasted_iota(jnp.int32, (8, LANE), 0)
    gtot_all = _dot_exact_lhs(same.astype(BF16), gmat)
    gcf_all = _dot_exact_lhs((same & (ci <= ri)).astype(BF16), gmat)
    gc_dirs = (gcf_all, (gtot_all - gcf_all) + gmat)
    incl_dirs = (same & (ci <= ri), same & (ci >= ri))
    strict_dirs = (same & (ci < ri), same & (ci > ri))

    lmats, rhss, dests = [], [], []
    for h in range(GDN_HEADS):
        cols = slice(h * LANE, (h + 1) * LANE)
        q = q_ref[:, cols]
        k = k_ref[:, cols]
        v = v_ref[:, cols]
        kk = _mm(k, k, _dot_nt)
        qk = _mm(q, k, _dot_nt)
        for d in range(2):
            incl = incl_dirs[d]
            sel_g = lane == d * GDN_HEADS + h
            sel_b = lane == 2 * GDN_HEADS + d * GDN_HEADS + h
            gc = jnp.sum(jnp.where(sel_g, gc_dirs[d], 0.0), axis=1, keepdims=True)
            gtot = jnp.sum(jnp.where(sel_g, gtot_all, 0.0), axis=1, keepdims=True)
            beta = jnp.sum(jnp.where(sel_b, bmat, 0.0), axis=1, keepdims=True)
            gc_b = jnp.broadcast_to(gc, (n, LANE))
            gc_row = jnp.transpose(gc_b)[0:1, :]
            diff = jnp.where(incl, gc - gc_row, 0.0)
            decay = jnp.where(incl, jnp.exp(diff), 0.0)
            eg = jnp.exp(gc_b)
            lmats.append(jnp.where(strict_dirs[d], kk * beta * decay, 0.0))
            rhss.append(jnp.concatenate([v * beta, (k * beta) * eg], axis=1))
            dests.append((d, cols))
            qg_ref[d, :, cols] = q * eg
            kd_ref[d, :, cols] = k * jnp.exp(jnp.broadcast_to(gtot - gc, (n, LANE)))
            amat = qk * decay
            a_ref[d, h] = (amat[:, 0:CHUNK] + amat[:, CHUNK:2 * CHUNK]) + (
                amat[:, 2 * CHUNK:3 * CHUNK] + amat[:, 3 * CHUNK:4 * CHUNK])
            egt = jnp.exp(jnp.broadcast_to(gtot, (n, LANE)))
            gl8 = jnp.zeros((8, LANE), F32)
            for c in range(n // CHUNK):
                gl8 = jnp.where(r8 == c, egt[c * CHUNK:c * CHUNK + 8, :], gl8)
            gl_ref[d, :, cols] = gl8

    for (d, cols), uw in zip(dests, _unit_tri_solve_many(lmats, rhss, eye)):
        u_ref[d, :, cols] = uw[:, :GDN_DV]
        w_ref[d, :, cols] = uw[:, GDN_DV:]


def _gdn_intra(qkv, proj_ab, alog_row, dt_row, row0):
    t = qkv.shape[0]
    nb = t // GDN_BLK
    pb0 = row0 // GDN_BLK
    big = jax.ShapeDtypeStruct((2, t, GDN_W), F32)
    big_spec = pl.BlockSpec((2, GDN_BLK, GDN_W), lambda i: (0, i, 0))
    return pl.pallas_call(
        _gdn_intra_kernel,
        grid=(nb,),
        in_specs=[
            pl.BlockSpec((GDN_BLK, GDN_W), lambda i: (i, 0)),
            pl.BlockSpec((GDN_BLK, GDN_W), lambda i: (i, 1)),
            pl.BlockSpec((GDN_BLK, GDN_W), lambda i: (i, 2)),
            pl.BlockSpec((GDN_BLK, LANE), lambda i: (pb0 + i, 0)),
            pl.BlockSpec((1, LANE), lambda i: (0, 0)),
            pl.BlockSpec((1, LANE), lambda i: (0, 0)),
        ],
        out_specs=[big_spec, big_spec, big_spec, big_spec,
                   pl.BlockSpec((2, GDN_HEADS, GDN_BLK, CHUNK), lambda i: (0, 0, i, 0)),
                   pl.BlockSpec((2, 8, GDN_W), lambda i: (0, i, 0))],
        out_shape=[big, big, big, big,
                   jax.ShapeDtypeStruct((2, GDN_HEADS, t, CHUNK), F32),
                   jax.ShapeDtypeStruct((2, nb * 8, GDN_W), F32)],
        compiler_params=_cparams("parallel"),
        name="gdn_intra",
    )(qkv, qkv, qkv, proj_ab, alog_row, dt_row)


def _gdn_scan_kernel(*refs, has_init):
    if has_init:
        s0_ref, u_ref, w_ref, qg_ref, kd_ref, a_ref, gl_ref, o_ref, sf_ref, s_ref = refs
    else:
        u_ref, w_ref, qg_ref, kd_ref, a_ref, gl_ref, o_ref, sf_ref, s_ref = refs
    d = pl.program_id(0)
    blk = pl.program_id(2)

    @pl.when(blk == 0)
    def _():
        if has_init:
            s_ref[...] = s0_ref[...]
        else:
            s_ref[...] = jnp.zeros_like(s_ref)

    r8 = lax.broadcasted_iota(jnp.int32, (8, LANE), 0)
    for c in range(GDN_BLK // CHUNK):
        cc = jnp.where(d == 0, c, GDN_BLK // CHUNK - 1 - c)
        rows = pl.ds(pl.multiple_of(cc * CHUNK, CHUNK), CHUNK)
        for h in range(GDN_HEADS):
            cols = slice(h * LANE, (h + 1) * LANE)
            s = s_ref[h]
            v_new = u_ref[rows, cols] - _mm(w_ref[rows, cols], s)
            o_ref[rows, cols] = _mm(qg_ref[rows, cols], s) + _mm(a_ref[h, rows, :], v_new)
            gl = jnp.sum(jnp.where(r8 == cc, gl_ref[:, cols], 0.0), axis=0, keepdims=True)
            s_ref[h] = s * gl + _mm(kd_ref[rows, cols], v_new, _dot_tn)

    @pl.when(blk == pl.num_programs(2) - 1)
    def _():
        sf_ref[...] = s_ref[...]


def _gdn_scan(s0, u, w, qg, kd, a, gl, nseq, seqlen):
    t = u.shape[1]
    nblk = seqlen // GDN_BLK

    def rb(d, s, b):
        return s * nblk + jnp.where(d == 0, b, nblk - 1 - b)

    big_spec = pl.BlockSpec((None, GDN_BLK, GDN_W), lambda d, s, b: (d, rb(d, s, b), 0))
    in_specs = [big_spec, big_spec, big_spec, big_spec,
                pl.BlockSpec((None, GDN_HEADS, GDN_BLK, CHUNK), lambda d, s, b: (d, 0, rb(d, s, b), 0)),
                pl.BlockSpec((None, 8, GDN_W), lambda d, s, b: (d, rb(d, s, b), 0))]
    args = [u, w, qg, kd, a, gl]
    state_spec = pl.BlockSpec((None, None, GDN_HEADS, GDN_DK, GDN_DV), lambda d, s, b: (s, d, 0, 0, 0))
    if s0 is not None:
        in_specs.insert(0, state_spec)
        args.insert(0, s0)
    return pl.pallas_call(
        functools.partial(_gdn_scan_kernel, has_init=s0 is not None),
        grid=(2, nseq, nblk),
        in_specs=in_specs,
        out_specs=[big_spec, state_spec],
        out_shape=[jax.ShapeDtypeStruct((2, t, GDN_W), F32),
                   jax.ShapeDtypeStruct((nseq, 2, GDN_HEADS, GDN_DK, GDN_DV), F32)],
        scratch_shapes=[pltpu.VMEM((GDN_HEADS, GDN_DK, GDN_DV), F32)],
        compiler_params=_cparams("parallel", "parallel", "arbitrary"),
        name="gdn_scan",
    )(*args)


def _gdn_post_kernel(o_ref, z_ref, g_ref, out_ref):
    for h in range(GDN_HEADS):
        cols = slice(h * LANE, (h + 1) * LANE)
        o = o_ref[0, :, cols] + o_ref[1, :, cols]
        y = o * lax.rsqrt(jnp.mean(o * o, axis=-1, keepdims=True) + EPS) * g_ref[...]
        out_ref[:, cols] = (y * _silu(z_ref[:, cols])).astype(out_ref.dtype)


def _gdn_post(o, proj, g, row0):
    t = o.shape[1]
    tm = 256
    pb0 = row0 // tm
    return pl.pallas_call(
        _gdn_post_kernel,
        grid=(t // tm,),
        in_specs=[
            pl.BlockSpec((2, tm, GDN_W), lambda i: (0, i, 0)),
            pl.BlockSpec((tm, GDN_W), lambda i: (pb0 + i, CB_Z // GDN_HEADS)),
            pl.BlockSpec((1, GDN_DV), lambda i: (0, 0)),
        ],
        out_specs=pl.BlockSpec((tm, GDN_W), lambda i: (i, 0)),
        out_shape=jax.ShapeDtypeStruct((t, GDN_W), BF16),
        compiler_params=_cparams("parallel"),
        name="gdn_post",
    )(o, proj, g.reshape(1, GDN_DV))


def _rope_kernel(x_ref, cos_ref, sin_ref, o_ref):
    cos = cos_ref[...]
    sin = sin_ref[...]
    lane = lax.broadcasted_iota(jnp.int32, cos.shape, 1)
    first = (lane % 32) < 16
    for hb in range(DIFF_HEADS):
        cols = slice(hb * LANE, (hb + 1) * LANE)
        x = x_ref[:, cols]
        partner = jnp.where(first, pltpu.roll(x, LANE - 16, 1), pltpu.roll(x, 16, 1))
        o_ref[:, cols] = x * cos + partner * sin


def _rope_tables(seqlen):
    pos = jnp.arange(seqlen)
    rowp = (pos // GRID_W).astype(F32)
    colp = (pos % GRID_W).astype(F32)
    half = DIFF_DQK // 2
    inv_freq = ROPE_BASE ** (-jnp.arange(0, half, 2, dtype=F32) / half)
    c = jnp.arange(LANE)
    c64 = c % DIFF_DQK
    p = jnp.where((c64 // half)[None, :] == 0, rowp[:, None], colp[:, None])
    ang = p * inv_freq[c64 % (half // 2)][None, :]
    sign = jnp.where((c64 % half) < half // 2, -1.0, 1.0).astype(F32)
    return jnp.cos(ang), jnp.sin(ang) * sign[None, :]


def _rope(proj, cos, sin, row0, nrows, seqlen):
    tm = 256
    pb0 = row0 // tm
    nsb = seqlen // tm
    return pl.pallas_call(
        _rope_kernel,
        grid=(2, nrows // tm),
        in_specs=[
            pl.BlockSpec((tm, DIFF_W), lambda j, i: (pb0 + i, CB_DQ // DIFF_HEADS + j)),
            pl.BlockSpec((tm, LANE), lambda j, i: (i % nsb, 0)),
            pl.BlockSpec((tm, LANE), lambda j, i: (i % nsb, 0)),
        ],
        out_specs=pl.BlockSpec((None, tm, DIFF_W), lambda j, i: (j, i, 0)),
        out_shape=jax.ShapeDtypeStruct((2, nrows, DIFF_W), F32),
        compiler_params=_cparams("parallel", "parallel"),
        name="rope",
    )(proj, cos, sin)


def _diff_attn_kernel(*refs, seg_lens, lam_init):
    nseg = len(seg_lens)
    q_ref = refs[0]
    k_refs = refs[1:1 + nseg]
    v_refs = refs[1 + nseg:1 + 2 * nseg]
    lam_ref, g_ref, o_ref, kb_ref, vb_ref = refs[1 + 2 * nseg:]
    offs = [sum(seg_lens[:s]) for s in range(nseg)]

    @pl.when(pl.program_id(2) == 0)
    def _():
        vb_ref[:, DIFF_DV:] = jnp.ones((vb_ref.shape[0], DIFF_DV), BF16)
        for s in range(nseg):
            kb_ref[offs[s]:offs[s] + seg_lens[s], :] = k_refs[s][...].astype(BF16)
            vb_ref[offs[s]:offs[s] + seg_lens[s], 0:DIFF_DV] = v_refs[s][...].astype(BF16)

    lv = lam_ref[...]
    lam = (jnp.exp(jnp.sum(lv[0:1] * lv[1:2], axis=1, keepdims=True))
           - jnp.exp(jnp.sum(lv[2:3] * lv[3:4], axis=1, keepdims=True)) + lam_init)
    kb = kb_ref[...]
    vb = vb_ref[...]
    lane = lax.broadcasted_iota(jnp.int32, (ATTN_SUB, DIFF_DV), 1)

    def scores(r):
        q = q_ref[r * ATTN_SUB:(r + 1) * ATTN_SUB, :] * (DIFF_DQK ** -0.5)
        return [_dot_nt(jnp.where((lane // DIFF_DQK) == m, q, 0.0).astype(BF16), kb) for m in range(2)]

    nsub = q_ref.shape[0] // ATTN_SUB
    ss_next = scores(0)
    for r in range(nsub):
        ss = ss_next
        if r + 1 < nsub:
            ss_next = scores(r + 1)
        es = [jnp.exp(s - jnp.max(s, axis=1, keepdims=True)).astype(BF16) for s in ss]
        accs = [_dot(e, vb) for e in es]
        outs = [acc[:, 0:DIFF_DV] / acc[:, DIFF_DV:] for acc in accs]
        o = outs[0] - lam * outs[1]
        y = o * lax.rsqrt(jnp.mean(o * o, axis=-1, keepdims=True) + EPS) * g_ref[...]
        o_ref[r * ATTN_SUB:(r + 1) * ATTN_SUB, :] = (y * (1.0 - lam_init)).astype(o_ref.dtype)


def _diff_attn(q_arr, q_spec, k_arrs, k_specs, v_arrs, v_specs, seg_lens, lam_vec, g, nseq, seqlen, tq,
               lam_init):
    nqb = seqlen // tq
    lk = sum(seg_lens)
    return pl.pallas_call(
        functools.partial(_diff_attn_kernel, seg_lens=tuple(seg_lens), lam_init=lam_init),
        grid=(nseq, DIFF_HEADS, nqb),
        in_specs=[q_spec, *k_specs, *v_specs,
                  pl.BlockSpec((4, DIFF_DQK), lambda b, h, i: (0, 0)),
                  pl.BlockSpec((1, DIFF_DV), lambda b, h, i: (0, 0))],
        out_specs=pl.BlockSpec((tq, DIFF_DV), lambda b, h, i: (b * nqb + i, h)),
        out_shape=jax.ShapeDtypeStruct((nseq * seqlen, DIFF_W), BF16),
        scratch_shapes=[pltpu.VMEM((lk, DIFF_DV), BF16), pltpu.VMEM((lk, 2 * DIFF_DV), BF16)],
        compiler_params=_cparams("parallel", "parallel", "arbitrary"),
        name="diff_attn",
    )(q_arr, *k_arrs, *v_arrs, lam_vec, g.reshape(1, DIFF_DV))


def _pool_kernel(x_ref, w_ref, sc_ref, o_ref):
    n = x_ref.shape[0]
    row = lax.broadcasted_iota(jnp.int32, (n, POOL_GW), 0)
    for gi, win in enumerate(POOL_WINDOWS):
        cols = slice(gi * POOL_GW, (gi + 1) * POOL_GW)
        x = x_ref[:, cols]
        a = win // 2
        b = win - a - 1
        acc = x
        for off in range(-a, b + 1):
            if off != 0:
                acc = acc + _shift_rows(x, off, row)
        cnt = (jnp.minimum(row + b + 1, n) - jnp.maximum(row - a, 0)).astype(F32)
        pooled = acc / cnt - x
        y = _dot(pooled.astype(BF16), w_ref[gi].astype(BF16))
        o_ref[:, cols] = (y * sc_ref[:, cols]).astype(o_ref.dtype)


def _pool(proj, pool_w, pool_scale, row0, nseq, seqlen):
    blk0 = row0 // seqlen
    return pl.pallas_call(
        _pool_kernel,
        grid=(nseq,),
        in_specs=[
            pl.BlockSpec((seqlen, POOL_W), lambda s: (blk0 + s, CB_PIN // len(POOL_WINDOWS))),
            pl.BlockSpec((len(POOL_WINDOWS), POOL_GW, POOL_GW), lambda s: (0, 0, 0)),
            pl.BlockSpec((1, POOL_W), lambda s: (0, 0)),
        ],
        out_specs=pl.BlockSpec((seqlen, POOL_W), lambda s: (s, 0)),
        out_shape=jax.ShapeDtypeStruct((nseq * seqlen, POOL_W), BF16),
        compiler_params=_cparams("parallel"),
        name="pool",
    )(proj, pool_w, pool_scale.reshape(1, POOL_W))


def _mix_group(proj, proj_ab, lp, l, row0, nseq, seqlen, ctx):
    nrows = nseq * seqlen
    qkv = _gdn_prep(proj, lp["gdn_conv"], row0, nseq, seqlen)
    u, w, qg, kd, a, gl = _gdn_intra(qkv, proj_ab, lp["alog_row"], lp["dt_row"], row0)
    o, s_fin = _gdn_scan(None if ctx is None else ctx[2], u, w, qg, kd, a, gl, nseq, seqlen)
    o_gdn = _gdn_post(o, proj, lp["gdn_norm"], row0)

    tq = min(4 * ATTN_SUB, seqlen)
    nqb = seqlen // tq
    qb0 = row0 // tq
    sb0 = row0 // seqlen
    if ctx is None:
        q_arr = proj
        q_spec = pl.BlockSpec((tq, DIFF_DV), lambda b, h, i: (qb0 + b * nqb + i, CB_DQ + h))
        k_arrs = [proj]
        k_specs = [pl.BlockSpec((seqlen, DIFF_DV), lambda b, h, i: (sb0 + b, CB_DK + h))]
        v_arrs = [proj]
        v_specs = [pl.BlockSpec((seqlen, DIFF_DV), lambda b, h, i: (sb0 + b, CB_DV + h))]
        seg_lens = [seqlen]
    else:
        cache_k, cache_v = ctx[0], ctx[1]
        past = cache_k.shape[2]
        cos, sin = _rope_tables(seqlen)
        qk_rot = _rope(proj, cos, sin, row0, nrows, seqlen)
        q_arr = qk_rot
        q_spec = pl.BlockSpec((None, tq, DIFF_DV), lambda b, h, i: (0, b * nqb + i, h))
        k_arrs = [cache_k, qk_rot]
        k_specs = [pl.BlockSpec((None, None, past, DIFF_DV), lambda b, h, i: (b, l, 0, h)),
                   pl.BlockSpec((None, seqlen, DIFF_DV), lambda b, h, i: (1, b, h))]
        v_arrs = [cache_v, proj]
        v_specs = [pl.BlockSpec((None, None, past, DIFF_DV), lambda b, h, i: (b, l, 0, h)),
                   pl.BlockSpec((seqlen, DIFF_DV), lambda b, h, i: (sb0 + b, CB_DV + h))]
        seg_lens = [past, seqlen]
    o_diff = _diff_attn(q_arr, q_spec, k_arrs, k_specs, v_arrs, v_specs, seg_lens, lp["diff_lam"],
                        lp["diff_norm"], nseq, seqlen, tq, _lambda_init(l))

    o_pool = _pool(proj, lp["pool_w"], lp["pool_scale"], row0, nseq, seqlen)
    return (o_gdn, o_diff, o_pool), s_fin


def _split_w_in(w_in):
    depth, d, _ = w_in.shape
    o_ab = 4 * GDN_W
    o_dq = o_ab + 4 * GDN_HEADS
    pad = jnp.zeros((depth, d, LANE - 4 * GDN_HEADS), w_in.dtype)
    main = jnp.concatenate([w_in[..., :o_ab], w_in[..., o_dq:]], axis=-1)
    ab = jnp.concatenate([w_in[..., o_ab:o_dq], pad], axis=-1)
    return main.astype(BF16), ab.astype(BF16)


def kernel(x_prompt, x_sample, c, cache_k, cache_v, state_gdn, c_ctx, w_ada, b_ada, norm_ffn1, ffn1_in,
           ffn1_out, norm_mix, w_in, gdn_conv, gdn_a_log, gdn_dt_bias, gdn_norm, diff_lam, diff_norm,
           pool_w, pool_scale, w_out, norm_ffn2, ffn2_in, ffn2_out, final_norm):
    batch, seq, d = x_prompt.shape
    dec_batch, dec_seq, _ = x_sample.shape
    depth = w_ada.shape[0]
    n_ctx = batch * seq
    n_dec = dec_batch * dec_seq
    layout = (n_ctx, dec_seq)
    past = cache_k.shape[2]

    x = jnp.concatenate([x_prompt.reshape(n_ctx, d), x_sample.reshape(n_dec, d)], axis=0)
    cond8 = jnp.concatenate([c_ctx[None, :], c, jnp.zeros((8 - 1 - dec_batch, d), F32)], axis=0)
    ada = _ada(cond8, w_ada, b_ada)
    cache_k4 = cache_k.reshape(dec_batch, depth, past, DIFF_W)
    cache_v4 = cache_v.reshape(dec_batch, depth, past, DIFF_W)
    w_in_main, w_in_ab = _split_w_in(w_in)
    w_out_b = w_out.astype(BF16)

    ks, vs, ss = [], [], []
    for l in range(depth):
        ada3 = ada[l, :1 + dec_batch].reshape(1 + dec_batch, 1, N_MOD * d)
        lane_pad = jnp.zeros((LANE - 2 * GDN_HEADS,), F32)
        lp = dict(
            gdn_conv=gdn_conv[l], gdn_norm=gdn_norm[l], diff_lam=diff_lam[l], diff_norm=diff_norm[l],
            pool_w=pool_w[l], pool_scale=pool_scale[l],
            alog_row=jnp.concatenate([gdn_a_log[l].reshape(-1), lane_pad]).reshape(1, LANE),
            dt_row=jnp.concatenate([gdn_dt_bias[l].reshape(-1), lane_pad]).reshape(1, LANE),
        )
        act = _ffn_in(x, ada3, (0, 1), norm_ffn1[l], ffn1_in, l, layout=layout)
        x = _ffn_out(act, ffn1_out, l, x, ada3, 2, layout=layout)

        proj, proj_ab = _mix_in(x, ada3, (3, 4), norm_mix[l], w_in_main, w_in_ab, l, layout=layout)
        acts_c, s_c = _mix_group(proj, proj_ab, lp, l, 0, batch, seq, None)
        acts_d, _ = _mix_group(proj, proj_ab, lp, l, n_ctx, dec_batch, dec_seq,
                               (cache_k4, cache_v4, state_gdn[:, l]))
        x = _mix_out(acts_c, w_out_b, l, x, ada3, 5, 0, layout=layout)
        x = _mix_out(acts_d, w_out_b, l, x, ada3, 5, n_ctx, layout=layout)
        ks.append(proj[:n_ctx, CB_DK * LANE:CB_DK * LANE + DIFF_W].reshape(batch, seq, DIFF_HEADS, DIFF_DV))
        vs.append(proj[:n_ctx, CB_DV * LANE:CB_DV * LANE + DIFF_W].reshape(batch, seq, DIFF_HEADS, DIFF_DV))
        ss.append(s_c)

        act = _ffn_in(x, ada3, (6, 7), norm_ffn2[l], ffn2_in, l, layout=layout)
        x = _ffn_out(act, ffn2_out, l, x, ada3, 8, layout=layout)

    y_prompt = _final_norm(x, final_norm, 0, n_ctx).reshape(batch, seq, d)
    y_sample = _final_norm(x, final_norm, n_ctx, n_dec).reshape(dec_batch, dec_seq, d)
    return (y_prompt, y_sample, jnp.stack(ks, axis=1), jnp.stack(vs, axis=1), jnp.stack(ss, axis=1))
```

```python
import functools
import math

import jax
import jax.numpy as jnp
from jax import lax
from jax.experimental import pallas as pl
from jax.experimental.pallas import tpu as pltpu

F32 = jnp.float32
BF16 = jnp.bfloat16

GRID_W = 64
GDN_HEADS = 4
GDN_DK = 128
GDN_DV = 128
GDN_W = GDN_HEADS * GDN_DV
CONV_K = 4
CHUNK = 64
DIFF_HEADS = 8
DIFF_DQK = 64
DIFF_DV = 2 * DIFF_DQK
DIFF_W = DIFF_HEADS * DIFF_DV
ROPE_BASE = 10000.0
POOL_WINDOWS = (2, 4, 8, 16)
POOL_GW = 128
POOL_W = len(POOL_WINDOWS) * POOL_GW
MIX_W = GDN_W + DIFF_W + POOL_W
N_MOD = 9
EPS = 1e-6

LANE = 128
GDN_BLK = 4 * CHUNK
ATTN_SUB = 256
PRO_SUB = 256
VMEM_LIMIT = 56 * 1024 * 1024

CB_Q, CB_K, CB_V, CB_Z = 0, 4, 8, 12
CB_DQ, CB_DK, CB_DV, CB_PIN = 16, 24, 32, 40
PROJ_MAIN = 44 * LANE


def _lambda_init(l):
    return 0.8 - 0.6 * math.exp(-0.3 * l)


def _cparams(*sem):
    return pltpu.CompilerParams(dimension_semantics=sem, vmem_limit_bytes=VMEM_LIMIT)


def _dot(a, b):
    return lax.dot_general(a, b, (((1,), (0,)), ((), ())), preferred_element_type=F32)


def _dot_nt(a, b):
    return lax.dot_general(a, b, (((1,), (1,)), ((), ())), preferred_element_type=F32)


def _dot_tn(a, b):
    return lax.dot_general(a, b, (((0,), (0,)), ((), ())), preferred_element_type=F32)


def _mm(a, b, dot=_dot):
    return dot(a.astype(BF16), b.astype(BF16))


def _dot_exact_lhs(m, b):
    b0 = b.astype(BF16)
    r = b - b0.astype(F32)
    b1 = r.astype(BF16)
    b2 = (r - b1.astype(F32)).astype(BF16)
    return _dot(m, b0) + (_dot(m, b1) + _dot(m, b2))


def _sigmoid(x):
    return 0.5 * (1.0 + jnp.tanh(0.5 * x))


def _silu(x):
    return x * _sigmoid(x)


def _ada_kernel(cond_ref, w_ref, b_ref, o_ref):
    s = _silu(cond_ref[...]).astype(BF16)
    o_ref[...] = _dot(s, w_ref[...].astype(BF16)) + b_ref[...]


def _ada(cond8, w_ada, b_ada):
    depth, d, n = w_ada.shape
    tn = 1024
    return pl.pallas_call(
        _ada_kernel,
        grid=(depth, n // tn),
        in_specs=[
            pl.BlockSpec((8, d), lambda l, j: (0, 0)),
            pl.BlockSpec((None, d, tn), lambda l, j: (l, 0, j)),
            pl.BlockSpec((None, 1, tn), lambda l, j: (l, 0, j)),
        ],
        out_specs=pl.BlockSpec((None, 8, tn), lambda l, j: (l, 0, j)),
        out_shape=jax.ShapeDtypeStruct((depth, 8, n), F32),
        compiler_params=_cparams("parallel", "parallel"),
        name="ada",
    )(cond8, w_ada, b_ada.reshape(depth, 1, n))


def _norm_mod(x, g, shift, scale):
    y = x * lax.rsqrt(jnp.mean(x * x, axis=-1, keepdims=True) + EPS) * g
    return y * (1.0 + scale) + shift


def _norm_mod_rows(x_ref, sh_ref, sc_ref, g_ref, rows):
    return _norm_mod(x_ref[rows, :], g_ref[...], sh_ref[0], sc_ref[0]).astype(BF16)


def _row_subs(n):
    return [slice(r * PRO_SUB, (r + 1) * PRO_SUB) for r in range(n // PRO_SUB)]


def _mix_in_kernel(x_ref, sh_ref, sc_ref, g_ref, w_ref, wab_ref, o_ref, ab_ref, h_ref):
    @pl.when(pl.program_id(1) == 0)
    def _():
        for rows in _row_subs(x_ref.shape[0]):
            h = _norm_mod_rows(x_ref, sh_ref, sc_ref, g_ref, rows)
            h_ref[rows, :] = h
            ab_ref[rows, :] = _dot(h, wab_ref[...])
            o_ref[rows, :] = _dot(h, w_ref[...])

    @pl.when(pl.program_id(1) != 0)
    def _():
        o_ref[...] = _dot(h_ref[...], w_ref[...])


def _ffn_in_kernel(x_ref, sh_ref, sc_ref, g_ref, wg_ref, wu_ref, o_ref, h_ref):
    def swiglu(h):
        gate = _dot(h, wg_ref[...])
        up = _dot(h, wu_ref[...])
        return (_silu(gate) * up).astype(o_ref.dtype)

    @pl.when(pl.program_id(1) == 0)
    def _():
        for rows in _row_subs(x_ref.shape[0]):
            h = _norm_mod_rows(x_ref, sh_ref, sc_ref, g_ref, rows)
            h_ref[rows, :] = h
            o_ref[rows, :] = swiglu(h)

    @pl.when(pl.program_id(1) != 0)
    def _():
        o_ref[...] = swiglu(h_ref[...])


def _row_tile(groups, cap):
    tm = cap
    while any(g % tm for g in groups):
        tm //= 2
    return tm


def _mod_row(n_ctx, dec_seq, tm, row0=0):
    def f(i):
        r = row0 + i * tm
        return jnp.where(r < n_ctx, 0, 1 + (r - n_ctx) // dec_seq)
    return f


def _pm_specs(d, tm, row, mod_cols):
    sh_c, sc_c = mod_cols
    return [
        pl.BlockSpec((tm, d), lambda i, j: (i, 0)),
        pl.BlockSpec((1, 1, d), lambda i, j: (row(i), 0, sh_c)),
        pl.BlockSpec((1, 1, d), lambda i, j: (row(i), 0, sc_c)),
        pl.BlockSpec((1, d), lambda i, j: (0, 0)),
    ]


def _ffn_in(x, ada3, mod_cols, g, w, l, *, layout):
    t, d = x.shape
    n_ctx, dec_seq = layout
    tm = _row_tile((n_ctx, dec_seq), 1024)
    row = _mod_row(n_ctx, dec_seq, tm)
    n = w.shape[2] // 2
    tn = 512 if n % 512 == 0 else n
    nj = n // tn
    return pl.pallas_call(
        _ffn_in_kernel,
        grid=(t // tm, nj),
        in_specs=_pm_specs(d, tm, row, mod_cols) + [
            pl.BlockSpec((None, d, tn), lambda i, j: (l, 0, j)),
            pl.BlockSpec((None, d, tn), lambda i, j: (l, 0, j + nj)),
        ],
        out_specs=pl.BlockSpec((tm, tn), lambda i, j: (i, j)),
        out_shape=jax.ShapeDtypeStruct((t, n), BF16),
        scratch_shapes=[pltpu.VMEM((tm, d), BF16)],
        compiler_params=_cparams("parallel", "arbitrary"),
        name="ffn_in",
    )(x, ada3, ada3, g.reshape(1, d), w, w)


def _mix_in(x, ada3, mod_cols, g, w_main, w_ab, l, *, layout):
    t, d = x.shape
    n_ctx, dec_seq = layout
    tm = _row_tile((n_ctx, dec_seq), 1024)
    row = _mod_row(n_ctx, dec_seq, tm)
    tn = 512
    return pl.pallas_call(
        _mix_in_kernel,
        grid=(t // tm, PROJ_MAIN // tn),
        in_specs=_pm_specs(d, tm, row, mod_cols) + [
            pl.BlockSpec((None, d, tn), lambda i, j: (l, 0, j)),
            pl.BlockSpec((None, d, LANE), lambda i, j: (l, 0, 0)),
        ],
        out_specs=[pl.BlockSpec((tm, tn), lambda i, j: (i, j)),
                   pl.BlockSpec((tm, LANE), lambda i, j: (i, 0))],
        out_shape=[jax.ShapeDtypeStruct((t, PROJ_MAIN), F32), jax.ShapeDtypeStruct((t, LANE), F32)],
        scratch_shapes=[pltpu.VMEM((tm, d), BF16)],
        compiler_params=_cparams("parallel", "arbitrary"),
        name="mix_in",
    )(x, ada3, ada3, g.reshape(1, d), w_main, w_ab)


def _ffn_out_kernel(a_ref, w_ref, x_ref, gate_ref, o_ref):
    y = _dot(a_ref[...], w_ref[...])
    o_ref[...] = x_ref[...] + (0.5 * gate_ref[0]) * y


def _ffn_out(a, w, l, x, ada3, gate_col, *, layout):
    t, k = a.shape
    d = w.shape[2]
    n_ctx, dec_seq = layout
    tm = _row_tile((n_ctx, dec_seq), 1024)
    tn = min(512, d)
    row = _mod_row(n_ctx, dec_seq, tm)
    gpb = d // tn
    return pl.pallas_call(
        _ffn_out_kernel,
        grid=(t // tm, d // tn),
        in_specs=[
            pl.BlockSpec((tm, k), lambda i, j: (i, 0)),
            pl.BlockSpec((None, k, tn), lambda i, j: (l, 0, j)),
            pl.BlockSpec((tm, tn), lambda i, j: (i, j)),
            pl.BlockSpec((1, 1, tn), lambda i, j: (row(i), 0, gate_col * gpb + j)),
        ],
        out_specs=pl.BlockSpec((tm, tn), lambda i, j: (i, j)),
        out_shape=jax.ShapeDtypeStruct((t, d), F32),
        compiler_params=_cparams("parallel", "parallel"),
        name="ffn_out",
    )(a, w, x, ada3)


def _mix_out_kernel(a1_ref, a2_ref, a3_ref, w_ref, x_ref, gate_ref, o_ref):
    y = (_dot(a1_ref[...], w_ref[0:GDN_W, :])
         + _dot(a2_ref[...], w_ref[GDN_W:GDN_W + DIFF_W, :])
         + _dot(a3_ref[...], w_ref[GDN_W + DIFF_W:MIX_W, :]))
    o_ref[...] = x_ref[...] + gate_ref[0] * y


def _mix_out(acts, w, l, x, ada3, gate_col, row0, *, layout):
    a1, a2, a3 = acts
    n = a1.shape[0]
    t, d = x.shape
    n_ctx, dec_seq = layout
    tm = _row_tile((n_ctx, dec_seq), 512)
    tn = min(512, d)
    row = _mod_row(n_ctx, dec_seq, tm, row0)
    rb0 = row0 // tm
    gpb = d // tn
    return pl.pallas_call(
        _mix_out_kernel,
        grid=(n // tm, d // tn),
        in_specs=[
            pl.BlockSpec((tm, GDN_W), lambda i, j: (i, 0)),
            pl.BlockSpec((tm, DIFF_W), lambda i, j: (i, 0)),
            pl.BlockSpec((tm, POOL_W), lambda i, j: (i, 0)),
            pl.BlockSpec((None, MIX_W, tn), lambda i, j: (l, 0, j)),
            pl.BlockSpec((tm, tn), lambda i, j: (rb0 + i, j)),
            pl.BlockSpec((1, 1, tn), lambda i, j: (row(i), 0, gate_col * gpb + j)),
        ],
        out_specs=pl.BlockSpec((tm, tn), lambda i, j: (rb0 + i, j)),
        out_shape=jax.ShapeDtypeStruct((t, d), F32),
        input_output_aliases={4: 0},
        compiler_params=_cparams("parallel", "parallel"),
        name="mix_out",
    )(a1, a2, a3, w, x, ada3)


def _final_norm_kernel(x_ref, g_ref, o_ref):
    x = x_ref[...]
    o_ref[...] = x * lax.rsqrt(jnp.mean(x * x, axis=-1, keepdims=True) + EPS) * g_ref[...]


def _final_norm(x, g, row0, nrows):
    d = x.shape[1]
    tm = 256
    rb0 = row0 // tm
    return pl.pallas_call(
        _final_norm_kernel,
        grid=(nrows // tm,),
        in_specs=[pl.BlockSpec((tm, d), lambda i: (rb0 + i, 0)), pl.BlockSpec((1, d), lambda i: (0, 0))],
        out_specs=pl.BlockSpec((tm, d), lambda i: (i, 0)),
        out_shape=jax.ShapeDtypeStruct((nrows, d), F32),
        compiler_params=_cparams("parallel"),
        name="final_norm",
    )(x, g.reshape(1, d))


def _shift_rows(x, off, row):
    if off == 0:
        return x
    n = x.shape[0]
    y = pltpu.roll(x, (-off) % n, 0)
    ok = (row + off >= 0) & (row + off < n)
    return jnp.where(ok, y, 0.0)


def _gdn_prep_kernel(x_ref, w_ref, o_ref):
    x = x_ref[...]
    n = x.shape[0]
    left = CONV_K // 2
    edge = 16

    def conv(xb, shift):
        acc = jnp.zeros_like(xb)
        for j in range(CONV_K):
            acc = acc + shift(xb, j - left) * w_ref[j:j + 1, :]
        return acc

    acc = conv(x, lambda xb, off: xb if off == 0 else pltpu.roll(xb, (-off) % n, 0))
    row = lax.broadcasted_iota(jnp.int32, (edge, LANE), 0)
    head = conv(x[0:edge], lambda xb, off: _shift_rows(xb, off, row))
    tail = conv(x[n - edge:n], lambda xb, off: _shift_rows(xb, off, row))
    acc = jnp.concatenate([head[0:8], acc[8:n - 8], tail[edge - 8:edge]], axis=0)
    y = _silu(acc)
    kind = pl.program_id(1) // GDN_HEADS
    inv = lax.rsqrt(jnp.sum(y * y, axis=-1, keepdims=True) + EPS)
    inv = inv * jnp.where(kind == 0, GDN_DK ** -0.5, 1.0)
    o_ref[...] = jnp.where(kind < 2, y * inv, y)


def _gdn_prep(proj, conv_w, row0, nseq, seqlen):
    blk0 = row0 // seqlen
    return pl.pallas_call(
        _gdn_prep_kernel,
        grid=(nseq, 3 * GDN_HEADS),
        in_specs=[
            pl.BlockSpec((seqlen, LANE), lambda s, c: (blk0 + s, CB_Q + c)),
            pl.BlockSpec((CONV_K, LANE), lambda s, c: (0, c)),
        ],
        out_specs=pl.BlockSpec((seqlen, LANE), lambda s, c: (s, c)),
        out_shape=jax.ShapeDtypeStruct((nseq * seqlen, 3 * GDN_W), F32),
        compiler_params=_cparams("parallel", "parallel"),
        name="gdn_prep",
    )(proj, conv_w)


def _unit_tri_solve_many(lmats, rhss, eye):
    xs = [eye - l for l in lmats]
    ps = [_mm(l, l) for l in lmats]
    steps = int(math.log2(CHUNK)) - 1
    for s in range(steps):
        xs = [x + _mm(x, p) for x, p in zip(xs, ps)]
        if s + 1 < steps:
            ps = [_mm(p, p) for p in ps]
    return [_mm(x, r) for x, r in zip(xs, rhss)]


def _gdn_intra_kernel(q_ref, k_ref, v_ref, ab_ref, alog_ref, dt_ref,
                      u_ref, w_ref, qg_ref, kd_ref, a_ref, gl_ref):
    n = GDN_BLK
    ab = ab_ref[...]
    lane = lax.broadcasted_iota(jnp.int32, ab.shape, 1)
    sp = jnp.maximum(ab + dt_ref[...], 0.0) + jnp.log(1.0 + jnp.exp(-jnp.abs(ab + dt_ref[...])))
    gmat = -jnp.exp(alog_ref[...]) * sp
    bmat = _sigmoid(ab)

    ri = lax.broadcasted_iota(jnp.int32, (n, n), 0)
    ci = lax.broadcasted_iota(jnp.int32, (n, n), 1)
    same = (ri // CHUNK) == (ci // CHUNK)
    eye = (ri == ci).astype(F32)
    r8 = lax.broadcasted_iota(jnp.int32, (8, LANE), 0)
    gtot_all = _dot_exact_lhs(same.astype(BF16), gmat)
    gcf_all = _dot_exact_lhs((same & (ci <= ri)).astype(BF16), gmat)
    gc_dirs = (gcf_all, (gtot_all - gcf_all) + gmat)
    incl_dirs = (same & (ci <= ri), same & (ci >= ri))
    strict_dirs = (same & (ci < ri), same & (ci > ri))

    lmats, rhss, dests = [], [], []
    for h in range(GDN_HEADS):
        cols = slice(h * LANE, (h + 1) * LANE)
        q = q_ref[:, cols]
        k = k_ref[:, cols]
        v = v_ref[:, cols]
        kk = _mm(k, k, _dot_nt)
        qk = _mm(q, k, _dot_nt)
        for d in range(2):
            incl = incl_dirs[d]
            sel_g = lane == d * GDN_HEADS + h
            sel_b = lane == 2 * GDN_HEADS + d * GDN_HEADS + h
            gc = jnp.sum(jnp.where(sel_g, gc_dirs[d], 0.0), axis=1, keepdims=True)
            gtot = jnp.sum(jnp.where(sel_g, gtot_all, 0.0), axis=1, keepdims=True)
            beta = jnp.sum(jnp.where(sel_b, bmat, 0.0), axis=1, keepdims=True)
            gc_b = jnp.broadcast_to(gc, (n, LANE))
            gc_row = jnp.transpose(gc_b)[0:1, :]
            diff = jnp.where(incl, gc - gc_row, 0.0)
            decay = jnp.where(incl, jnp.exp(diff), 0.0)
            eg = jnp.exp(gc_b)
            lmats.append(jnp.where(strict_dirs[d], kk * beta * decay, 0.0))
            rhss.append(jnp.concatenate([v * beta, (k * beta) * eg], axis=1))
            dests.append((d, cols))
            qg_ref[d, :, cols] = q * eg
            kd_ref[d, :, cols] = k * jnp.exp(jnp.broadcast_to(gtot - gc, (n, LANE)))
            amat = qk * decay
            a_ref[d, h] = (amat[:, 0:CHUNK] + amat[:, CHUNK:2 * CHUNK]) + (
                amat[:, 2 * CHUNK:3 * CHUNK] + amat[:, 3 * CHUNK:4 * CHUNK])
            egt = jnp.exp(jnp.broadcast_to(gtot, (n, LANE)))
            gl8 = jnp.zeros((8, LANE), F32)
            for c in range(n // CHUNK):
                gl8 = jnp.where(r8 == c, egt[c * CHUNK:c * CHUNK + 8, :], gl8)
            gl_ref[d, :, cols] = gl8

    for (d, cols), uw in zip(dests, _unit_tri_solve_many(lmats, rhss, eye)):
        u_ref[d, :, cols] = uw[:, :GDN_DV]
        w_ref[d, :, cols] = uw[:, GDN_DV:]


def _gdn_intra(qkv, proj_ab, alog_row, dt_row, row0):
    t = qkv.shape[0]
    nb = t // GDN_BLK
    pb0 = row0 // GDN_BLK
    big = jax.ShapeDtypeStruct((2, t, GDN_W), F32)
    big_spec = pl.BlockSpec((2, GDN_BLK, GDN_W), lambda i: (0, i, 0))
    return pl.pallas_call(
        _gdn_intra_kernel,
        grid=(nb,),
        in_specs=[
            pl.BlockSpec((GDN_BLK, GDN_W), lambda i: (i, 0)),
            pl.BlockSpec((GDN_BLK, GDN_W), lambda i: (i, 1)),
            pl.BlockSpec((GDN_BLK, GDN_W), lambda i: (i, 2)),
            pl.BlockSpec((GDN_BLK, LANE), lambda i: (pb0 + i, 0)),
            pl.BlockSpec((1, LANE), lambda i: (0, 0)),
            pl.BlockSpec((1, LANE), lambda i: (0, 0)),
        ],
        out_specs=[big_spec, big_spec, big_spec, big_spec,
                   pl.BlockSpec((2, GDN_HEADS, GDN_BLK, CHUNK), lambda i: (0, 0, i, 0)),
                   pl.BlockSpec((2, 8, GDN_W), lambda i: (0, i, 0))],
        out_shape=[big, big, big, big,
                   jax.ShapeDtypeStruct((2, GDN_HEADS, t, CHUNK), F32),
                   jax.ShapeDtypeStruct((2, nb * 8, GDN_W), F32)],
        compiler_params=_cparams("parallel"),
        name="gdn_intra",
    )(qkv, qkv, qkv, proj_ab, alog_row, dt_row)


def _gdn_scan_kernel(*refs, has_init):
    if has_init:
        s0_ref, u_ref, w_ref, qg_ref, kd_ref, a_ref, gl_ref, o_ref, sf_ref, s_ref = refs
    else:
        u_ref, w_ref, qg_ref, kd_ref, a_ref, gl_ref, o_ref, sf_ref, s_ref = refs
    d = pl.program_id(0)
    blk = pl.program_id(2)

    @pl.when(blk == 0)
    def _():
        if has_init:
            s_ref[...] = s0_ref[...]
        else:
            s_ref[...] = jnp.zeros_like(s_ref)

    r8 = lax.broadcasted_iota(jnp.int32, (8, LANE), 0)
    for c in range(GDN_BLK // CHUNK):
        cc = jnp.where(d == 0, c, GDN_BLK // CHUNK - 1 - c)
        rows = pl.ds(pl.multiple_of(cc * CHUNK, CHUNK), CHUNK)
        for h in range(GDN_HEADS):
            cols = slice(h * LANE, (h + 1) * LANE)
            s = s_ref[h]
            v_new = u_ref[rows, cols] - _mm(w_ref[rows, cols], s)
            o_ref[rows, cols] = _mm(qg_ref[rows, cols], s) + _mm(a_ref[h, rows, :], v_new)
            gl = jnp.sum(jnp.where(r8 == cc, gl_ref[:, cols], 0.0), axis=0, keepdims=True)
            s_ref[h] = s * gl + _mm(kd_ref[rows, cols], v_new, _dot_tn)

    @pl.when(blk == pl.num_programs(2) - 1)
    def _():
        sf_ref[...] = s_ref[...]


def _gdn_scan(s0, u, w, qg, kd, a, gl, nseq, seqlen):
    t = u.shape[1]
    nblk = seqlen // GDN_BLK

    def rb(d, s, b):
        return s * nblk + jnp.where(d == 0, b, nblk - 1 - b)

    big_spec = pl.BlockSpec((None, GDN_BLK, GDN_W), lambda d, s, b: (d, rb(d, s, b), 0))
    in_specs = [big_spec, big_spec, big_spec, big_spec,
                pl.BlockSpec((None, GDN_HEADS, GDN_BLK, CHUNK), lambda d, s, b: (d, 0, rb(d, s, b), 0)),
                pl.BlockSpec((None, 8, GDN_W), lambda d, s, b: (d, rb(d, s, b), 0))]
    args = [u, w, qg, kd, a, gl]
    state_spec = pl.BlockSpec((None, None, GDN_HEADS, GDN_DK, GDN_DV), lambda d, s, b: (s, d, 0, 0, 0))
    if s0 is not None:
        in_specs.insert(0, state_spec)
        args.insert(0, s0)
    return pl.pallas_call(
        functools.partial(_gdn_scan_kernel, has_init=s0 is not None),
        grid=(2, nseq, nblk),
        in_specs=in_specs,
        out_specs=[big_spec, state_spec],
        out_shape=[jax.ShapeDtypeStruct((2, t, GDN_W), F32),
                   jax.ShapeDtypeStruct((nseq, 2, GDN_HEADS, GDN_DK, GDN_DV), F32)],
        scratch_shapes=[pltpu.VMEM((GDN_HEADS, GDN_DK, GDN_DV), F32)],
        compiler_params=_cparams("parallel", "parallel", "arbitrary"),
        name="gdn_scan",
    )(*args)


def _gdn_post_kernel(o_ref, z_ref, g_ref, out_ref):
    for h in range(GDN_HEADS):
        cols = slice(h * LANE, (h + 1) * LANE)
        o = o_ref[0, :, cols] + o_ref[1, :, cols]
        y = o * lax.rsqrt(jnp.mean(o * o, axis=-1, keepdims=True) + EPS) * g_ref[...]
        out_ref[:, cols] = (y * _silu(z_ref[:, cols])).astype(out_ref.dtype)


def _gdn_post(o, proj, g, row0):
    t = o.shape[1]
    tm = 256
    pb0 = row0 // tm
    return pl.pallas_call(
        _gdn_post_kernel,
        grid=(t // tm,),
        in_specs=[
            pl.BlockSpec((2, tm, GDN_W), lambda i: (0, i, 0)),
            pl.BlockSpec((tm, GDN_W), lambda i: (pb0 + i, CB_Z // GDN_HEADS)),
            pl.BlockSpec((1, GDN_DV), lambda i: (0, 0)),
        ],
        out_specs=pl.BlockSpec((tm, GDN_W), lambda i: (i, 0)),
        out_shape=jax.ShapeDtypeStruct((t, GDN_W), BF16),
        compiler_params=_cparams("parallel"),
        name="gdn_post",
    )(o, proj, g.reshape(1, GDN_DV))


def _rope_rotate(x, cos, sin_signed):
    lane = lax.broadcasted_iota(jnp.int32, x.shape, 1)
    first = (lane % 32) < 16
    partner = jnp.where(first, pltpu.roll(x, LANE - 16, 1), pltpu.roll(x, 16, 1))
    return x * cos + partner * sin_signed


def _rope_tables(seqlen):
    pos = jnp.arange(seqlen)
    rowp = (pos // GRID_W).astype(F32)
    colp = (pos % GRID_W).astype(F32)
    half = DIFF_DQK // 2
    inv_freq = ROPE_BASE ** (-jnp.arange(0, half, 2, dtype=F32) / half)
    c = jnp.arange(LANE)
    c64 = c % DIFF_DQK
    p = jnp.where((c64 // half)[None, :] == 0, rowp[:, None], colp[:, None])
    ang = p * inv_freq[c64 % (half // 2)][None, :]
    sign = jnp.where((c64 % half) < half // 2, -1.0, 1.0).astype(F32)
    return jnp.cos(ang), jnp.sin(ang) * sign[None, :]


def _diff_attn_kernel(*refs, seg_lens, lam_init, rope):
    nseg = len(seg_lens)
    q_ref = refs[0]
    k_refs = refs[1:1 + nseg]
    v_refs = refs[1 + nseg:1 + 2 * nseg]
    if rope:
        cosq_ref, sinq_ref, cosk_ref, sink_ref = refs[1 + 2 * nseg:5 + 2 * nseg]
    lam_ref, g_ref, o_ref, kb_ref, vb_ref = refs[-5:]
    offs = [sum(seg_lens[:s]) for s in range(nseg)]

    @pl.when(pl.program_id(2) == 0)
    def _():
        vb_ref[:, DIFF_DV:] = jnp.ones((vb_ref.shape[0], DIFF_DV), BF16)
        for s in range(nseg):
            k = k_refs[s][...]
            if rope and s == nseg - 1:
                k = _rope_rotate(k, cosk_ref[...], sink_ref[...])
            kb_ref[offs[s]:offs[s] + seg_lens[s], :] = k.astype(BF16)
            vb_ref[offs[s]:offs[s] + seg_lens[s], 0:DIFF_DV] = v_refs[s][...].astype(BF16)

    lv = lam_ref[...]
    lam = (jnp.exp(jnp.sum(lv[0:1] * lv[1:2], axis=1, keepdims=True))
           - jnp.exp(jnp.sum(lv[2:3] * lv[3:4], axis=1, keepdims=True)) + lam_init)
    kb = kb_ref[...]
    vb = vb_ref[...]
    lane = lax.broadcasted_iota(jnp.int32, (ATTN_SUB, DIFF_DV), 1)

    def scores(r):
        rows = slice(r * ATTN_SUB, (r + 1) * ATTN_SUB)
        q = q_ref[rows, :]
        if rope:
            q = _rope_rotate(q, cosq_ref[rows, :], sinq_ref[rows, :])
        q = q * (DIFF_DQK ** -0.5)
        return [_dot_nt(jnp.where((lane // DIFF_DQK) == m, q, 0.0).astype(BF16), kb) for m in range(2)]

    nsub = q_ref.shape[0] // ATTN_SUB
    ss_next = scores(0)
    for r in range(nsub):
        ss = ss_next
        if r + 1 < nsub:
            ss_next = scores(r + 1)
        es = [jnp.exp(s - jnp.max(s, axis=1, keepdims=True)).astype(BF16) for s in ss]
        accs = [_dot(e, vb) for e in es]
        outs = [acc[:, 0:DIFF_DV] / acc[:, DIFF_DV:] for acc in accs]
        o = outs[0] - lam * outs[1]
        y = o * lax.rsqrt(jnp.mean(o * o, axis=-1, keepdims=True) + EPS) * g_ref[...]
        o_ref[r * ATTN_SUB:(r + 1) * ATTN_SUB, :] = (y * (1.0 - lam_init)).astype(o_ref.dtype)


def _diff_attn(q_arr, q_spec, k_arrs, k_specs, v_arrs, v_specs, seg_lens, rope_tabs, lam_vec, g, nseq, seqlen,
               tq, lam_init):
    nqb = seqlen // tq
    lk = sum(seg_lens)
    rope_arrs, rope_specs = [], []
    if rope_tabs is not None:
        cos, sin = rope_tabs
        rope_arrs = [cos, sin, cos, sin]
        rope_specs = [pl.BlockSpec((tq, DIFF_DV), lambda b, h, i: (i, 0))] * 2 + [
            pl.BlockSpec((seqlen, DIFF_DV), lambda b, h, i: (0, 0))] * 2
    return pl.pallas_call(
        functools.partial(_diff_attn_kernel, seg_lens=tuple(seg_lens), lam_init=lam_init,
                          rope=rope_tabs is not None),
        grid=(nseq, DIFF_HEADS, nqb),
        in_specs=[q_spec, *k_specs, *v_specs, *rope_specs,
                  pl.BlockSpec((4, DIFF_DQK), lambda b, h, i: (0, 0)),
                  pl.BlockSpec((1, DIFF_DV), lambda b, h, i: (0, 0))],
        out_specs=pl.BlockSpec((tq, DIFF_DV), lambda b, h, i: (b * nqb + i, h)),
        out_shape=jax.ShapeDtypeStruct((nseq * seqlen, DIFF_W), BF16),
        scratch_shapes=[pltpu.VMEM((lk, DIFF_DV), BF16), pltpu.VMEM((lk, 2 * DIFF_DV), BF16)],
        compiler_params=_cparams("parallel", "parallel", "arbitrary"),
        name="diff_attn",
    )(q_arr, *k_arrs, *v_arrs, *rope_arrs, lam_vec, g.reshape(1, DIFF_DV))


def _pool_kernel(x_ref, w_ref, sc_ref, o_ref):
    n = x_ref.shape[0]
    row = lax.broadcasted_iota(jnp.int32, (n, POOL_GW), 0)
    for gi, win in enumerate(POOL_WINDOWS):
        cols = slice(gi * POOL_GW, (gi + 1) * POOL_GW)
        x = x_ref[:, cols]
        a = win // 2
        b = win - a - 1
        acc = x
        for off in range(-a, b + 1):
            if off != 0:
                acc = acc + _shift_rows(x, off, row)
        cnt = (jnp.minimum(row + b + 1, n) - jnp.maximum(row - a, 0)).astype(F32)
        pooled = acc / cnt - x
        y = _dot(pooled.astype(BF16), w_ref[gi].astype(BF16))
        o_ref[:, cols] = (y * sc_ref[:, cols]).astype(o_ref.dtype)


def _pool(proj, pool_w, pool_scale, row0, nseq, seqlen):
    blk0 = row0 // seqlen
    return pl.pallas_call(
        _pool_kernel,
        grid=(nseq,),
        in_specs=[
            pl.BlockSpec((seqlen, POOL_W), lambda s: (blk0 + s, CB_PIN // len(POOL_WINDOWS))),
            pl.BlockSpec((len(POOL_WINDOWS), POOL_GW, POOL_GW), lambda s: (0, 0, 0)),
            pl.BlockSpec((1, POOL_W), lambda s: (0, 0)),
        ],
        out_specs=pl.BlockSpec((seqlen, POOL_W), lambda s: (s, 0)),
        out_shape=jax.ShapeDtypeStruct((nseq * seqlen, POOL_W), BF16),
        compiler_params=_cparams("parallel"),
        name="pool",
    )(proj, pool_w, pool_scale.reshape(1, POOL_W))


def _mix_group(proj, proj_ab, lp, l, row0, nseq, seqlen, ctx):
    qkv = _gdn_prep(proj, lp["gdn_conv"], row0, nseq, seqlen)
    u, w, qg, kd, a, gl = _gdn_intra(qkv, proj_ab, lp["alog_row"], lp["dt_row"], row0)
    o, s_fin = _gdn_scan(None if ctx is None else ctx[2], u, w, qg, kd, a, gl, nseq, seqlen)
    o_gdn = _gdn_post(o, proj, lp["gdn_norm"], row0)

    tq = min(4 * ATTN_SUB, seqlen)
    nqb = seqlen // tq
    qb0 = row0 // tq
    sb0 = row0 // seqlen
    q_spec = pl.BlockSpec((tq, DIFF_DV), lambda b, h, i: (qb0 + b * nqb + i, CB_DQ + h))
    k_new = pl.BlockSpec((seqlen, DIFF_DV), lambda b, h, i: (sb0 + b, CB_DK + h))
    v_new = pl.BlockSpec((seqlen, DIFF_DV), lambda b, h, i: (sb0 + b, CB_DV + h))
    if ctx is None:
        k_arrs, k_specs, v_arrs, v_specs, seg_lens, rope_tabs = [proj], [k_new], [proj], [v_new], [seqlen], None
    else:
        cache_k, cache_v = ctx[0], ctx[1]
        past = cache_k.shape[2]
        cached = pl.BlockSpec((None, None, past, DIFF_DV), lambda b, h, i: (b, l, 0, h))
        k_arrs, k_specs, v_arrs, v_specs = [cache_k, proj], [cached, k_new], [cache_v, proj], [cached, v_new]
        seg_lens, rope_tabs = [past, seqlen], _rope_tables(seqlen)
    o_diff = _diff_attn(proj, q_spec, k_arrs, k_specs, v_arrs, v_specs, seg_lens, rope_tabs, lp["diff_lam"],
                        lp["diff_norm"], nseq, seqlen, tq, _lambda_init(l))

    o_pool = _pool(proj, lp["pool_w"], lp["pool_scale"], row0, nseq, seqlen)
    return (o_gdn, o_diff, o_pool), s_fin


def _split_w_in(w_in):
    depth, d, _ = w_in.shape
    o_ab = 4 * GDN_W
    o_dq = o_ab + 4 * GDN_HEADS
    pad = jnp.zeros((depth, d, LANE - 4 * GDN_HEADS), w_in.dtype)
    main = jnp.concatenate([w_in[..., :o_ab], w_in[..., o_dq:]], axis=-1)
    ab = jnp.concatenate([w_in[..., o_ab:o_dq], pad], axis=-1)
    return main.astype(BF16), ab.astype(BF16)


def kernel(x_prompt, x_sample, c, cache_k, cache_v, state_gdn, c_ctx, w_ada, b_ada, norm_ffn1, ffn1_in,
           ffn1_out, norm_mix, w_in, gdn_conv, gdn_a_log, gdn_dt_bias, gdn_norm, diff_lam, diff_norm,
           pool_w, pool_scale, w_out, norm_ffn2, ffn2_in, ffn2_out, final_norm):
    batch, seq, d = x_prompt.shape
    dec_batch, dec_seq, _ = x_sample.shape
    depth = w_ada.shape[0]
    n_ctx = batch * seq
    n_dec = dec_batch * dec_seq
    layout = (n_ctx, dec_seq)
    past = cache_k.shape[2]

    x = jnp.concatenate([x_prompt.reshape(n_ctx, d), x_sample.reshape(n_dec, d)], axis=0)
    cond8 = jnp.concatenate([c_ctx[None, :], c, jnp.zeros((8 - 1 - dec_batch, d), F32)], axis=0)
    ada = _ada(cond8, w_ada, b_ada)
    cache_k4 = cache_k.reshape(dec_batch, depth, past, DIFF_W)
    cache_v4 = cache_v.reshape(dec_batch, depth, past, DIFF_W)
    w_in_main, w_in_ab = _split_w_in(w_in)
    w_out_b = w_out.astype(BF16)
    ffn_w = [(ffn1_in.astype(BF16), ffn1_out.astype(BF16)), (ffn2_in.astype(BF16), ffn2_out.astype(BF16))]

    ks, vs, ss = [], [], []
    for l in range(depth):
        ada3 = ada[l, :1 + dec_batch].reshape(1 + dec_batch, 1, N_MOD * d)
        lane_pad = jnp.zeros((LANE - 2 * GDN_HEADS,), F32)
        lp = dict(
            gdn_conv=gdn_conv[l], gdn_norm=gdn_norm[l], diff_lam=diff_lam[l], diff_norm=diff_norm[l],
            pool_w=pool_w[l], pool_scale=pool_scale[l],
            alog_row=jnp.concatenate([gdn_a_log[l].reshape(-1), lane_pad]).reshape(1, LANE),
            dt_row=jnp.concatenate([gdn_dt_bias[l].reshape(-1), lane_pad]).reshape(1, LANE),
        )
        act = _ffn_in(x, ada3, (0, 1), norm_ffn1[l], ffn_w[0][0], l, layout=layout)
        x = _ffn_out(act, ffn_w[0][1], l, x, ada3, 2, layout=layout)

        proj, proj_ab = _mix_in(x, ada3, (3, 4), norm_mix[l], w_in_main, w_in_ab, l, layout=layout)
        acts_c, s_c = _mix_group(proj, proj_ab, lp, l, 0, batch, seq, None)
        acts_d, _ = _mix_group(proj, proj_ab, lp, l, n_ctx, dec_batch, dec_seq,
                               (cache_k4, cache_v4, state_gdn[:, l]))
        x = _mix_out(acts_c, w_out_b, l, x, ada3, 5, 0, layout=layout)
        x = _mix_out(acts_d, w_out_b, l, x, ada3, 5, n_ctx, layout=layout)
        ks.append(proj[:n_ctx, CB_DK * LANE:CB_DK * LANE + DIFF_W].reshape(batch, seq, DIFF_HEADS, DIFF_DV))
        vs.append(proj[:n_ctx, CB_DV * LANE:CB_DV * LANE + DIFF_W].reshape(batch, seq, DIFF_HEADS, DIFF_DV))
        ss.append(s_c)

        act = _ffn_in(x, ada3, (6, 7), norm_ffn2[l], ffn_w[1][0], l, layout=layout)
        x = _ffn_out(act, ffn_w[1][1], l, x, ada3, 8, layout=layout)

    y_prompt = _final_norm(x, final_norm, 0, n_ctx).reshape(batch, seq, d)
    y_sample = _final_norm(x, final_norm, n_ctx, n_dec).reshape(dec_batch, dec_seq, d)
    return (y_prompt, y_sample, jnp.stack(ks, axis=1), jnp.stack(vs, axis=1), jnp.stack(ss, axis=1))
```

```python
import functools
import math

import jax
import jax.numpy as jnp
from jax import lax
from jax.experimental import pallas as pl
from jax.experimental.pallas import tpu as pltpu

F32 = jnp.float32
BF16 = jnp.bfloat16

GRID_W = 64
GDN_HEADS = 4
GDN_DK = 128
GDN_DV = 128
GDN_W = GDN_HEADS * GDN_DV
CONV_K = 4
CHUNK = 64
DIFF_HEADS = 8
DIFF_DQK = 64
DIFF_DV = 2 * DIFF_DQK
DIFF_W = DIFF_HEADS * DIFF_DV
ROPE_BASE = 10000.0
POOL_WINDOWS = (2, 4, 8, 16)
POOL_GW = 128
POOL_W = len(POOL_WINDOWS) * POOL_GW
MIX_W = GDN_W + DIFF_W + POOL_W
N_MOD = 9
EPS = 1e-6

LANE = 128
GDN_BLK = 4 * CHUNK
ATTN_SUB = 256
PRO_SUB = 256
VMEM_LIMIT = 56 * 1024 * 1024

CB_Q, CB_K, CB_V, CB_Z = 0, 4, 8, 12
CB_DQ, CB_DK, CB_DV, CB_PIN = 16, 24, 32, 40
PROJ_MAIN = 44 * LANE


def _lambda_init(l):
    return 0.8 - 0.6 * math.exp(-0.3 * l)


def _cparams(*sem):
    return pltpu.CompilerParams(dimension_semantics=sem, vmem_limit_bytes=VMEM_LIMIT)


def _dot(a, b):
    return lax.dot_general(a, b, (((1,), (0,)), ((), ())), preferred_element_type=F32)


def _dot_nt(a, b):
    return lax.dot_general(a, b, (((1,), (1,)), ((), ())), preferred_element_type=F32)


def _dot_tn(a, b):
    return lax.dot_general(a, b, (((0,), (0,)), ((), ())), preferred_element_type=F32)


def _mm(a, b, dot=_dot):
    return dot(a.astype(BF16), b.astype(BF16))


def _dot_exact_lhs(m, b):
    b0 = b.astype(BF16)
    r = b - b0.astype(F32)
    b1 = r.astype(BF16)
    b2 = (r - b1.astype(F32)).astype(BF16)
    return _dot(m, b0) + (_dot(m, b1) + _dot(m, b2))


def _sigmoid(x):
    return 0.5 * (1.0 + jnp.tanh(0.5 * x))


def _silu(x):
    return x * _sigmoid(x)


def _ada_kernel(cond_ref, w_ref, b_ref, o_ref):
    s = _silu(cond_ref[...]).astype(BF16)
    o_ref[...] = _dot(s, w_ref[...].astype(BF16)) + b_ref[...]


def _ada(cond8, w_ada, b_ada):
    depth, d, n = w_ada.shape
    tn = 1024
    return pl.pallas_call(
        _ada_kernel,
        grid=(depth, n // tn),
        in_specs=[
            pl.BlockSpec((8, d), lambda l, j: (0, 0)),
            pl.BlockSpec((None, d, tn), lambda l, j: (l, 0, j)),
            pl.BlockSpec((None, 1, tn), lambda l, j: (l, 0, j)),
        ],
        out_specs=pl.BlockSpec((None, 8, tn), lambda l, j: (l, 0, j)),
        out_shape=jax.ShapeDtypeStruct((depth, 8, n), F32),
        compiler_params=_cparams("parallel", "parallel"),
        name="ada",
    )(cond8, w_ada, b_ada.reshape(depth, 1, n))


def _norm_mod(x, g, shift, scale):
    y = x * lax.rsqrt(jnp.mean(x * x, axis=-1, keepdims=True) + EPS) * g
    return y * (1.0 + scale) + shift


def _norm_mod_rows(x_ref, sh_ref, sc_ref, g_ref, rows):
    return _norm_mod(x_ref[rows, :], g_ref[...], sh_ref[0], sc_ref[0]).astype(BF16)


def _row_subs(n):
    return [slice(r * PRO_SUB, (r + 1) * PRO_SUB) for r in range(n // PRO_SUB)]


def _mix_in_kernel(x_ref, sh_ref, sc_ref, g_ref, w_ref, wab_ref, o_ref, ab_ref, h_ref):
    @pl.when(pl.program_id(1) == 0)
    def _():
        for rows in _row_subs(x_ref.shape[0]):
            h = _norm_mod_rows(x_ref, sh_ref, sc_ref, g_ref, rows)
            h_ref[rows, :] = h
            ab_ref[rows, :] = _dot(h, wab_ref[...])
            o_ref[rows, :] = _dot(h, w_ref[...])

    @pl.when(pl.program_id(1) != 0)
    def _():
        o_ref[...] = _dot(h_ref[...], w_ref[...])


def _ffn_in_kernel(x_ref, sh_ref, sc_ref, g_ref, wg_ref, wu_ref, o_ref, h_ref):
    def swiglu(h):
        gate = _dot(h, wg_ref[...].astype(BF16))
        up = _dot(h, wu_ref[...].astype(BF16))
        return (_silu(gate) * up).astype(o_ref.dtype)

    @pl.when(pl.program_id(1) == 0)
    def _():
        for rows in _row_subs(x_ref.shape[0]):
            h = _norm_mod_rows(x_ref, sh_ref, sc_ref, g_ref, rows)
            h_ref[rows, :] = h
            o_ref[rows, :] = swiglu(h)

    @pl.when(pl.program_id(1) != 0)
    def _():
        o_ref[...] = swiglu(h_ref[...])


def _row_tile(groups, cap):
    tm = cap
    while any(g % tm for g in groups):
        tm //= 2
    return tm


def _mod_row(n_ctx, dec_seq, tm, row0=0):
    def f(i):
        r = row0 + i * tm
        return jnp.where(r < n_ctx, 0, 1 + (r - n_ctx) // dec_seq)
    return f


def _pm_specs(d, tm, row, mod_cols):
    sh_c, sc_c = mod_cols
    return [
        pl.BlockSpec((tm, d), lambda i, j: (i, 0)),
        pl.BlockSpec((1, 1, d), lambda i, j: (row(i), 0, sh_c)),
        pl.BlockSpec((1, 1, d), lambda i, j: (row(i), 0, sc_c)),
        pl.BlockSpec((1, d), lambda i, j: (0, 0)),
    ]


def _ffn_in(x, ada3, mod_cols, g, w, l, *, layout):
    t, d = x.shape
    n_ctx, dec_seq = layout
    tm = _row_tile((n_ctx, dec_seq), 1024)
    row = _mod_row(n_ctx, dec_seq, tm)
    n = w.shape[2] // 2
    tn = 512 if n % 512 == 0 else n
    nj = n // tn
    return pl.pallas_call(
        _ffn_in_kernel,
        grid=(t // tm, nj),
        in_specs=_pm_specs(d, tm, row, mod_cols) + [
            pl.BlockSpec((None, d, tn), lambda i, j: (l, 0, j)),
            pl.BlockSpec((None, d, tn), lambda i, j: (l, 0, j + nj)),
        ],
        out_specs=pl.BlockSpec((tm, tn), lambda i, j: (i, j)),
        out_shape=jax.ShapeDtypeStruct((t, n), BF16),
        scratch_shapes=[pltpu.VMEM((tm, d), BF16)],
        compiler_params=_cparams("parallel", "arbitrary"),
        name="ffn_in",
    )(x, ada3, ada3, g.reshape(1, d), w, w)


def _mix_in(x, ada3, mod_cols, g, w_main, w_ab, l, *, layout):
    t, d = x.shape
    n_ctx, dec_seq = layout
    tm = _row_tile((n_ctx, dec_seq), 1024)
    row = _mod_row(n_ctx, dec_seq, tm)
    tn = 512
    return pl.pallas_call(
        _mix_in_kernel,
        grid=(t // tm, PROJ_MAIN // tn),
        in_specs=_pm_specs(d, tm, row, mod_cols) + [
            pl.BlockSpec((None, d, tn), lambda i, j: (l, 0, j)),
            pl.BlockSpec((None, d, LANE), lambda i, j: (l, 0, 0)),
        ],
        out_specs=[pl.BlockSpec((tm, tn), lambda i, j: (i, j)),
                   pl.BlockSpec((tm, LANE), lambda i, j: (i, 0))],
        out_shape=[jax.ShapeDtypeStruct((t, PROJ_MAIN), F32), jax.ShapeDtypeStruct((t, LANE), F32)],
        scratch_shapes=[pltpu.VMEM((tm, d), BF16)],
        compiler_params=_cparams("parallel", "arbitrary"),
        name="mix_in",
    )(x, ada3, ada3, g.reshape(1, d), w_main, w_ab)


def _ffn_out_kernel(a_ref, w_ref, x_ref, gate_ref, o_ref):
    y = _dot(a_ref[...], w_ref[...])
    o_ref[...] = x_ref[...] + (0.5 * gate_ref[0]) * y


def _ffn_out(a, w, l, x, ada3, gate_col, *, layout):
    t, k = a.shape
    d = w.shape[2]
    n_ctx, dec_seq = layout
    tm = _row_tile((n_ctx, dec_seq), 1024)
    tn = min(512, d)
    row = _mod_row(n_ctx, dec_seq, tm)
    gpb = d // tn
    return pl.pallas_call(
        _ffn_out_kernel,
        grid=(t // tm, d // tn),
        in_specs=[
            pl.BlockSpec((tm, k), lambda i, j: (i, 0)),
            pl.BlockSpec((None, k, tn), lambda i, j: (l, 0, j)),
            pl.BlockSpec((tm, tn), lambda i, j: (i, j)),
            pl.BlockSpec((1, 1, tn), lambda i, j: (row(i), 0, gate_col * gpb + j)),
        ],
        out_specs=pl.BlockSpec((tm, tn), lambda i, j: (i, j)),
        out_shape=jax.ShapeDtypeStruct((t, d), F32),
        compiler_params=_cparams("parallel", "parallel"),
        name="ffn_out",
    )(a, w, x, ada3)


def _mix_out_kernel(a1_ref, a2_ref, a3_ref, w_ref, x_ref, gate_ref, o_ref):
    y = (_dot(a1_ref[...], w_ref[0:GDN_W, :])
         + _dot(a2_ref[...], w_ref[GDN_W:GDN_W + DIFF_W, :])
         + _dot(a3_ref[...], w_ref[GDN_W + DIFF_W:MIX_W, :]))
    o_ref[...] = x_ref[...] + gate_ref[0] * y


def _mix_out(acts, w, l, x, ada3, gate_col, row0, *, layout):
    a1, a2, a3 = acts
    n = a1.shape[0]
    t, d = x.shape
    n_ctx, dec_seq = layout
    tm = _row_tile((n_ctx, dec_seq), 512)
    tn = min(512, d)
    row = _mod_row(n_ctx, dec_seq, tm, row0)
    rb0 = row0 // tm
    gpb = d // tn
    return pl.pallas_call(
        _mix_out_kernel,
        grid=(n // tm, d // tn),
        in_specs=[
            pl.BlockSpec((tm, GDN_W), lambda i, j: (i, 0)),
            pl.BlockSpec((tm, DIFF_W), lambda i, j: (i, 0)),
            pl.BlockSpec((tm, POOL_W), lambda i, j: (i, 0)),
            pl.BlockSpec((None, MIX_W, tn), lambda i, j: (l, 0, j)),
            pl.BlockSpec((tm, tn), lambda i, j: (rb0 + i, j)),
            pl.BlockSpec((1, 1, tn), lambda i, j: (row(i), 0, gate_col * gpb + j)),
        ],
        out_specs=pl.BlockSpec((tm, tn), lambda i, j: (rb0 + i, j)),
        out_shape=jax.ShapeDtypeStruct((t, d), F32),
        input_output_aliases={4: 0},
        compiler_params=_cparams("parallel", "parallel"),
        name="mix_out",
    )(a1, a2, a3, w, x, ada3)


def _final_norm_kernel(x_ref, g_ref, o_ref):
    x = x_ref[...]
    o_ref[...] = x * lax.rsqrt(jnp.mean(x * x, axis=-1, keepdims=True) + EPS) * g_ref[...]


def _final_norm(x, g, row0, nrows):
    d = x.shape[1]
    tm = 256
    rb0 = row0 // tm
    return pl.pallas_call(
        _final_norm_kernel,
        grid=(nrows // tm,),
        in_specs=[pl.BlockSpec((tm, d), lambda i: (rb0 + i, 0)), pl.BlockSpec((1, d), lambda i: (0, 0))],
        out_specs=pl.BlockSpec((tm, d), lambda i: (i, 0)),
        out_shape=jax.ShapeDtypeStruct((nrows, d), F32),
        compiler_params=_cparams("parallel"),
        name="final_norm",
    )(x, g.reshape(1, d))


def _shift_rows(x, off, row):
    if off == 0:
        return x
    n = x.shape[0]
    y = pltpu.roll(x, (-off) % n, 0)
    ok = (row + off >= 0) & (row + off < n)
    return jnp.where(ok, y, 0.0)


def _gdn_prep_kernel(x_ref, w_ref, o_ref):
    x = x_ref[...]
    n = x.shape[0]
    left = CONV_K // 2
    edge = 16

    def conv(xb, shift):
        acc = jnp.zeros_like(xb)
        for j in range(CONV_K):
            acc = acc + shift(xb, j - left) * w_ref[j:j + 1, :]
        return acc

    acc = conv(x, lambda xb, off: xb if off == 0 else pltpu.roll(xb, (-off) % n, 0))
    row = lax.broadcasted_iota(jnp.int32, (edge, LANE), 0)
    head = conv(x[0:edge], lambda xb, off: _shift_rows(xb, off, row))
    tail = conv(x[n - edge:n], lambda xb, off: _shift_rows(xb, off, row))
    acc = jnp.concatenate([head[0:8], acc[8:n - 8], tail[edge - 8:edge]], axis=0)
    y = _silu(acc)
    kind = pl.program_id(1) // GDN_HEADS
    inv = lax.rsqrt(jnp.sum(y * y, axis=-1, keepdims=True) + EPS)
    inv = inv * jnp.where(kind == 0, GDN_DK ** -0.5, 1.0)
    o_ref[...] = jnp.where(kind < 2, y * inv, y)


def _gdn_prep(proj, conv_w, row0, nseq, seqlen):
    blk0 = row0 // seqlen
    return pl.pallas_call(
        _gdn_prep_kernel,
        grid=(nseq, 3 * GDN_HEADS),
        in_specs=[
            pl.BlockSpec((seqlen, LANE), lambda s, c: (blk0 + s, CB_Q + c)),
            pl.BlockSpec((CONV_K, LANE), lambda s, c: (0, c)),
        ],
        out_specs=pl.BlockSpec((seqlen, LANE), lambda s, c: (s, c)),
        out_shape=jax.ShapeDtypeStruct((nseq * seqlen, 3 * GDN_W), F32),
        compiler_params=_cparams("parallel", "parallel"),
        name="gdn_prep",
    )(proj, conv_w)


def _unit_tri_solve_many(lmats, rhss, eye):
    xs = [eye - l for l in lmats]
    ps = [_mm(l, l) for l in lmats]
    steps = int(math.log2(CHUNK)) - 1
    for s in range(steps):
        xs = [x + _mm(x, p) for x, p in zip(xs, ps)]
        if s + 1 < steps:
            ps = [_mm(p, p) for p in ps]
    x0s = [_mm(x, r) for x, r in zip(xs, rhss)]
    resids = [r - (x0 + _mm(l, x0)) for l, r, x0 in zip(lmats, rhss, x0s)]
    return [x0 + _mm(x, rs) for x, x0, rs in zip(xs, x0s, resids)]


def _gdn_intra_kernel(q_ref, k_ref, v_ref, ab_ref, alog_ref, dt_ref,
                      u_ref, w_ref, qg_ref, kd_ref, a_ref, gl_ref):
    n = GDN_BLK
    ab = ab_ref[...]
    lane = lax.broadcasted_iota(jnp.int32, ab.shape, 1)
    sp = jnp.maximum(ab + dt_ref[...], 0.0) + jnp.log(1.0 + jnp.exp(-jnp.abs(ab + dt_ref[...])))
    gmat = -jnp.exp(alog_ref[...]) * sp
    bmat = _sigmoid(ab)

    ri = lax.broadcasted_iota(jnp.int32, (n, n), 0)
    ci = lax.broadcasted_iota(jnp.int32, (n, n), 1)
    same = (ri // CHUNK) == (ci // CHUNK)
    eye = (ri == ci).astype(F32)
    r8 = lax.broadcasted_iota(jnp.int32, (8, LANE), 0)
    gtot_all = _dot_exact_lhs(same.astype(BF16), gmat)
    gcf_all = _dot_exact_lhs((same & (ci <= ri)).astype(BF16), gmat)
    gc_dirs = (gcf_all, (gtot_all - gcf_all) + gmat)
    incl_dirs = (same & (ci <= ri), same & (ci >= ri))
    strict_dirs = (same & (ci < ri), same & (ci > ri))

    lmats, rhss, dests = [], [], []
    for h in range(GDN_HEADS):
        cols = slice(h * LANE, (h + 1) * LANE)
        q = q_ref[:, cols]
        k = k_ref[:, cols]
        v = v_ref[:, cols]
        kk = _mm(k, k, _dot_nt)
        qk = _mm(q, k, _dot_nt)
        for d in range(2):
            incl = incl_dirs[d]
            sel_g = lane == d * GDN_HEADS + h
            sel_b = lane == 2 * GDN_HEADS + d * GDN_HEADS + h
            gc = jnp.sum(jnp.where(sel_g, gc_dirs[d], 0.0), axis=1, keepdims=True)
            gtot = jnp.sum(jnp.where(sel_g, gtot_all, 0.0), axis=1, keepdims=True)
            beta = jnp.sum(jnp.where(sel_b, bmat, 0.0), axis=1, keepdims=True)
            gc_b = jnp.broadcast_to(gc, (n, LANE))
            gc_row = jnp.transpose(gc_b)[0:1, :]
            diff = jnp.where(incl, gc - gc_row, 0.0)
            decay = jnp.where(incl, jnp.exp(diff), 0.0)
            eg = jnp.exp(gc_b)
            lmats.append(jnp.where(strict_dirs[d], kk * beta * decay, 0.0))
            rhss.append(jnp.concatenate([v * beta, (k * beta) * eg], axis=1))
            dests.append((d, cols))
            qg_ref[d, :, cols] = q * eg
            kd_ref[d, :, cols] = k * jnp.exp(jnp.broadcast_to(gtot - gc, (n, LANE)))
            amat = qk * decay
            a_ref[d, h] = (amat[:, 0:CHUNK] + amat[:, CHUNK:2 * CHUNK]) + (
                amat[:, 2 * CHUNK:3 * CHUNK] + amat[:, 3 * CHUNK:4 * CHUNK])
            egt = jnp.exp(jnp.broadcast_to(gtot, (n, LANE)))
            gl8 = jnp.zeros((8, LANE), F32)
            for c in range(n // CHUNK):
                gl8 = jnp.where(r8 == c, egt[c * CHUNK:c * CHUNK + 8, :], gl8)
            gl_ref[d, :, cols] = gl8

    for (d, cols), uw in zip(dests, _unit_tri_solve_many(lmats, rhss, eye)):
        u_ref[d, :, cols] = uw[:, :GDN_DV]
        w_ref[d, :, cols] = uw[:, GDN_DV:]


def _gdn_intra(qkv, proj_ab, alog_row, dt_row, row0):
    t = qkv.shape[0]
    nb = t // GDN_BLK
    pb0 = row0 // GDN_BLK
    big = jax.ShapeDtypeStruct((2, t, GDN_W), F32)
    big_spec = pl.BlockSpec((2, GDN_BLK, GDN_W), lambda i: (0, i, 0))
    return pl.pallas_call(
        _gdn_intra_kernel,
        grid=(nb,),
        in_specs=[
            pl.BlockSpec((GDN_BLK, GDN_W), lambda i: (i, 0)),
            pl.BlockSpec((GDN_BLK, GDN_W), lambda i: (i, 1)),
            pl.BlockSpec((GDN_BLK, GDN_W), lambda i: (i, 2)),
            pl.BlockSpec((GDN_BLK, LANE), lambda i: (pb0 + i, 0)),
            pl.BlockSpec((1, LANE), lambda i: (0, 0)),
            pl.BlockSpec((1, LANE), lambda i: (0, 0)),
        ],
        out_specs=[big_spec, big_spec, big_spec, big_spec,
                   pl.BlockSpec((2, GDN_HEADS, GDN_BLK, CHUNK), lambda i: (0, 0, i, 0)),
                   pl.BlockSpec((2, 8, GDN_W), lambda i: (0, i, 0))],
        out_shape=[big, big, big, big,
                   jax.ShapeDtypeStruct((2, GDN_HEADS, t, CHUNK), F32),
                   jax.ShapeDtypeStruct((2, nb * 8, GDN_W), F32)],
        compiler_params=_cparams("parallel"),
        name="gdn_intra",
    )(qkv, qkv, qkv, proj_ab, alog_row, dt_row)


def _gdn_scan_kernel(*refs, has_init):
    if has_init:
        s0_ref, refs = refs[0], refs[1:]
    ins = (refs[0:6], refs[6:12])
    o_refs, sf_ref, s_ref = refs[12:14], refs[14], refs[15]
    blk = pl.program_id(1)
    nchunk = GDN_BLK // CHUNK

    @pl.when(blk == 0)
    def _():
        if has_init:
            s_ref[...] = s0_ref[...]
        else:
            s_ref[...] = jnp.zeros_like(s_ref)

    U, W, QG, KD, A, GL = range(6)
    chains = [(d, h) for d in range(2) for h in range(GDN_HEADS)]
    states = [s_ref[d, h] for d, h in chains]
    for c in range(nchunk):
        ccs = [c if d == 0 else nchunk - 1 - c for d, _ in chains]
        rows = [slice(cc * CHUNK, (cc + 1) * CHUNK) for cc in ccs]
        cols = [slice(h * LANE, (h + 1) * LANE) for _, h in chains]
        n = range(len(chains))
        ws = [_mm(ins[chains[i][0]][W][rows[i], cols[i]], states[i]) for i in n]
        qs = [_mm(ins[chains[i][0]][QG][rows[i], cols[i]], states[i]) for i in n]
        v_news = [ins[chains[i][0]][U][rows[i], cols[i]] - ws[i] for i in n]
        for i in n:
            d, h = chains[i]
            o_refs[d][rows[i], cols[i]] = qs[i] + _mm(ins[d][A][h, rows[i], :], v_news[i])
        states = [states[i] * ins[chains[i][0]][GL][ccs[i]:ccs[i] + 1, cols[i]]
                  + _mm(ins[chains[i][0]][KD][rows[i], cols[i]], v_news[i], _dot_tn) for i in n]
    for (d, h), s in zip(chains, states):
        s_ref[d, h] = s

    @pl.when(blk == pl.num_programs(1) - 1)
    def _():
        sf_ref[...] = s_ref[...]


def _gdn_scan(s0, u, w, qg, kd, a, gl, nseq, seqlen):
    t = u.shape[1]
    nblk = seqlen // GDN_BLK
    in_specs, args = [], []
    for d in range(2):
        rb = (lambda s, b: s * nblk + b) if d == 0 else (lambda s, b: s * nblk + nblk - 1 - b)
        big = pl.BlockSpec((None, GDN_BLK, GDN_W), lambda s, b, d=d, rb=rb: (d, rb(s, b), 0))
        in_specs += [big, big, big, big,
                     pl.BlockSpec((None, GDN_HEADS, GDN_BLK, CHUNK), lambda s, b, d=d, rb=rb: (d, 0, rb(s, b), 0)),
                     pl.BlockSpec((None, 8, GDN_W), lambda s, b, d=d, rb=rb: (d, rb(s, b), 0))]
        args += [u, w, qg, kd, a, gl]
    state_spec = pl.BlockSpec((None, 2, GDN_HEADS, GDN_DK, GDN_DV), lambda s, b: (s, 0, 0, 0, 0))
    if s0 is not None:
        in_specs.insert(0, state_spec)
        args.insert(0, s0)
    out_specs = [pl.BlockSpec((GDN_BLK, GDN_W), lambda s, b: (s * nblk + b, 0)),
                 pl.BlockSpec((GDN_BLK, GDN_W), lambda s, b: (s * nblk + nblk - 1 - b, 0)),
                 state_spec]
    return pl.pallas_call(
        functools.partial(_gdn_scan_kernel, has_init=s0 is not None),
        grid=(nseq, nblk),
        in_specs=in_specs,
        out_specs=out_specs,
        out_shape=[jax.ShapeDtypeStruct((t, GDN_W), F32), jax.ShapeDtypeStruct((t, GDN_W), F32),
                   jax.ShapeDtypeStruct((nseq, 2, GDN_HEADS, GDN_DK, GDN_DV), F32)],
        scratch_shapes=[pltpu.VMEM((2, GDN_HEADS, GDN_DK, GDN_DV), F32)],
        compiler_params=_cparams("parallel", "arbitrary"),
        name="gdn_scan",
    )(*args)


def _gdn_post_kernel(of_ref, ob_ref, z_ref, g_ref, out_ref):
    for h in range(GDN_HEADS):
        cols = slice(h * LANE, (h + 1) * LANE)
        o = of_ref[:, cols] + ob_ref[:, cols]
        y = o * lax.rsqrt(jnp.mean(o * o, axis=-1, keepdims=True) + EPS) * g_ref[...]
        out_ref[:, cols] = (y * _silu(z_ref[:, cols])).astype(out_ref.dtype)


def _gdn_post(o_f, o_b, proj, g, row0):
    t = o_f.shape[0]
    tm = 256
    pb0 = row0 // tm
    return pl.pallas_call(
        _gdn_post_kernel,
        grid=(t // tm,),
        in_specs=[
            pl.BlockSpec((tm, GDN_W), lambda i: (i, 0)),
            pl.BlockSpec((tm, GDN_W), lambda i: (i, 0)),
            pl.BlockSpec((tm, GDN_W), lambda i: (pb0 + i, CB_Z // GDN_HEADS)),
            pl.BlockSpec((1, GDN_DV), lambda i: (0, 0)),
        ],
        out_specs=pl.BlockSpec((tm, GDN_W), lambda i: (i, 0)),
        out_shape=jax.ShapeDtypeStruct((t, GDN_W), BF16),
        compiler_params=_cparams("parallel"),
        name="gdn_post",
    )(o_f, o_b, proj, g.reshape(1, GDN_DV))


def _rope_rotate(x, cos, sin_signed):
    lane = lax.broadcasted_iota(jnp.int32, x.shape, 1)
    first = (lane % 32) < 16
    partner = jnp.where(first, pltpu.roll(x, LANE - 16, 1), pltpu.roll(x, 16, 1))
    return x * cos + partner * sin_signed


def _rope_tables(seqlen):
    pos = jnp.arange(seqlen)
    rowp = (pos // GRID_W).astype(F32)
    colp = (pos % GRID_W).astype(F32)
    half = DIFF_DQK // 2
    inv_freq = ROPE_BASE ** (-jnp.arange(0, half, 2, dtype=F32) / half)
    c = jnp.arange(LANE)
    c64 = c % DIFF_DQK
    p = jnp.where((c64 // half)[None, :] == 0, rowp[:, None], colp[:, None])
    ang = p * inv_freq[c64 % (half // 2)][None, :]
    sign = jnp.where((c64 % half) < half // 2, -1.0, 1.0).astype(F32)
    return jnp.cos(ang), jnp.sin(ang) * sign[None, :]


def _diff_attn_kernel(*refs, seg_lens, lam_init, rope):
    nseg = len(seg_lens)
    q_ref = refs[0]
    k_refs = refs[1:1 + nseg]
    v_refs = refs[1 + nseg:1 + 2 * nseg]
    if rope:
        cosq_ref, sinq_ref, cosk_ref, sink_ref = refs[1 + 2 * nseg:5 + 2 * nseg]
    lam_ref, g_ref, o_ref, kb_ref, vb_ref = refs[-5:]
    offs = [sum(seg_lens[:s]) for s in range(nseg)]

    @pl.when(pl.program_id(2) == 0)
    def _():
        vb_ref[:, DIFF_DV:] = jnp.ones((vb_ref.shape[0], DIFF_DV), BF16)
        for s in range(nseg):
            k = k_refs[s][...]
            if rope and s == nseg - 1:
                k = _rope_rotate(k, cosk_ref[...], sink_ref[...])
            kb_ref[offs[s]:offs[s] + seg_lens[s], :] = k.astype(BF16)
            vb_ref[offs[s]:offs[s] + seg_lens[s], 0:DIFF_DV] = v_refs[s][...].astype(BF16)

    lv = lam_ref[...]
    lam = (jnp.exp(jnp.sum(lv[0:1] * lv[1:2], axis=1, keepdims=True))
           - jnp.exp(jnp.sum(lv[2:3] * lv[3:4], axis=1, keepdims=True)) + lam_init)
    kb = kb_ref[...]
    vb = vb_ref[...]
    lane = lax.broadcasted_iota(jnp.int32, (ATTN_SUB, DIFF_DV), 1)

    def scores(r):
        rows = slice(r * ATTN_SUB, (r + 1) * ATTN_SUB)
        q = q_ref[rows, :]
        if rope:
            q = _rope_rotate(q, cosq_ref[rows, :], sinq_ref[rows, :])
        q = q * (DIFF_DQK ** -0.5)
        return [_dot_nt(jnp.where((lane // DIFF_DQK) == m, q, 0.0).astype(BF16), kb) for m in range(2)]

    nsub = q_ref.shape[0] // ATTN_SUB
    ss_next = scores(0)
    for r in range(nsub):
        ss = ss_next
        if r + 1 < nsub:
            ss_next = scores(r + 1)
        es = [jnp.exp(s - jnp.max(s, axis=1, keepdims=True)).astype(BF16) for s in ss]
        accs = [_dot(e, vb) for e in es]
        outs = [acc[:, 0:DIFF_DV] / acc[:, DIFF_DV:] for acc in accs]
        o = outs[0] - lam * outs[1]
        y = o * lax.rsqrt(jnp.mean(o * o, axis=-1, keepdims=True) + EPS) * g_ref[...]
        o_ref[r * ATTN_SUB:(r + 1) * ATTN_SUB, :] = (y * (1.0 - lam_init)).astype(o_ref.dtype)


def _diff_attn(q_arr, q_spec, k_arrs, k_specs, v_arrs, v_specs, seg_lens, rope_tabs, lam_vec, g, nseq, seqlen,
               tq, lam_init):
    nqb = seqlen // tq
    lk = sum(seg_lens)
    rope_arrs, rope_specs = [], []
    if rope_tabs is not None:
        cos, sin = rope_tabs
        rope_arrs = [cos, sin, cos, sin]
        rope_specs = [pl.BlockSpec((tq, DIFF_DV), lambda b, h, i: (i, 0))] * 2 + [
            pl.BlockSpec((seqlen, DIFF_DV), lambda b, h, i: (0, 0))] * 2
    return pl.pallas_call(
        functools.partial(_diff_attn_kernel, seg_lens=tuple(seg_lens), lam_init=lam_init,
                          rope=rope_tabs is not None),
        grid=(nseq, DIFF_HEADS, nqb),
        in_specs=[q_spec, *k_specs, *v_specs, *rope_specs,
                  pl.BlockSpec((4, DIFF_DQK), lambda b, h, i: (0, 0)),
                  pl.BlockSpec((1, DIFF_DV), lambda b, h, i: (0, 0))],
        out_specs=pl.BlockSpec((tq, DIFF_DV), lambda b, h, i: (b * nqb + i, h)),
        out_shape=jax.ShapeDtypeStruct((nseq * seqlen, DIFF_W), BF16),
        scratch_shapes=[pltpu.VMEM((lk, DIFF_DV), BF16), pltpu.VMEM((lk, 2 * DIFF_DV), BF16)],
        compiler_params=_cparams("parallel", "parallel", "arbitrary"),
        name="diff_attn",
    )(q_arr, *k_arrs, *v_arrs, *rope_arrs, lam_vec, g.reshape(1, DIFF_DV))


def _pool_kernel(x_ref, w_ref, sc_ref, o_ref):
    n = x_ref.shape[0]
    row = lax.broadcasted_iota(jnp.int32, (n, POOL_GW), 0)
    for gi, win in enumerate(POOL_WINDOWS):
        cols = slice(gi * POOL_GW, (gi + 1) * POOL_GW)
        x = x_ref[:, cols]
        a = win // 2
        b = win - a - 1
        acc = x
        for off in range(-a, b + 1):
            if off != 0:
                acc = acc + _shift_rows(x, off, row)
        cnt = (jnp.minimum(row + b + 1, n) - jnp.maximum(row - a, 0)).astype(F32)
        pooled = acc / cnt - x
        y = _dot(pooled.astype(BF16), w_ref[gi].astype(BF16))
        o_ref[:, cols] = (y * sc_ref[:, cols]).astype(o_ref.dtype)


def _pool(proj, pool_w, pool_scale, row0, nseq, seqlen):
    blk0 = row0 // seqlen
    return pl.pallas_call(
        _pool_kernel,
        grid=(nseq,),
        in_specs=[
            pl.BlockSpec((seqlen, POOL_W), lambda s: (blk0 + s, CB_PIN // len(POOL_WINDOWS))),
            pl.BlockSpec((len(POOL_WINDOWS), POOL_GW, POOL_GW), lambda s: (0, 0, 0)),
            pl.BlockSpec((1, POOL_W), lambda s: (0, 0)),
        ],
        out_specs=pl.BlockSpec((seqlen, POOL_W), lambda s: (s, 0)),
        out_shape=jax.ShapeDtypeStruct((nseq * seqlen, POOL_W), BF16),
        compiler_params=_cparams("parallel"),
        name="pool",
    )(proj, pool_w, pool_scale.reshape(1, POOL_W))


def _mix_group(proj, proj_ab, lp, l, row0, nseq, seqlen, ctx):
    qkv = _gdn_prep(proj, lp["gdn_conv"], row0, nseq, seqlen)
    u, w, qg, kd, a, gl = _gdn_intra(qkv, proj_ab, lp["alog_row"], lp["dt_row"], row0)
    o_f, o_b, s_fin = _gdn_scan(None if ctx is None else ctx[2], u, w, qg, kd, a, gl, nseq, seqlen)
    o_gdn = _gdn_post(o_f, o_b, proj, lp["gdn_norm"], row0)

    tq = min(4 * ATTN_SUB, seqlen)
    nqb = seqlen // tq
    qb0 = row0 // tq
    sb0 = row0 // seqlen
    q_spec = pl.BlockSpec((tq, DIFF_DV), lambda b, h, i: (qb0 + b * nqb + i, CB_DQ + h))
    k_new = pl.BlockSpec((seqlen, DIFF_DV), lambda b, h, i: (sb0 + b, CB_DK + h))
    v_new = pl.BlockSpec((seqlen, DIFF_DV), lambda b, h, i: (sb0 + b, CB_DV + h))
    if ctx is None:
        k_arrs, k_specs, v_arrs, v_specs, seg_lens, rope_tabs = [proj], [k_new], [proj], [v_new], [seqlen], None
    else:
        cache_k, cache_v = ctx[0], ctx[1]
        past = cache_k.shape[2]
        cached = pl.BlockSpec((None, None, past, DIFF_DV), lambda b, h, i: (b, l, 0, h))
        k_arrs, k_specs, v_arrs, v_specs = [cache_k, proj], [cached, k_new], [cache_v, proj], [cached, v_new]
        seg_lens, rope_tabs = [past, seqlen], _rope_tables(seqlen)
    o_diff = _diff_attn(proj, q_spec, k_arrs, k_specs, v_arrs, v_specs, seg_lens, rope_tabs, lp["diff_lam"],
                        lp["diff_norm"], nseq, seqlen, tq, _lambda_init(l))

    o_pool = _pool(proj, lp["pool_w"], lp["pool_scale"], row0, nseq, seqlen)
    return (o_gdn, o_diff, o_pool), s_fin


def _split_w_in(w_in):
    depth, d, _ = w_in.shape
    o_ab = 4 * GDN_W
    o_dq = o_ab + 4 * GDN_HEADS
    pad = jnp.zeros((depth, d, LANE - 4 * GDN_HEADS), w_in.dtype)
    main = jnp.concatenate([w_in[..., :o_ab], w_in[..., o_dq:]], axis=-1)
    ab = jnp.concatenate([w_in[..., o_ab:o_dq], pad], axis=-1)
    return main.astype(BF16), ab.astype(BF16)


def kernel(x_prompt, x_sample, c, cache_k, cache_v, state_gdn, c_ctx, w_ada, b_ada, norm_ffn1, ffn1_in,
           ffn1_out, norm_mix, w_in, gdn_conv, gdn_a_log, gdn_dt_bias, gdn_norm, diff_lam, diff_norm,
           pool_w, pool_scale, w_out, norm_ffn2, ffn2_in, ffn2_out, final_norm):
    batch, seq, d = x_prompt.shape
    dec_batch, dec_seq, _ = x_sample.shape
    depth = w_ada.shape[0]
    n_ctx = batch * seq
    n_dec = dec_batch * dec_seq
    layout = (n_ctx, dec_seq)
    past = cache_k.shape[2]

    x = jnp.concatenate([x_prompt.reshape(n_ctx, d), x_sample.reshape(n_dec, d)], axis=0)
    cond8 = jnp.concatenate([c_ctx[None, :], c, jnp.zeros((8 - 1 - dec_batch, d), F32)], axis=0)
    ada = _ada(cond8, w_ada, b_ada)
    cache_k4 = cache_k.reshape(dec_batch, depth, past, DIFF_W)
    cache_v4 = cache_v.reshape(dec_batch, depth, past, DIFF_W)
    w_in_main, w_in_ab = _split_w_in(w_in)
    w_out_b = w_out.astype(BF16)
    ffn_w = [(ffn1_in, ffn1_out.astype(BF16)), (ffn2_in, ffn2_out.astype(BF16))]

    ks, vs, ss = [], [], []
    for l in range(depth):
        ada3 = ada[l, :1 + dec_batch].reshape(1 + dec_batch, 1, N_MOD * d)
        lane_pad = jnp.zeros((LANE - 2 * GDN_HEADS,), F32)
        lp = dict(
            gdn_conv=gdn_conv[l], gdn_norm=gdn_norm[l], diff_lam=diff_lam[l], diff_norm=diff_norm[l],
            pool_w=pool_w[l], pool_scale=pool_scale[l],
            alog_row=jnp.concatenate([gdn_a_log[l].reshape(-1), lane_pad]).reshape(1, LANE),
            dt_row=jnp.concatenate([gdn_dt_bias[l].reshape(-1), lane_pad]).reshape(1, LANE),
        )
        act = _ffn_in(x, ada3, (0, 1), norm_ffn1[l], ffn_w[0][0], l, layout=layout)
        x = _ffn_out(act, ffn_w[0][1], l, x, ada3, 2, layout=layout)

        proj, proj_ab = _mix_in(x, ada3, (3, 4), norm_mix[l], w_in_main, w_in_ab, l, layout=layout)
        acts_c, s_c = _mix_group(proj, proj_ab, lp, l, 0, batch, seq, None)
        acts_d, _ = _mix_group(proj, proj_ab, lp, l, n_ctx, dec_batch, dec_seq,
                               (cache_k4, cache_v4, state_gdn[:, l]))
        x = _mix_out(acts_c, w_out_b, l, x, ada3, 5, 0, layout=layout)
        x = _mix_out(acts_d, w_out_b, l, x, ada3, 5, n_ctx, layout=layout)
        ks.append(proj[:n_ctx, CB_DK * LANE:CB_DK * LANE + DIFF_W].reshape(batch, seq, DIFF_HEADS, DIFF_DV))
        vs.append(proj[:n_ctx, CB_DV * LANE:CB_DV * LANE + DIFF_W].reshape(batch, seq, DIFF_HEADS, DIFF_DV))
        ss.append(s_c)

        act = _ffn_in(x, ada3, (6, 7), norm_ffn2[l], ffn_w[1][0], l, layout=layout)
        x = _ffn_out(act, ffn_w[1][1], l, x, ada3, 8, layout=layout)

    y_prompt = _final_norm(x, final_norm, 0, n_ctx).reshape(batch, seq, d)
    y_sample = _final_norm(x, final_norm, n_ctx, n_dec).reshape(dec_batch, dec_seq, d)
    return (y_prompt, y_sample, jnp.stack(ks, axis=1), jnp.stack(vs, axis=1), jnp.stack(ss, axis=1))
```

```python
import functools
import math

import jax
import jax.numpy as jnp
from jax import lax
from jax.experimental import pallas as pl
from jax.experimental.pallas import tpu as pltpu

F32 = jnp.float32
BF16 = jnp.bfloat16

GRID_W = 64
GDN_HEADS = 4
GDN_DK = 128
GDN_DV = 128
GDN_W = GDN_HEADS * GDN_DV
CONV_K = 4
CHUNK = 64
DIFF_HEADS = 8
DIFF_DQK = 64
DIFF_DV = 2 * DIFF_DQK
DIFF_W = DIFF_HEADS * DIFF_DV
ROPE_BASE = 10000.0
POOL_WINDOWS = (2, 4, 8, 16)
POOL_GW = 128
POOL_W = len(POOL_WINDOWS) * POOL_GW
MIX_W = GDN_W + DIFF_W + POOL_W
N_MOD = 9
EPS = 1e-6

LANE = 128
GDN_BLK = 4 * CHUNK
ATTN_SUB = 256
ATTN_STEP_ROWS = 2048
PRO_SUB = 256
VMEM_LIMIT = 56 * 1024 * 1024

CB_Q, CB_K, CB_V, CB_Z = 0, 4, 8, 12
CB_DQ, CB_DK, CB_DV, CB_PIN = 16, 24, 32, 40
PROJ_MAIN = 44 * LANE


def _lambda_init(l):
    return 0.8 - 0.6 * math.exp(-0.3 * l)


def _cparams(*sem):
    return pltpu.CompilerParams(dimension_semantics=sem, vmem_limit_bytes=VMEM_LIMIT)


def _dot(a, b):
    return lax.dot_general(a, b, (((1,), (0,)), ((), ())), preferred_element_type=F32)


def _dot_nt(a, b):
    return lax.dot_general(a, b, (((1,), (1,)), ((), ())), preferred_element_type=F32)


def _dot_tn(a, b):
    return lax.dot_general(a, b, (((0,), (0,)), ((), ())), preferred_element_type=F32)


def _mm(a, b, dot=_dot):
    return dot(a.astype(BF16), b.astype(BF16))


def _dot_exact_lhs(m, b):
    b0 = b.astype(BF16)
    r = b - b0.astype(F32)
    b1 = r.astype(BF16)
    b2 = (r - b1.astype(F32)).astype(BF16)
    return _dot(m, b0) + (_dot(m, b1) + _dot(m, b2))


def _sigmoid(x):
    return 0.5 * (1.0 + jnp.tanh(0.5 * x))


def _silu(x):
    return x * _sigmoid(x)


def _ada_kernel(cond_ref, w_ref, b_ref, o_ref):
    s = _silu(cond_ref[...]).astype(BF16)
    o_ref[...] = _dot(s, w_ref[...].astype(BF16)) + b_ref[...]


def _ada(cond8, w_ada, b_ada):
    depth, d, n = w_ada.shape
    tn = 1024
    return pl.pallas_call(
        _ada_kernel,
        grid=(depth, n // tn),
        in_specs=[
            pl.BlockSpec((8, d), lambda l, j: (0, 0)),
            pl.BlockSpec((None, d, tn), lambda l, j: (l, 0, j)),
            pl.BlockSpec((None, 1, tn), lambda l, j: (l, 0, j)),
        ],
        out_specs=pl.BlockSpec((None, 8, tn), lambda l, j: (l, 0, j)),
        out_shape=jax.ShapeDtypeStruct((depth, 8, n), F32),
        compiler_params=_cparams("parallel", "parallel"),
        name="ada",
    )(cond8, w_ada, b_ada.reshape(depth, 1, n))


def _norm_mod(x, g, shift, scale):
    y = x * lax.rsqrt(jnp.mean(x * x, axis=-1, keepdims=True) + EPS) * g
    return y * (1.0 + scale) + shift


def _norm_mod_rows(x_ref, sh_ref, sc_ref, g_ref, rows):
    return _norm_mod(x_ref[rows, :], g_ref[...], sh_ref[0], sc_ref[0]).astype(BF16)


def _row_subs(n):
    return [slice(r * PRO_SUB, (r + 1) * PRO_SUB) for r in range(n // PRO_SUB)]


def _mix_in_kernel(x_ref, sh_ref, sc_ref, g_ref, w_ref, wab_ref, o_ref, ab_ref, h_ref):
    @pl.when(pl.program_id(1) == 0)
    def _():
        for rows in _row_subs(x_ref.shape[0]):
            h = _norm_mod_rows(x_ref, sh_ref, sc_ref, g_ref, rows)
            h_ref[rows, :] = h
            ab_ref[rows, :] = _dot(h, wab_ref[...])
            o_ref[rows, :] = _dot(h, w_ref[...])

    @pl.when(pl.program_id(1) != 0)
    def _():
        o_ref[...] = _dot(h_ref[...], w_ref[...])


def _ffn_in_kernel(x_ref, sh_ref, sc_ref, g_ref, wg_ref, wu_ref, o_ref, h_ref):
    def swiglu(h):
        gate = _dot(h, wg_ref[...].astype(BF16))
        up = _dot(h, wu_ref[...].astype(BF16))
        return (_silu(gate) * up).astype(o_ref.dtype)

    @pl.when(pl.program_id(1) == 0)
    def _():
        for rows in _row_subs(x_ref.shape[0]):
            h = _norm_mod_rows(x_ref, sh_ref, sc_ref, g_ref, rows)
            h_ref[rows, :] = h
            o_ref[rows, :] = swiglu(h)

    @pl.when(pl.program_id(1) != 0)
    def _():
        o_ref[...] = swiglu(h_ref[...])


def _row_tile(groups, cap):
    tm = cap
    while any(g % tm for g in groups):
        tm //= 2
    return tm


def _mod_row(n_ctx, dec_seq, tm, row0=0):
    def f(i):
        r = row0 + i * tm
        return jnp.where(r < n_ctx, 0, 1 + (r - n_ctx) // dec_seq)
    return f


def _pm_specs(d, tm, row, mod_cols):
    sh_c, sc_c = mod_cols
    return [
        pl.BlockSpec((tm, d), lambda i, j: (i, 0)),
        pl.BlockSpec((1, 1, d), lambda i, j: (row(i), 0, sh_c)),
        pl.BlockSpec((1, 1, d), lambda i, j: (row(i), 0, sc_c)),
        pl.BlockSpec((1, d), lambda i, j: (0, 0)),
    ]


def _ffn_in(x, ada3, mod_cols, g, w, l, *, layout):
    t, d = x.shape
    n_ctx, dec_seq = layout
    tm = _row_tile((n_ctx, dec_seq), 1024)
    row = _mod_row(n_ctx, dec_seq, tm)
    n = w.shape[2] // 2
    tn = 512 if n % 512 == 0 else n
    nj = n // tn
    return pl.pallas_call(
        _ffn_in_kernel,
        grid=(t // tm, nj),
        in_specs=_pm_specs(d, tm, row, mod_cols) + [
            pl.BlockSpec((None, d, tn), lambda i, j: (l, 0, j)),
            pl.BlockSpec((None, d, tn), lambda i, j: (l, 0, j + nj)),
        ],
        out_specs=pl.BlockSpec((tm, tn), lambda i, j: (i, j)),
        out_shape=jax.ShapeDtypeStruct((t, n), BF16),
        scratch_shapes=[pltpu.VMEM((tm, d), BF16)],
        compiler_params=_cparams("parallel", "arbitrary"),
        name="ffn_in",
    )(x, ada3, ada3, g.reshape(1, d), w, w)


def _mix_in(x, ada3, mod_cols, g, w_main, w_ab, l, *, layout):
    t, d = x.shape
    n_ctx, dec_seq = layout
    tm = _row_tile((n_ctx, dec_seq), 1024)
    row = _mod_row(n_ctx, dec_seq, tm)
    tn = 512
    return pl.pallas_call(
        _mix_in_kernel,
        grid=(t // tm, PROJ_MAIN // tn),
        in_specs=_pm_specs(d, tm, row, mod_cols) + [
            pl.BlockSpec((None, d, tn), lambda i, j: (l, 0, j)),
            pl.BlockSpec((None, d, LANE), lambda i, j: (l, 0, 0)),
        ],
        out_specs=[pl.BlockSpec((tm, tn), lambda i, j: (i, j)),
                   pl.BlockSpec((tm, LANE), lambda i, j: (i, 0))],
        out_shape=[jax.ShapeDtypeStruct((t, PROJ_MAIN), F32), jax.ShapeDtypeStruct((t, LANE), F32)],
        scratch_shapes=[pltpu.VMEM((tm, d), BF16)],
        compiler_params=_cparams("parallel", "arbitrary"),
        name="mix_in",
    )(x, ada3, ada3, g.reshape(1, d), w_main, w_ab)


def _ffn_out_kernel(a_ref, w_ref, x_ref, gate_ref, o_ref):
    y = _dot(a_ref[...], w_ref[...])
    o_ref[...] = x_ref[...] + (0.5 * gate_ref[0]) * y


def _ffn_out(a, w, l, x, ada3, gate_col, *, layout):
    t, k = a.shape
    d = w.shape[2]
    n_ctx, dec_seq = layout
    tm = _row_tile((n_ctx, dec_seq), 1024)
    tn = min(512, d)
    row = _mod_row(n_ctx, dec_seq, tm)
    gpb = d // tn
    return pl.pallas_call(
        _ffn_out_kernel,
        grid=(t // tm, d // tn),
        in_specs=[
            pl.BlockSpec((tm, k), lambda i, j: (i, 0)),
            pl.BlockSpec((None, k, tn), lambda i, j: (l, 0, j)),
            pl.BlockSpec((tm, tn), lambda i, j: (i, j)),
            pl.BlockSpec((1, 1, tn), lambda i, j: (row(i), 0, gate_col * gpb + j)),
        ],
        out_specs=pl.BlockSpec((tm, tn), lambda i, j: (i, j)),
        out_shape=jax.ShapeDtypeStruct((t, d), F32),
        compiler_params=_cparams("parallel", "parallel"),
        name="ffn_out",
    )(a, w, x, ada3)


def _mix_out_kernel(a1_ref, a2_ref, a3_ref, w_ref, x_ref, gate_ref, o_ref):
    y = (_dot(a1_ref[...], w_ref[0:GDN_W, :])
         + _dot(a2_ref[...], w_ref[GDN_W:GDN_W + DIFF_W, :])
         + _dot(a3_ref[...], w_ref[GDN_W + DIFF_W:MIX_W, :]))
    o_ref[...] = x_ref[...] + gate_ref[0] * y


def _mix_out(acts, w, l, x, ada3, gate_col, row0, *, layout):
    a1, a2, a3 = acts
    n = a1.shape[0]
    t, d = x.shape
    n_ctx, dec_seq = layout
    tm = _row_tile((n_ctx, dec_seq), 1024)
    tn = min(512, d)
    row = _mod_row(n_ctx, dec_seq, tm, row0)
    rb0 = row0 // tm
    gpb = d // tn
    return pl.pallas_call(
        _mix_out_kernel,
        grid=(n // tm, d // tn),
        in_specs=[
            pl.BlockSpec((tm, GDN_W), lambda i, j: (i, 0)),
            pl.BlockSpec((tm, DIFF_W), lambda i, j: (i, 0)),
            pl.BlockSpec((tm, POOL_W), lambda i, j: (i, 0)),
            pl.BlockSpec((None, MIX_W, tn), lambda i, j: (l, 0, j)),
            pl.BlockSpec((tm, tn), lambda i, j: (rb0 + i, j)),
            pl.BlockSpec((1, 1, tn), lambda i, j: (row(i), 0, gate_col * gpb + j)),
        ],
        out_specs=pl.BlockSpec((tm, tn), lambda i, j: (rb0 + i, j)),
        out_shape=jax.ShapeDtypeStruct((t, d), F32),
        input_output_aliases={4: 0},
        compiler_params=_cparams("parallel", "parallel"),
        name="mix_out",
    )(a1, a2, a3, w, x, ada3)


def _final_norm_kernel(x_ref, g_ref, o_ref):
    x = x_ref[...]
    o_ref[...] = x * lax.rsqrt(jnp.mean(x * x, axis=-1, keepdims=True) + EPS) * g_ref[...]


def _final_norm(x, g, row0, nrows):
    d = x.shape[1]
    tm = 256
    rb0 = row0 // tm
    return pl.pallas_call(
        _final_norm_kernel,
        grid=(nrows // tm,),
        in_specs=[pl.BlockSpec((tm, d), lambda i: (rb0 + i, 0)), pl.BlockSpec((1, d), lambda i: (0, 0))],
        out_specs=pl.BlockSpec((tm, d), lambda i: (i, 0)),
        out_shape=jax.ShapeDtypeStruct((nrows, d), F32),
        compiler_params=_cparams("parallel"),
        name="final_norm",
    )(x, g.reshape(1, d))


def _shift_rows(x, off, row):
    if off == 0:
        return x
    n = x.shape[0]
    y = pltpu.roll(x, (-off) % n, 0)
    ok = (row + off >= 0) & (row + off < n)
    return jnp.where(ok, y, 0.0)


def _gdn_prep_kernel(x_ref, w_ref, o_ref):
    x = x_ref[...]
    n = x.shape[0]
    left = CONV_K // 2
    edge = 16

    def conv(xb, shift):
        acc = jnp.zeros_like(xb)
        for j in range(CONV_K):
            acc = acc + shift(xb, j - left) * w_ref[j:j + 1, :]
        return acc

    acc = conv(x, lambda xb, off: xb if off == 0 else pltpu.roll(xb, (-off) % n, 0))
    row = lax.broadcasted_iota(jnp.int32, (edge, LANE), 0)
    head = conv(x[0:edge], lambda xb, off: _shift_rows(xb, off, row))
    tail = conv(x[n - edge:n], lambda xb, off: _shift_rows(xb, off, row))
    acc = jnp.concatenate([head[0:8], acc[8:n - 8], tail[edge - 8:edge]], axis=0)
    y = _silu(acc)
    kind = pl.program_id(1) // GDN_HEADS
    inv = lax.rsqrt(jnp.sum(y * y, axis=-1, keepdims=True) + EPS)
    inv = inv * jnp.where(kind == 0, GDN_DK ** -0.5, 1.0)
    o_ref[...] = jnp.where(kind < 2, y * inv, y)


def _gdn_prep(proj, conv_w, row0, nseq, seqlen):
    blk0 = row0 // seqlen
    return pl.pallas_call(
        _gdn_prep_kernel,
        grid=(nseq, 3 * GDN_HEADS),
        in_specs=[
            pl.BlockSpec((seqlen, LANE), lambda s, c: (blk0 + s, CB_Q + c)),
            pl.BlockSpec((CONV_K, LANE), lambda s, c: (0, c)),
        ],
        out_specs=pl.BlockSpec((seqlen, LANE), lambda s, c: (s, c)),
        out_shape=jax.ShapeDtypeStruct((nseq * seqlen, 3 * GDN_W), F32),
        compiler_params=_cparams("parallel", "parallel"),
        name="gdn_prep",
    )(proj, conv_w)


def _unit_tri_solve_many(lmats, rhss, eye):
    xs = [eye - l for l in lmats]
    ps = [_mm(l, l) for l in lmats]
    steps = int(math.log2(CHUNK)) - 1
    for s in range(steps):
        xs = [x + _mm(x, p) for x, p in zip(xs, ps)]
        if s + 1 < steps:
            ps = [_mm(p, p) for p in ps]
    x0s = [_mm(x, r) for x, r in zip(xs, rhss)]
    resids = [r - (x0 + _mm(l, x0)) for l, r, x0 in zip(lmats, rhss, x0s)]
    return [x0 + _mm(x, rs) for x, x0, rs in zip(xs, x0s, resids)]


def _gdn_intra_kernel(q_ref, k_ref, v_ref, ab_ref, alog_ref, dt_ref,
                      u_ref, w_ref, qg_ref, kd_ref, a_ref, gl_ref):
    n = GDN_BLK
    ab = ab_ref[...]
    lane = lax.broadcasted_iota(jnp.int32, ab.shape, 1)
    sp = jnp.maximum(ab + dt_ref[...], 0.0) + jnp.log(1.0 + jnp.exp(-jnp.abs(ab + dt_ref[...])))
    gmat = -jnp.exp(alog_ref[...]) * sp
    bmat = _sigmoid(ab)

    ri = lax.broadcasted_iota(jnp.int32, (n, n), 0)
    ci = lax.broadcasted_iota(jnp.int32, (n, n), 1)
    same = (ri // CHUNK) == (ci // CHUNK)
    eye = (ri == ci).astype(F32)
    r8 = lax.broadcasted_iota(jnp.int32, (8, LANE), 0)
    gtot_all = _dot_exact_lhs(same.astype(BF16), gmat)
    gcf_all = _dot_exact_lhs((same & (ci <= ri)).astype(BF16), gmat)
    gc_dirs = (gcf_all, (gtot_all - gcf_all) + gmat)
    incl_dirs = (same & (ci <= ri), same & (ci >= ri))
    strict_dirs = (same & (ci < ri), same & (ci > ri))

    lmats, rhss, dests = [], [], []
    for h in range(GDN_HEADS):
        cols = slice(h * LANE, (h + 1) * LANE)
        q = q_ref[:, cols]
        k = k_ref[:, cols]
        v = v_ref[:, cols]
        kk = _mm(k, k, _dot_nt)
        qk = _mm(q, k, _dot_nt)
        for d in range(2):
            incl = incl_dirs[d]
            sel_g = lane == d * GDN_HEADS + h
            sel_b = lane == 2 * GDN_HEADS + d * GDN_HEADS + h
            gc = jnp.sum(jnp.where(sel_g, gc_dirs[d], 0.0), axis=1, keepdims=True)
            gtot = jnp.sum(jnp.where(sel_g, gtot_all, 0.0), axis=1, keepdims=True)
            beta = jnp.sum(jnp.where(sel_b, bmat, 0.0), axis=1, keepdims=True)
            gc_b = jnp.broadcast_to(gc, (n, LANE))
            gc_row = jnp.transpose(gc_b)[0:1, :]
            diff = jnp.where(incl, gc - gc_row, 0.0)
            decay = jnp.where(incl, jnp.exp(diff), 0.0)
            eg = jnp.exp(gc_b)
            lmats.append(jnp.where(strict_dirs[d], kk * beta * decay, 0.0))
            rhss.append(jnp.concatenate([v * beta, (k * beta) * eg], axis=1))
            dests.append((d, cols))
            qg_ref[d, :, cols] = q * eg
            kd_ref[d, :, cols] = k * jnp.exp(jnp.broadcast_to(gtot - gc, (n, LANE)))
            amat = qk * decay
            a_ref[d, h] = (amat[:, 0:CHUNK] + amat[:, CHUNK:2 * CHUNK]) + (
                amat[:, 2 * CHUNK:3 * CHUNK] + amat[:, 3 * CHUNK:4 * CHUNK])
            egt = jnp.exp(jnp.broadcast_to(gtot, (n, LANE)))
            gl8 = jnp.zeros((8, LANE), F32)
            for c in range(n // CHUNK):
                gl8 = jnp.where(r8 == c, egt[c * CHUNK:c * CHUNK + 8, :], gl8)
            gl_ref[d, :, cols] = gl8

    for (d, cols), uw in zip(dests, _unit_tri_solve_many(lmats, rhss, eye)):
        u_ref[d, :, cols] = uw[:, :GDN_DV]
        w_ref[d, :, cols] = uw[:, GDN_DV:]


def _gdn_intra(qkv, proj_ab, alog_row, dt_row, row0):
    t = qkv.shape[0]
    nb = t // GDN_BLK
    pb0 = row0 // GDN_BLK
    big = jax.ShapeDtypeStruct((2, t, GDN_W), F32)
    big_spec = pl.BlockSpec((2, GDN_BLK, GDN_W), lambda i: (0, i, 0))
    return pl.pallas_call(
        _gdn_intra_kernel,
        grid=(nb,),
        in_specs=[
            pl.BlockSpec((GDN_BLK, GDN_W), lambda i: (i, 0)),
            pl.BlockSpec((GDN_BLK, GDN_W), lambda i: (i, 1)),
            pl.BlockSpec((GDN_BLK, GDN_W), lambda i: (i, 2)),
            pl.BlockSpec((GDN_BLK, LANE), lambda i: (pb0 + i, 0)),
            pl.BlockSpec((1, LANE), lambda i: (0, 0)),
            pl.BlockSpec((1, LANE), lambda i: (0, 0)),
        ],
        out_specs=[big_spec, big_spec, big_spec, big_spec,
                   pl.BlockSpec((2, GDN_HEADS, GDN_BLK, CHUNK), lambda i: (0, 0, i, 0)),
                   pl.BlockSpec((2, 8, GDN_W), lambda i: (0, i, 0))],
        out_shape=[big, big, big, big,
                   jax.ShapeDtypeStruct((2, GDN_HEADS, t, CHUNK), F32),
                   jax.ShapeDtypeStruct((2, nb * 8, GDN_W), F32)],
        compiler_params=_cparams("parallel"),
        name="gdn_intra",
    )(qkv, qkv, qkv, proj_ab, alog_row, dt_row)


def _gdn_scan_kernel(*refs, has_init):
    if has_init:
        s0_ref, refs = refs[0], refs[1:]
    ins = (refs[0:6], refs[6:12])
    o_refs, sf_ref, s_ref = refs[12:14], refs[14], refs[15]
    blk = pl.program_id(1)
    nchunk = GDN_BLK // CHUNK

    @pl.when(blk == 0)
    def _():
        if has_init:
            s_ref[...] = s0_ref[...]
        else:
            s_ref[...] = jnp.zeros_like(s_ref)

    U, W, QG, KD, A, GL = range(6)
    chains = [(d, h) for d in range(2) for h in range(GDN_HEADS)]
    states = [s_ref[d, h] for d, h in chains]
    for c in range(nchunk):
        ccs = [c if d == 0 else nchunk - 1 - c for d, _ in chains]
        rows = [slice(cc * CHUNK, (cc + 1) * CHUNK) for cc in ccs]
        cols = [slice(h * LANE, (h + 1) * LANE) for _, h in chains]
        n = range(len(chains))
        ws = [_mm(ins[chains[i][0]][W][rows[i], cols[i]], states[i]) for i in n]
        qs = [_mm(ins[chains[i][0]][QG][rows[i], cols[i]], states[i]) for i in n]
        v_news = [ins[chains[i][0]][U][rows[i], cols[i]] - ws[i] for i in n]
        for i in n:
            d, h = chains[i]
            o_refs[d][rows[i], cols[i]] = qs[i] + _mm(ins[d][A][h, rows[i], :], v_news[i])
        states = [states[i] * ins[chains[i][0]][GL][ccs[i]:ccs[i] + 1, cols[i]]
                  + _mm(ins[chains[i][0]][KD][rows[i], cols[i]], v_news[i], _dot_tn) for i in n]
    for (d, h), s in zip(chains, states):
        s_ref[d, h] = s

    @pl.when(blk == pl.num_programs(1) - 1)
    def _():
        sf_ref[...] = s_ref[...]


def _gdn_scan(s0, u, w, qg, kd, a, gl, nseq, seqlen):
    t = u.shape[1]
    nblk = seqlen // GDN_BLK
    in_specs, args = [], []
    for d in range(2):
        rb = (lambda s, b: s * nblk + b) if d == 0 else (lambda s, b: s * nblk + nblk - 1 - b)
        big = pl.BlockSpec((None, GDN_BLK, GDN_W), lambda s, b, d=d, rb=rb: (d, rb(s, b), 0))
        in_specs += [big, big, big, big,
                     pl.BlockSpec((None, GDN_HEADS, GDN_BLK, CHUNK), lambda s, b, d=d, rb=rb: (d, 0, rb(s, b), 0)),
                     pl.BlockSpec((None, 8, GDN_W), lambda s, b, d=d, rb=rb: (d, rb(s, b), 0))]
        args += [u, w, qg, kd, a, gl]
    state_spec = pl.BlockSpec((None, 2, GDN_HEADS, GDN_DK, GDN_DV), lambda s, b: (s, 0, 0, 0, 0))
    if s0 is not None:
        in_specs.insert(0, state_spec)
        args.insert(0, s0)
    out_specs = [pl.BlockSpec((GDN_BLK, GDN_W), lambda s, b: (s * nblk + b, 0)),
                 pl.BlockSpec((GDN_BLK, GDN_W), lambda s, b: (s * nblk + nblk - 1 - b, 0)),
                 state_spec]
    return pl.pallas_call(
        functools.partial(_gdn_scan_kernel, has_init=s0 is not None),
        grid=(nseq, nblk),
        in_specs=in_specs,
        out_specs=out_specs,
        out_shape=[jax.ShapeDtypeStruct((t, GDN_W), F32), jax.ShapeDtypeStruct((t, GDN_W), F32),
                   jax.ShapeDtypeStruct((nseq, 2, GDN_HEADS, GDN_DK, GDN_DV), F32)],
        scratch_shapes=[pltpu.VMEM((2, GDN_HEADS, GDN_DK, GDN_DV), F32)],
        compiler_params=_cparams("parallel", "arbitrary"),
        name="gdn_scan",
    )(*args)


def _gdn_post_kernel(of_ref, ob_ref, z_ref, g_ref, out_ref):
    for h in range(GDN_HEADS):
        cols = slice(h * LANE, (h + 1) * LANE)
        o = of_ref[:, cols] + ob_ref[:, cols]
        y = o * lax.rsqrt(jnp.mean(o * o, axis=-1, keepdims=True) + EPS) * g_ref[...]
        out_ref[:, cols] = (y * _silu(z_ref[:, cols])).astype(out_ref.dtype)


def _gdn_post(o_f, o_b, proj, g, row0):
    t = o_f.shape[0]
    tm = 256
    pb0 = row0 // tm
    return pl.pallas_call(
        _gdn_post_kernel,
        grid=(t // tm,),
        in_specs=[
            pl.BlockSpec((tm, GDN_W), lambda i: (i, 0)),
            pl.BlockSpec((tm, GDN_W), lambda i: (i, 0)),
            pl.BlockSpec((tm, GDN_W), lambda i: (pb0 + i, CB_Z // GDN_HEADS)),
            pl.BlockSpec((1, GDN_DV), lambda i: (0, 0)),
        ],
        out_specs=pl.BlockSpec((tm, GDN_W), lambda i: (i, 0)),
        out_shape=jax.ShapeDtypeStruct((t, GDN_W), BF16),
        compiler_params=_cparams("parallel"),
        name="gdn_post",
    )(o_f, o_b, proj, g.reshape(1, GDN_DV))


def _rope_rotate(x, cos, sin_signed):
    lane = lax.broadcasted_iota(jnp.int32, x.shape, 1)
    first = (lane % 32) < 16
    partner = jnp.where(first, pltpu.roll(x, LANE - 16, 1), pltpu.roll(x, 16, 1))
    return x * cos + partner * sin_signed


def _rope_tables(seqlen):
    pos = jnp.arange(seqlen)
    rowp = (pos // GRID_W).astype(F32)
    colp = (pos % GRID_W).astype(F32)
    half = DIFF_DQK // 2
    inv_freq = ROPE_BASE ** (-jnp.arange(0, half, 2, dtype=F32) / half)
    c = jnp.arange(LANE)
    c64 = c % DIFF_DQK
    p = jnp.where((c64 // half)[None, :] == 0, rowp[:, None], colp[:, None])
    ang = p * inv_freq[c64 % (half // 2)][None, :]
    sign = jnp.where((c64 % half) < half // 2, -1.0, 1.0).astype(F32)
    return jnp.cos(ang), jnp.sin(ang) * sign[None, :]


def _diff_attn_kernel(*refs, seg_lens, lam_init, rope, hp, n_alias, emit_cache):
    nseg = len(seg_lens)
    q_ref = refs[0]
    k_refs = refs[1:1 + nseg]
    v_refs = refs[1 + nseg:1 + 2 * nseg]
    pos = 1 + 2 * nseg
    if rope:
        cosq_ref, sinq_ref, cosk_ref, sink_ref = refs[pos:pos + 4]
        pos += 4
    lam_ref, g_ref = refs[pos:pos + 2]
    pos += 2 + n_alias
    o_ref = refs[pos]
    if emit_cache:
        kc_ref, vc_ref = refs[pos + 1:pos + 3]
    kb_ref, vb_ref = refs[-2:]
    offs = [sum(seg_lens[:s]) for s in range(nseg)]
    hcols = [slice(hh * DIFF_DV, (hh + 1) * DIFF_DV) for hh in range(hp)]

    @pl.when(pl.program_id(2) == 0)
    def _():
        if emit_cache:
            kc_ref[...] = k_refs[-1][...]
            vc_ref[...] = v_refs[-1][...]
        for hh in range(hp):
            vb_ref[:, (2 * hh + 1) * DIFF_DV:(2 * hh + 2) * DIFF_DV] = jnp.ones((vb_ref.shape[0], DIFF_DV), BF16)
            for s in range(nseg):
                rows = slice(offs[s], offs[s] + seg_lens[s])
                k = k_refs[s][:, hcols[hh]]
                if rope and s == nseg - 1:
                    k = _rope_rotate(k, cosk_ref[...], sink_ref[...])
                kb_ref[rows, hcols[hh]] = k.astype(BF16)
                vb_ref[rows, 2 * hh * DIFF_DV:(2 * hh + 1) * DIFF_DV] = v_refs[s][:, hcols[hh]].astype(BF16)

    lv = lam_ref[...]
    lam = (jnp.exp(jnp.sum(lv[0:1] * lv[1:2], axis=1, keepdims=True))
           - jnp.exp(jnp.sum(lv[2:3] * lv[3:4], axis=1, keepdims=True)) + lam_init)
    lane = lax.broadcasted_iota(jnp.int32, (ATTN_SUB, DIFF_DV), 1)

    def scores(hh, rows):
        q = q_ref[rows, hcols[hh]]
        if rope:
            q = _rope_rotate(q, cosq_ref[rows, :], sinq_ref[rows, :])
        q = q * (DIFF_DQK ** -0.5)
        kb = kb_ref[:, hcols[hh]]
        return [_dot_nt(jnp.where((lane // DIFF_DQK) == m, q, 0.0).astype(BF16), kb) for m in range(2)]

    units = [(hh, slice(r * ATTN_SUB, (r + 1) * ATTN_SUB))
             for hh in range(hp) for r in range(q_ref.shape[0] // ATTN_SUB)]
    ss_next = scores(*units[0])
    for n, (hh, rows) in enumerate(units):
        ss = ss_next
        if n + 1 < len(units):
            ss_next = scores(*units[n + 1])
        vb = vb_ref[:, 2 * hh * DIFF_DV:(2 * hh + 2) * DIFF_DV]
        es = [jnp.exp(s - jnp.max(s, axis=1, keepdims=True)).astype(BF16) for s in ss]
        accs = [_dot(e, vb) for e in es]
        outs = [acc[:, 0:DIFF_DV] / acc[:, DIFF_DV:] for acc in accs]
        o = outs[0] - lam * outs[1]
        y = o * lax.rsqrt(jnp.mean(o * o, axis=-1, keepdims=True) + EPS) * g_ref[...]
        o_ref[rows, hcols[hh]] = (y * (1.0 - lam_init)).astype(o_ref.dtype)


def _diff_attn(proj, row0, nseq, seqlen, past_kv, l, lam_vec, g, cache_out):
    tq = min(4 * ATTN_SUB, seqlen)
    hp = min(DIFF_HEADS, max(1, ATTN_STEP_ROWS // seqlen))
    nqb = seqlen // tq
    qb0 = row0 // tq
    sb0 = row0 // seqlen
    wcols = hp * DIFF_DV
    q_spec = pl.BlockSpec((tq, wcols), lambda b, h, i: (qb0 + b * nqb + i, CB_DQ // hp + h))
    k_new = pl.BlockSpec((seqlen, wcols), lambda b, h, i: (sb0 + b, CB_DK // hp + h))
    v_new = pl.BlockSpec((seqlen, wcols), lambda b, h, i: (sb0 + b, CB_DV // hp + h))
    arrs, specs, seg_lens = [proj], [q_spec], [seqlen]
    if past_kv is None:
        arrs += [proj, proj]
        specs += [k_new, v_new]
    else:
        past = past_kv[0].shape[2]
        cached = pl.BlockSpec((None, None, past, wcols), lambda b, h, i: (b, l, 0, h))
        arrs += [past_kv[0], proj, past_kv[1], proj]
        specs += [cached, k_new, cached, v_new]
        seg_lens = [past, seqlen]
        cos, sin = _rope_tables(seqlen)
        arrs += [cos, sin, cos, sin]
        specs += [pl.BlockSpec((tq, DIFF_DV), lambda b, h, i: (i, 0))] * 2 + [
            pl.BlockSpec((seqlen, DIFF_DV), lambda b, h, i: (0, 0))] * 2
    arrs += [lam_vec, g.reshape(1, DIFF_DV)]
    specs += [pl.BlockSpec((4, DIFF_DQK), lambda b, h, i: (0, 0)),
              pl.BlockSpec((1, DIFF_DV), lambda b, h, i: (0, 0))]
    out_specs = [pl.BlockSpec((tq, wcols), lambda b, h, i: (b * nqb + i, h))]
    out_shape = [jax.ShapeDtypeStruct((nseq * seqlen, DIFF_W), BF16)]
    aliases, n_alias = {}, 0
    if cache_out is not None:
        depth, prev = cache_out
        entry = pl.BlockSpec((None, None, seqlen, wcols), lambda b, h, i: (b, l, 0, h))
        out_specs += [entry, entry]
        out_shape += [jax.ShapeDtypeStruct((nseq, depth, seqlen, DIFF_W), F32)] * 2
        if prev is not None:
            aliases = {len(arrs): 1, len(arrs) + 1: 2}
            n_alias = 2
            arrs += list(prev)
            specs += [pl.BlockSpec(memory_space=pl.ANY)] * 2
    lk = sum(seg_lens)
    outs = pl.pallas_call(
        functools.partial(_diff_attn_kernel, seg_lens=tuple(seg_lens), lam_init=_lambda_init(l),
                          rope=past_kv is not None, hp=hp, n_alias=n_alias, emit_cache=cache_out is not None),
        grid=(nseq, DIFF_HEADS // hp, nqb),
        in_specs=specs,
        out_specs=out_specs,
        out_shape=out_shape,
        input_output_aliases=aliases,
        scratch_shapes=[pltpu.VMEM((lk, wcols), BF16), pltpu.VMEM((lk, 2 * wcols), BF16)],
        compiler_params=_cparams("parallel", "parallel", "arbitrary"),
        name="diff_attn",
    )(*arrs)
    return outs[0], (tuple(outs[1:]) if cache_out is not None else None)


def _pool_kernel(x_ref, w_ref, sc_ref, o_ref):
    n = x_ref.shape[0]
    row = lax.broadcasted_iota(jnp.int32, (n, POOL_GW), 0)
    for gi, win in enumerate(POOL_WINDOWS):
        cols = slice(gi * POOL_GW, (gi + 1) * POOL_GW)
        x = x_ref[:, cols]
        a = win // 2
        b = win - a - 1
        acc = x
        for off in range(-a, b + 1):
            if off != 0:
                acc = acc + _shift_rows(x, off, row)
        cnt = (jnp.minimum(row + b + 1, n) - jnp.maximum(row - a, 0)).astype(F32)
        pooled = acc / cnt - x
        y = _dot(pooled.astype(BF16), w_ref[gi].astype(BF16))
        o_ref[:, cols] = (y * sc_ref[:, cols]).astype(o_ref.dtype)


def _pool(proj, pool_w, pool_scale, row0, nseq, seqlen):
    blk0 = row0 // seqlen
    return pl.pallas_call(
        _pool_kernel,
        grid=(nseq,),
        in_specs=[
            pl.BlockSpec((seqlen, POOL_W), lambda s: (blk0 + s, CB_PIN // len(POOL_WINDOWS))),
            pl.BlockSpec((len(POOL_WINDOWS), POOL_GW, POOL_GW), lambda s: (0, 0, 0)),
            pl.BlockSpec((1, POOL_W), lambda s: (0, 0)),
        ],
        out_specs=pl.BlockSpec((seqlen, POOL_W), lambda s: (s, 0)),
        out_shape=jax.ShapeDtypeStruct((nseq * seqlen, POOL_W), BF16),
        compiler_params=_cparams("parallel"),
        name="pool",
    )(proj, pool_w, pool_scale.reshape(1, POOL_W))


def _mix_group(proj, proj_ab, lp, l, row0, nseq, seqlen, ctx, cache_out):
    qkv = _gdn_prep(proj, lp["gdn_conv"], row0, nseq, seqlen)
    u, w, qg, kd, a, gl = _gdn_intra(qkv, proj_ab, lp["alog_row"], lp["dt_row"], row0)
    o_f, o_b, s_fin = _gdn_scan(None if ctx is None else ctx[2], u, w, qg, kd, a, gl, nseq, seqlen)
    o_gdn = _gdn_post(o_f, o_b, proj, lp["gdn_norm"], row0)

    o_diff, caches = _diff_attn(proj, row0, nseq, seqlen, None if ctx is None else ctx[:2], l,
                                lp["diff_lam"], lp["diff_norm"], cache_out)
    o_pool = _pool(proj, lp["pool_w"], lp["pool_scale"], row0, nseq, seqlen)
    return (o_gdn, o_diff, o_pool), s_fin, caches


def _split_w_in(w_in):
    depth, d, _ = w_in.shape
    o_ab = 4 * GDN_W
    o_dq = o_ab + 4 * GDN_HEADS
    pad = jnp.zeros((depth, d, LANE - 4 * GDN_HEADS), w_in.dtype)
    main = jnp.concatenate([w_in[..., :o_ab], w_in[..., o_dq:]], axis=-1)
    ab = jnp.concatenate([w_in[..., o_ab:o_dq], pad], axis=-1)
    return main.astype(BF16), ab.astype(BF16)


def kernel(x_prompt, x_sample, c, cache_k, cache_v, state_gdn, c_ctx, w_ada, b_ada, norm_ffn1, ffn1_in,
           ffn1_out, norm_mix, w_in, gdn_conv, gdn_a_log, gdn_dt_bias, gdn_norm, diff_lam, diff_norm,
           pool_w, pool_scale, w_out, norm_ffn2, ffn2_in, ffn2_out, final_norm):
    batch, seq, d = x_prompt.shape
    dec_batch, dec_seq, _ = x_sample.shape
    depth = w_ada.shape[0]
    n_ctx = batch * seq
    n_dec = dec_batch * dec_seq
    layout = (n_ctx, dec_seq)
    past = cache_k.shape[2]

    x = jnp.concatenate([x_prompt.reshape(n_ctx, d), x_sample.reshape(n_dec, d)], axis=0)
    cond8 = jnp.concatenate([c_ctx[None, :], c, jnp.zeros((8 - 1 - dec_batch, d), F32)], axis=0)
    ada = _ada(cond8, w_ada, b_ada)
    cache_k4 = cache_k.reshape(dec_batch, depth, past, DIFF_W)
    cache_v4 = cache_v.reshape(dec_batch, depth, past, DIFF_W)
    w_in_main, w_in_ab = _split_w_in(w_in)
    w_out_b = w_out.astype(BF16)
    ffn_w = [(ffn1_in, ffn1_out.astype(BF16)), (ffn2_in, ffn2_out.astype(BF16))]

    caches, ss = None, []
    for l in range(depth):
        ada3 = ada[l, :1 + dec_batch].reshape(1 + dec_batch, 1, N_MOD * d)
        lane_pad = jnp.zeros((LANE - 2 * GDN_HEADS,), F32)
        lp = dict(
            gdn_conv=gdn_conv[l], gdn_norm=gdn_norm[l], diff_lam=diff_lam[l], diff_norm=diff_norm[l],
            pool_w=pool_w[l], pool_scale=pool_scale[l],
            alog_row=jnp.concatenate([gdn_a_log[l].reshape(-1), lane_pad]).reshape(1, LANE),
            dt_row=jnp.concatenate([gdn_dt_bias[l].reshape(-1), lane_pad]).reshape(1, LANE),
        )
        act = _ffn_in(x, ada3, (0, 1), norm_ffn1[l], ffn_w[0][0], l, layout=layout)
        x = _ffn_out(act, ffn_w[0][1], l, x, ada3, 2, layout=layout)

        proj, proj_ab = _mix_in(x, ada3, (3, 4), norm_mix[l], w_in_main, w_in_ab, l, layout=layout)
        acts_c, s_c, caches = _mix_group(proj, proj_ab, lp, l, 0, batch, seq, None, (depth, caches))
        acts_d, _, _ = _mix_group(proj, proj_ab, lp, l, n_ctx, dec_batch, dec_seq,
                                  (cache_k4, cache_v4, state_gdn[:, l]), None)
        x = _mix_out(acts_c, w_out_b, l, x, ada3, 5, 0, layout=layout)
        x = _mix_out(acts_d, w_out_b, l, x, ada3, 5, n_ctx, layout=layout)
        ss.append(s_c)

        act = _ffn_in(x, ada3, (6, 7), norm_ffn2[l], ffn_w[1][0], l, layout=layout)
        x = _ffn_out(act, ffn_w[1][1], l, x, ada3, 8, layout=layout)

    y_prompt = _final_norm(x, final_norm, 0, n_ctx).reshape(batch, seq, d)
    y_sample = _final_norm(x, final_norm, n_ctx, n_dec).reshape(dec_batch, dec_seq, d)
    new_k, new_v = (t.reshape(batch, depth, seq, DIFF_HEADS, DIFF_DV) for t in caches)
    return (y_prompt, y_sample, new_k, new_v, jnp.stack(ss, axis=1))
```

```python
import functools
import math

import jax
import jax.numpy as jnp
from jax import lax
from jax.experimental import pallas as pl
from jax.experimental.pallas import tpu as pltpu

F32 = jnp.float32
BF16 = jnp.bfloat16

GRID_W = 64
GDN_HEADS = 4
GDN_DK = 128
GDN_DV = 128
GDN_W = GDN_HEADS * GDN_DV
CONV_K = 4
CHUNK = 64
DIFF_HEADS = 8
DIFF_DQK = 64
DIFF_DV = 2 * DIFF_DQK
DIFF_W = DIFF_HEADS * DIFF_DV
ROPE_BASE = 10000.0
POOL_WINDOWS = (2, 4, 8, 16)
POOL_GW = 128
POOL_W = len(POOL_WINDOWS) * POOL_GW
MIX_W = GDN_W + DIFF_W + POOL_W
N_MOD = 9
EPS = 1e-6

LANE = 128
GDN_BLK = 4 * CHUNK
ATTN_SUB = 256
ATTN_STEP_ROWS = 2048
PRO_SUB = 256
VMEM_LIMIT = 60 * 1024 * 1024

CB_Q, CB_K, CB_V, CB_Z = 0, 4, 8, 12
CB_DQ, CB_DK, CB_DV, CB_PIN = 16, 24, 32, 40
PROJ_MAIN = 44 * LANE


def _lambda_init(l):
    return 0.8 - 0.6 * math.exp(-0.3 * l)


def _cparams(*sem):
    return pltpu.CompilerParams(dimension_semantics=sem, vmem_limit_bytes=VMEM_LIMIT)


def _dot(a, b):
    return lax.dot_general(a, b, (((1,), (0,)), ((), ())), preferred_element_type=F32)


def _dot_nt(a, b):
    return lax.dot_general(a, b, (((1,), (1,)), ((), ())), preferred_element_type=F32)


def _dot_tn(a, b):
    return lax.dot_general(a, b, (((0,), (0,)), ((), ())), preferred_element_type=F32)


def _mm(a, b, dot=_dot):
    return dot(a.astype(BF16), b.astype(BF16))


def _dot_exact_lhs(m, b):
    b0 = b.astype(BF16)
    r = b - b0.astype(F32)
    b1 = r.astype(BF16)
    b2 = (r - b1.astype(F32)).astype(BF16)
    return _dot(m, b0) + (_dot(m, b1) + _dot(m, b2))


def _sigmoid(x):
    return 0.5 * (1.0 + jnp.tanh(0.5 * x))


def _silu(x):
    return x * _sigmoid(x)


def _ada_kernel(cond_ref, w_ref, b_ref, o_ref):
    s = _silu(cond_ref[...]).astype(BF16)
    o_ref[...] = _dot(s, w_ref[...].astype(BF16)) + b_ref[...]


def _ada(cond8, w_ada, b_ada):
    depth, d, n = w_ada.shape
    tn = 1024
    return pl.pallas_call(
        _ada_kernel,
        grid=(depth, n // tn),
        in_specs=[
            pl.BlockSpec((8, d), lambda l, j: (0, 0)),
            pl.BlockSpec((None, d, tn), lambda l, j: (l, 0, j)),
            pl.BlockSpec((None, 1, tn), lambda l, j: (l, 0, j)),
        ],
        out_specs=pl.BlockSpec((None, 8, tn), lambda l, j: (l, 0, j)),
        out_shape=jax.ShapeDtypeStruct((depth, 8, n), F32),
        compiler_params=_cparams("parallel", "parallel"),
        name="ada",
    )(cond8, w_ada, b_ada.reshape(depth, 1, n))


def _norm_mod(x, g, shift, scale):
    y = x * lax.rsqrt(jnp.mean(x * x, axis=-1, keepdims=True) + EPS) * g
    return y * (1.0 + scale) + shift


def _read_x(x_refs, n_first, idx):
    if len(x_refs) == 1:
        return x_refs[0][idx]
    return jnp.where(pl.program_id(0) < n_first, x_refs[0][idx], x_refs[1][idx])


def _norm_mod_rows(x_refs, n_first, sh_ref, sc_ref, g_ref, rows):
    x = _read_x(x_refs, n_first, (rows, slice(None)))
    return _norm_mod(x, g_ref[...], sh_ref[0], sc_ref[0]).astype(BF16)


def _row_subs(n):
    return [slice(r * PRO_SUB, (r + 1) * PRO_SUB) for r in range(n // PRO_SUB)]


def _mix_in_kernel(x_ref, sh_ref, sc_ref, g_ref, w_ref, wab_ref, o_ref, ab_ref, h_ref):
    @pl.when(pl.program_id(1) == 0)
    def _():
        for rows in _row_subs(x_ref.shape[0]):
            h = _norm_mod_rows((x_ref,), None, sh_ref, sc_ref, g_ref, rows)
            h_ref[rows, :] = h
            ab_ref[rows, :] = _dot(h, wab_ref[...])
            o_ref[rows, :] = _dot(h, w_ref[...])

    @pl.when(pl.program_id(1) != 0)
    def _():
        o_ref[...] = _dot(h_ref[...], w_ref[...])


def _ffn_in_kernel(*refs, n_first):
    x_refs = refs[:-7]
    sh_ref, sc_ref, g_ref, wg_ref, wu_ref, o_ref, h_ref = refs[-7:]

    def swiglu(h):
        gate = _dot(h, wg_ref[...].astype(BF16))
        up = _dot(h, wu_ref[...].astype(BF16))
        return (_silu(gate) * up).astype(o_ref.dtype)

    @pl.when(pl.program_id(1) == 0)
    def _():
        for rows in _row_subs(h_ref.shape[0]):
            h = _norm_mod_rows(x_refs, n_first, sh_ref, sc_ref, g_ref, rows)
            h_ref[rows, :] = h
            o_ref[rows, :] = swiglu(h)

    @pl.when(pl.program_id(1) != 0)
    def _():
        o_ref[...] = swiglu(h_ref[...])


def _row_tile(groups, cap):
    tm = cap
    while any(g % tm for g in groups):
        tm //= 2
    return tm


def _mod_row(n_ctx, dec_seq, tm, row0=0):
    def f(i):
        r = row0 + i * tm
        return jnp.where(r < n_ctx, 0, 1 + (r - n_ctx) // dec_seq)
    return f


def _x_parts(x, tm):
    if isinstance(x, tuple):
        return list(x), x[0].shape[0] // tm, sum(p.shape[0] for p in x)
    return [x], None, x.shape[0]


def _x_specs(n_first, tm, tn):
    if n_first is None:
        return [pl.BlockSpec((tm, tn), lambda i, j: (i, j))]
    return [pl.BlockSpec((tm, tn), lambda i, j: (jnp.where(i < n_first, i, n_first - 1), jnp.where(i < n_first, j, 0))),
            pl.BlockSpec((tm, tn), lambda i, j: (jnp.where(i < n_first, 0, i - n_first), jnp.where(i < n_first, 0, j)))]


def _pm_specs(d, tm, row, mod_cols, n_first=None):
    sh_c, sc_c = mod_cols
    x_specs = [pl.BlockSpec((tm, d), lambda i, j: (i, 0))] if n_first is None else [
        pl.BlockSpec((tm, d), lambda i, j: (jnp.minimum(i, n_first - 1), 0)),
        pl.BlockSpec((tm, d), lambda i, j: (jnp.maximum(i - n_first, 0), 0))]
    return x_specs + [
        pl.BlockSpec((1, 1, d), lambda i, j: (row(i), 0, sh_c)),
        pl.BlockSpec((1, 1, d), lambda i, j: (row(i), 0, sc_c)),
        pl.BlockSpec((1, d), lambda i, j: (0, 0)),
    ]


def _ffn_in(x, ada3, mod_cols, g, w, l, *, layout):
    d = w.shape[1]
    n_ctx, dec_seq = layout
    tm = _row_tile((n_ctx, dec_seq), 1024)
    xs, n_first, t = _x_parts(x, tm)
    row = _mod_row(n_ctx, dec_seq, tm)
    n = w.shape[2] // 2
    tn = 512 if n % 512 == 0 else n
    nj = n // tn
    return pl.pallas_call(
        functools.partial(_ffn_in_kernel, n_first=n_first),
        grid=(t // tm, nj),
        in_specs=_pm_specs(d, tm, row, mod_cols, n_first) + [
            pl.BlockSpec((None, d, tn), lambda i, j: (l, 0, j)),
            pl.BlockSpec((None, d, tn), lambda i, j: (l, 0, j + nj)),
        ],
        out_specs=pl.BlockSpec((tm, tn), lambda i, j: (i, j)),
        out_shape=jax.ShapeDtypeStruct((t, n), BF16),
        scratch_shapes=[pltpu.VMEM((tm, d), BF16)],
        compiler_params=_cparams("parallel", "arbitrary"),
        name="ffn_in",
    )(*xs, ada3, ada3, g.reshape(1, d), w, w)


def _mix_in(x, ada3, mod_cols, g, w_main, w_ab, l, *, layout):
    t, d = x.shape
    n_ctx, dec_seq = layout
    tm = _row_tile((n_ctx, dec_seq), 1024)
    row = _mod_row(n_ctx, dec_seq, tm)
    tn = 512
    return pl.pallas_call(
        _mix_in_kernel,
        grid=(t // tm, PROJ_MAIN // tn),
        in_specs=_pm_specs(d, tm, row, mod_cols) + [
            pl.BlockSpec((None, d, tn), lambda i, j: (l, 0, j)),
            pl.BlockSpec((None, d, LANE), lambda i, j: (l, 0, 0)),
        ],
        out_specs=[pl.BlockSpec((tm, tn), lambda i, j: (i, j)),
                   pl.BlockSpec((tm, LANE), lambda i, j: (i, 0))],
        out_shape=[jax.ShapeDtypeStruct((t, PROJ_MAIN), F32), jax.ShapeDtypeStruct((t, LANE), F32)],
        scratch_shapes=[pltpu.VMEM((tm, d), BF16)],
        compiler_params=_cparams("parallel", "arbitrary"),
        name="mix_in",
    )(x, ada3, ada3, g.reshape(1, d), w_main, w_ab)


def _ffn_out_kernel(*refs, n_first):
    a_ref, w_ref = refs[:2]
    x_refs = refs[2:-2]
    gate_ref, o_ref = refs[-2:]
    y = _dot(a_ref[...], w_ref[...])
    o_ref[...] = _read_x(x_refs, n_first, Ellipsis) + (0.5 * gate_ref[0]) * y


def _ffn_out(a, w, l, x, ada3, gate_col, *, layout):
    t, k = a.shape
    d = w.shape[2]
    n_ctx, dec_seq = layout
    tm = _row_tile((n_ctx, dec_seq), 1024)
    tn = min(512, d)
    xs, n_first, _ = _x_parts(x, tm)
    row = _mod_row(n_ctx, dec_seq, tm)
    gpb = d // tn
    return pl.pallas_call(
        functools.partial(_ffn_out_kernel, n_first=n_first),
        grid=(t // tm, d // tn),
        in_specs=[
            pl.BlockSpec((tm, k), lambda i, j: (i, 0)),
            pl.BlockSpec((None, k, tn), lambda i, j: (l, 0, j)),
            *_x_specs(n_first, tm, tn),
            pl.BlockSpec((1, 1, tn), lambda i, j: (row(i), 0, gate_col * gpb + j)),
        ],
        out_specs=pl.BlockSpec((tm, tn), lambda i, j: (i, j)),
        out_shape=jax.ShapeDtypeStruct((t, d), F32),
        compiler_params=_cparams("parallel", "parallel"),
        name="ffn_out",
    )(a, w, *xs, ada3)


def _mix_out_kernel(a1_ref, a2_ref, a3_ref, w_ref, x_ref, gate_ref, o_ref):
    y = (_dot(a1_ref[...], w_ref[0:GDN_W, :])
         + _dot(a2_ref[...], w_ref[GDN_W:GDN_W + DIFF_W, :])
         + _dot(a3_ref[...], w_ref[GDN_W + DIFF_W:MIX_W, :]))
    o_ref[...] = x_ref[...] + gate_ref[0] * y


def _mix_out(acts, w, l, x, ada3, gate_col, row0, *, layout):
    a1, a2, a3 = acts
    n = a1.shape[0]
    t, d = x.shape
    n_ctx, dec_seq = layout
    tm = _row_tile((n_ctx, dec_seq), 1024)
    tn = min(512, d)
    row = _mod_row(n_ctx, dec_seq, tm, row0)
    rb0 = row0 // tm
    gpb = d // tn
    return pl.pallas_call(
        _mix_out_kernel,
        grid=(n // tm, d // tn),
        in_specs=[
            pl.BlockSpec((tm, GDN_W), lambda i, j: (i, 0)),
            pl.BlockSpec((tm, DIFF_W), lambda i, j: (i, 0)),
            pl.BlockSpec((tm, POOL_W), lambda i, j: (i, 0)),
            pl.BlockSpec((None, MIX_W, tn), lambda i, j: (l, 0, j)),
            pl.BlockSpec((tm, tn), lambda i, j: (rb0 + i, j)),
            pl.BlockSpec((1, 1, tn), lambda i, j: (row(i), 0, gate_col * gpb + j)),
        ],
        out_specs=pl.BlockSpec((tm, tn), lambda i, j: (rb0 + i, j)),
        out_shape=jax.ShapeDtypeStruct((t, d), F32),
        input_output_aliases={4: 0},
        compiler_params=_cparams("parallel", "parallel"),
        name="mix_out",
    )(a1, a2, a3, w, x, ada3)


def _final_norm_kernel(x_ref, g_ref, o_ref):
    x = x_ref[...]
    o_ref[...] = x * lax.rsqrt(jnp.mean(x * x, axis=-1, keepdims=True) + EPS) * g_ref[...]


def _final_norm(x, g, row0, nrows):
    d = x.shape[1]
    tm = 256
    rb0 = row0 // tm
    return pl.pallas_call(
        _final_norm_kernel,
        grid=(nrows // tm,),
        in_specs=[pl.BlockSpec((tm, d), lambda i: (rb0 + i, 0)), pl.BlockSpec((1, d), lambda i: (0, 0))],
        out_specs=pl.BlockSpec((tm, d), lambda i: (i, 0)),
        out_shape=jax.ShapeDtypeStruct((nrows, d), F32),
        compiler_params=_cparams("parallel"),
        name="final_norm",
    )(x, g.reshape(1, d))


def _shift_rows(x, off, row):
    if off == 0:
        return x
    n = x.shape[0]
    y = pltpu.roll(x, (-off) % n, 0)
    ok = (row + off >= 0) & (row + off < n)
    return jnp.where(ok, y, 0.0)


def _gdn_prep_kernel(x_ref, w_ref, o_ref):
    n = x_ref.shape[0]
    left = CONV_K // 2
    edge = 16
    row = lax.broadcasted_iota(jnp.int32, (edge, LANE), 0)
    kind = pl.program_id(1)
    for h in range(GDN_HEADS):
        cols = slice(h * LANE, (h + 1) * LANE)
        x = x_ref[:, cols]

        def conv(xb, shift):
            acc = jnp.zeros_like(xb)
            for j in range(CONV_K):
                acc = acc + shift(xb, j - left) * w_ref[j:j + 1, cols]
            return acc

        acc = conv(x, lambda xb, off: xb if off == 0 else pltpu.roll(xb, (-off) % n, 0))
        head = conv(x[0:edge], lambda xb, off: _shift_rows(xb, off, row))
        tail = conv(x[n - edge:n], lambda xb, off: _shift_rows(xb, off, row))
        acc = jnp.concatenate([head[0:8], acc[8:n - 8], tail[edge - 8:edge]], axis=0)
        y = _silu(acc)
        inv = lax.rsqrt(jnp.sum(y * y, axis=-1, keepdims=True) + EPS)
        inv = inv * jnp.where(kind == 0, GDN_DK ** -0.5, 1.0)
        o_ref[:, cols] = jnp.where(kind < 2, y * inv, y)


def _gdn_prep(proj, conv_w, row0, nseq, seqlen):
    blk0 = row0 // seqlen
    return pl.pallas_call(
        _gdn_prep_kernel,
        grid=(nseq, 3),
        in_specs=[
            pl.BlockSpec((seqlen, GDN_W), lambda s, c: (blk0 + s, CB_Q // GDN_HEADS + c)),
            pl.BlockSpec((CONV_K, GDN_W), lambda s, c: (0, c)),
        ],
        out_specs=pl.BlockSpec((seqlen, GDN_W), lambda s, c: (s, c)),
        out_shape=jax.ShapeDtypeStruct((nseq * seqlen, 3 * GDN_W), F32),
        compiler_params=_cparams("parallel", "parallel"),
        name="gdn_prep",
    )(proj, conv_w)


def _unit_tri_solve_many(lmats, rhss, eye):
    xs = [eye - l for l in lmats]
    ps = [_mm(l, l) for l in lmats]
    steps = int(math.log2(CHUNK)) - 1
    for s in range(steps):
        xs = [x + _mm(x, p) for x, p in zip(xs, ps)]
        if s + 1 < steps:
            ps = [_mm(p, p) for p in ps]
    x0s = [_mm(x, r) for x, r in zip(xs, rhss)]
    resids = [r - (x0 + _mm(l, x0)) for l, r, x0 in zip(lmats, rhss, x0s)]
    return [x0 + _mm(x, rs) for x, x0, rs in zip(xs, x0s, resids)]


def _gdn_intra_kernel(q_ref, k_ref, v_ref, ab_ref, alog_ref, dt_ref,
                      u_ref, w_ref, qg_ref, kd_ref, a_ref, gl_ref):
    n = GDN_BLK
    ab = ab_ref[...]
    lane = lax.broadcasted_iota(jnp.int32, ab.shape, 1)
    sp = jnp.maximum(ab + dt_ref[...], 0.0) + jnp.log(1.0 + jnp.exp(-jnp.abs(ab + dt_ref[...])))
    gmat = -jnp.exp(alog_ref[...]) * sp
    bmat = _sigmoid(ab)

    ri = lax.broadcasted_iota(jnp.int32, (n, n), 0)
    ci = lax.broadcasted_iota(jnp.int32, (n, n), 1)
    same = (ri // CHUNK) == (ci // CHUNK)
    eye = (ri == ci).astype(F32)
    r8 = lax.broadcasted_iota(jnp.int32, (8, LANE), 0)
    gtot_all = _dot_exact_lhs(same.astype(BF16), gmat)
    gcf_all = _dot_exact_lhs((same & (ci <= ri)).astype(BF16), gmat)
    gc_dirs = (gcf_all, (gtot_all - gcf_all) + gmat)
    incl_dirs = (same & (ci <= ri), same & (ci >= ri))
    strict_dirs = (same & (ci < ri), same & (ci > ri))

    lmats, rhss, dests = [], [], []
    for h in range(GDN_HEADS):
        cols = slice(h * LANE, (h + 1) * LANE)
        q = q_ref[:, cols]
        k = k_ref[:, cols]
        v = v_ref[:, cols]
        kk = _mm(k, k, _dot_nt)
        qk = _mm(q, k, _dot_nt)
        for d in range(2):
            incl = incl_dirs[d]
            sel_g = lane == d * GDN_HEADS + h
            sel_b = lane == 2 * GDN_HEADS + d * GDN_HEADS + h
            gc = jnp.sum(jnp.where(sel_g, gc_dirs[d], 0.0), axis=1, keepdims=True)
            gtot = jnp.sum(jnp.where(sel_g, gtot_all, 0.0), axis=1, keepdims=True)
            beta = jnp.sum(jnp.where(sel_b, bmat, 0.0), axis=1, keepdims=True)
            gc_b = jnp.broadcast_to(gc, (n, LANE))
            gc_row = jnp.transpose(gc_b)[0:1, :]
            diff = jnp.where(incl, gc - gc_row, 0.0)
            decay = jnp.where(incl, jnp.exp(diff), 0.0)
            eg = jnp.exp(gc_b)
            lmats.append(jnp.where(strict_dirs[d], kk * beta * decay, 0.0))
            rhss.append(jnp.concatenate([v * beta, (k * beta) * eg], axis=1))
            dests.append((d, cols))
            qg_ref[d, :, cols] = q * eg
            kd_ref[d, :, cols] = k * jnp.exp(jnp.broadcast_to(gtot - gc, (n, LANE)))
            amat = qk * decay
            a_ref[d, h] = (amat[:, 0:CHUNK] + amat[:, CHUNK:2 * CHUNK]) + (
                amat[:, 2 * CHUNK:3 * CHUNK] + amat[:, 3 * CHUNK:4 * CHUNK])
            egt = jnp.exp(jnp.broadcast_to(gtot, (n, LANE)))
            gl8 = jnp.zeros((8, LANE), F32)
            for c in range(n // CHUNK):
                gl8 = jnp.where(r8 == c, egt[c * CHUNK:c * CHUNK + 8, :], gl8)
            gl_ref[d, :, cols] = gl8

    for (d, cols), uw in zip(dests, _unit_tri_solve_many(lmats, rhss, eye)):
        u_ref[d, :, cols] = uw[:, :GDN_DV]
        w_ref[d, :, cols] = uw[:, GDN_DV:]


def _gdn_intra(qkv, proj_ab, alog_row, dt_row, row0):
    t = qkv.shape[0]
    nb = t // GDN_BLK
    pb0 = row0 // GDN_BLK
    big = jax.ShapeDtypeStruct((2, t, GDN_W), F32)
    big_spec = pl.BlockSpec((2, GDN_BLK, GDN_W), lambda i: (0, i, 0))
    return pl.pallas_call(
        _gdn_intra_kernel,
        grid=(nb,),
        in_specs=[
            pl.BlockSpec((GDN_BLK, GDN_W), lambda i: (i, 0)),
            pl.BlockSpec((GDN_BLK, GDN_W), lambda i: (i, 1)),
            pl.BlockSpec((GDN_BLK, GDN_W), lambda i: (i, 2)),
            pl.BlockSpec((GDN_BLK, LANE), lambda i: (pb0 + i, 0)),
            pl.BlockSpec((1, LANE), lambda i: (0, 0)),
            pl.BlockSpec((1, LANE), lambda i: (0, 0)),
        ],
        out_specs=[big_spec, big_spec, big_spec, big_spec,
                   pl.BlockSpec((2, GDN_HEADS, GDN_BLK, CHUNK), lambda i: (0, 0, i, 0)),
                   pl.BlockSpec((2, 8, GDN_W), lambda i: (0, i, 0))],
        out_shape=[big, big, big, big,
                   jax.ShapeDtypeStruct((2, GDN_HEADS, t, CHUNK), F32),
                   jax.ShapeDtypeStruct((2, nb * 8, GDN_W), F32)],
        compiler_params=_cparams("parallel"),
        name="gdn_intra",
    )(qkv, qkv, qkv, proj_ab, alog_row, dt_row)


def _gdn_scan_kernel(*refs, has_init):
    if has_init:
        s0_ref, refs = refs[0], refs[1:]
    ins = (refs[0:6], refs[6:12])
    o_refs, sf_ref, s_ref = refs[12:14], refs[14], refs[15]
    blk = pl.program_id(1)
    nchunk = GDN_BLK // CHUNK

    @pl.when(blk == 0)
    def _():
        if has_init:
            s_ref[...] = s0_ref[...]
        else:
            s_ref[...] = jnp.zeros_like(s_ref)

    U, W, QG, KD, A, GL = range(6)
    chains = [(d, h) for d in range(2) for h in range(GDN_HEADS)]
    states = [s_ref[d, h] for d, h in chains]
    for c in range(nchunk):
        ccs = [c if d == 0 else nchunk - 1 - c for d, _ in chains]
        rows = [slice(cc * CHUNK, (cc + 1) * CHUNK) for cc in ccs]
        cols = [slice(h * LANE, (h + 1) * LANE) for _, h in chains]
        n = range(len(chains))
        ws = [_mm(ins[chains[i][0]][W][rows[i], cols[i]], states[i]) for i in n]
        qs = [_mm(ins[chains[i][0]][QG][rows[i], cols[i]], states[i]) for i in n]
        v_news = [ins[chains[i][0]][U][rows[i], cols[i]] - ws[i] for i in n]
        for i in n:
            d, h = chains[i]
            o_refs[d][rows[i], cols[i]] = qs[i] + _mm(ins[d][A][h, rows[i], :], v_news[i])
        states = [states[i] * ins[chains[i][0]][GL][ccs[i]:ccs[i] + 1, cols[i]]
                  + _mm(ins[chains[i][0]][KD][rows[i], cols[i]], v_news[i], _dot_tn) for i in n]
    for (d, h), s in zip(chains, states):
        s_ref[d, h] = s

    @pl.when(blk == pl.num_programs(1) - 1)
    def _():
        sf_ref[...] = s_ref[...]


def _gdn_scan(s0, u, w, qg, kd, a, gl, nseq, seqlen):
    t = u.shape[1]
    nblk = seqlen // GDN_BLK
    in_specs, args = [], []
    for d in range(2):
        rb = (lambda s, b: s * nblk + b) if d == 0 else (lambda s, b: s * nblk + nblk - 1 - b)
        big = pl.BlockSpec((None, GDN_BLK, GDN_W), lambda s, b, d=d, rb=rb: (d, rb(s, b), 0))
        in_specs += [big, big, big, big,
                     pl.BlockSpec((None, GDN_HEADS, GDN_BLK, CHUNK), lambda s, b, d=d, rb=rb: (d, 0, rb(s, b), 0)),
                     pl.BlockSpec((None, 8, GDN_W), lambda s, b, d=d, rb=rb: (d, rb(s, b), 0))]
        args += [u, w, qg, kd, a, gl]
    state_spec = pl.BlockSpec((None, 2, GDN_HEADS, GDN_DK, GDN_DV), lambda s, b: (s, 0, 0, 0, 0))
    if s0 is not None:
        in_specs.insert(0, state_spec)
        args.insert(0, s0)
    out_specs = [pl.BlockSpec((GDN_BLK, GDN_W), lambda s, b: (s * nblk + b, 0)),
                 pl.BlockSpec((GDN_BLK, GDN_W), lambda s, b: (s * nblk + nblk - 1 - b, 0)),
                 state_spec]
    return pl.pallas_call(
        functools.partial(_gdn_scan_kernel, has_init=s0 is not None),
        grid=(nseq, nblk),
        in_specs=in_specs,
        out_specs=out_specs,
        out_shape=[jax.ShapeDtypeStruct((t, GDN_W), F32), jax.ShapeDtypeStruct((t, GDN_W), F32),
                   jax.ShapeDtypeStruct((nseq, 2, GDN_HEADS, GDN_DK, GDN_DV), F32)],
        scratch_shapes=[pltpu.VMEM((2, GDN_HEADS, GDN_DK, GDN_DV), F32)],
        compiler_params=_cparams("parallel", "arbitrary"),
        name="gdn_scan",
    )(*args)


def _gdn_post_kernel(of_ref, ob_ref, z_ref, g_ref, out_ref):
    for h in range(GDN_HEADS):
        cols = slice(h * LANE, (h + 1) * LANE)
        o = of_ref[:, cols] + ob_ref[:, cols]
        y = o * lax.rsqrt(jnp.mean(o * o, axis=-1, keepdims=True) + EPS) * g_ref[...]
        out_ref[:, cols] = (y * _silu(z_ref[:, cols])).astype(out_ref.dtype)


def _gdn_post(o_f, o_b, proj, g, row0):
    t = o_f.shape[0]
    tm = 256
    pb0 = row0 // tm
    return pl.pallas_call(
        _gdn_post_kernel,
        grid=(t // tm,),
        in_specs=[
            pl.BlockSpec((tm, GDN_W), lambda i: (i, 0)),
            pl.BlockSpec((tm, GDN_W), lambda i: (i, 0)),
            pl.BlockSpec((tm, GDN_W), lambda i: (pb0 + i, CB_Z // GDN_HEADS)),
            pl.BlockSpec((1, GDN_DV), lambda i: (0, 0)),
        ],
        out_specs=pl.BlockSpec((tm, GDN_W), lambda i: (i, 0)),
        out_shape=jax.ShapeDtypeStruct((t, GDN_W), BF16),
        compiler_params=_cparams("parallel"),
        name="gdn_post",
    )(o_f, o_b, proj, g.reshape(1, GDN_DV))


def _rope_rotate(x, cos, sin_signed):
    lane = lax.broadcasted_iota(jnp.int32, x.shape, 1)
    first = (lane % 32) < 16
    partner = jnp.where(first, pltpu.roll(x, LANE - 16, 1), pltpu.roll(x, 16, 1))
    return x * cos + partner * sin_signed


def _rope_tables(seqlen):
    pos = jnp.arange(seqlen)
    rowp = (pos // GRID_W).astype(F32)
    colp = (pos % GRID_W).astype(F32)
    half = DIFF_DQK // 2
    inv_freq = ROPE_BASE ** (-jnp.arange(0, half, 2, dtype=F32) / half)
    c = jnp.arange(LANE)
    c64 = c % DIFF_DQK
    p = jnp.where((c64 // half)[None, :] == 0, rowp[:, None], colp[:, None])
    ang = p * inv_freq[c64 % (half // 2)][None, :]
    sign = jnp.where((c64 % half) < half // 2, -1.0, 1.0).astype(F32)
    return jnp.cos(ang), jnp.sin(ang) * sign[None, :]


def _diff_attn_kernel(*refs, seg_lens, lam_init, rope, hp, n_alias, emit_cache):
    nseg = len(seg_lens)
    q_ref = refs[0]
    k_refs = refs[1:1 + nseg]
    v_refs = refs[1 + nseg:1 + 2 * nseg]
    pos = 1 + 2 * nseg
    if rope:
        cosq_ref, sinq_ref, cosk_ref, sink_ref = refs[pos:pos + 4]
        pos += 4
    lam_ref, g_ref = refs[pos:pos + 2]
    pos += 2 + n_alias
    o_ref = refs[pos]
    if emit_cache:
        kc_ref, vc_ref = refs[pos + 1:pos + 3]
    kb_ref, vb_ref = refs[-2:]
    offs = [sum(seg_lens[:s]) for s in range(nseg)]
    hcols = [slice(hh * DIFF_DV, (hh + 1) * DIFF_DV) for hh in range(hp)]

    @pl.when(pl.program_id(2) == 0)
    def _():
        if emit_cache:
            kc_ref[...] = k_refs[-1][...]
            vc_ref[...] = v_refs[-1][...]
        for hh in range(hp):
            vb_ref[:, (2 * hh + 1) * DIFF_DV:(2 * hh + 2) * DIFF_DV] = jnp.ones((vb_ref.shape[0], DIFF_DV), BF16)
            for s in range(nseg):
                rows = slice(offs[s], offs[s] + seg_lens[s])
                k = k_refs[s][:, hcols[hh]]
                if rope and s == nseg - 1:
                    k = _rope_rotate(k, cosk_ref[...], sink_ref[...])
                kb_ref[rows, hcols[hh]] = k.astype(BF16)
                vb_ref[rows, 2 * hh * DIFF_DV:(2 * hh + 1) * DIFF_DV] = v_refs[s][:, hcols[hh]].astype(BF16)

    lv = lam_ref[...]
    lam = (jnp.exp(jnp.sum(lv[0:1] * lv[1:2], axis=1, keepdims=True))
           - jnp.exp(jnp.sum(lv[2:3] * lv[3:4], axis=1, keepdims=True)) + lam_init)
    lane = lax.broadcasted_iota(jnp.int32, (ATTN_SUB, DIFF_DV), 1)

    def scores(hh, rows):
        q = q_ref[rows, hcols[hh]]
        if rope:
            q = _rope_rotate(q, cosq_ref[rows, :], sinq_ref[rows, :])
        q = q * (DIFF_DQK ** -0.5)
        kb = kb_ref[:, hcols[hh]]
        return [_dot_nt(jnp.where((lane // DIFF_DQK) == m, q, 0.0).astype(BF16), kb) for m in range(2)]

    units = [(hh, slice(r * ATTN_SUB, (r + 1) * ATTN_SUB))
             for hh in range(hp) for r in range(q_ref.shape[0] // ATTN_SUB)]
    ss_next = scores(*units[0])
    for n, (hh, rows) in enumerate(units):
        ss = ss_next
        if n + 1 < len(units):
            ss_next = scores(*units[n + 1])
        vb = vb_ref[:, 2 * hh * DIFF_DV:(2 * hh + 2) * DIFF_DV]
        es = [jnp.exp(s - jnp.max(s, axis=1, keepdims=True)).astype(BF16) for s in ss]
        accs = [_dot(e, vb) for e in es]
        outs = [acc[:, 0:DIFF_DV] / acc[:, DIFF_DV:] for acc in accs]
        o = outs[0] - lam * outs[1]
        y = o * lax.rsqrt(jnp.mean(o * o, axis=-1, keepdims=True) + EPS) * g_ref[...]
        o_ref[rows, hcols[hh]] = (y * (1.0 - lam_init)).astype(o_ref.dtype)


def _diff_attn(proj, row0, nseq, seqlen, past_kv, l, lam_vec, g, cache_out):
    tq = min(4 * ATTN_SUB, seqlen)
    hp = min(DIFF_HEADS, max(1, ATTN_STEP_ROWS // seqlen))
    nqb = seqlen // tq
    qb0 = row0 // tq
    sb0 = row0 // seqlen
    wcols = hp * DIFF_DV
    q_spec = pl.BlockSpec((tq, wcols), lambda b, h, i: (qb0 + b * nqb + i, CB_DQ // hp + h))
    k_new = pl.BlockSpec((seqlen, wcols), lambda b, h, i: (sb0 + b, CB_DK // hp + h))
    v_new = pl.BlockSpec((seqlen, wcols), lambda b, h, i: (sb0 + b, CB_DV // hp + h))
    arrs, specs, seg_lens = [proj], [q_spec], [seqlen]
    if past_kv is None:
        arrs += [proj, proj]
        specs += [k_new, v_new]
    else:
        past = past_kv[0].shape[2]
        cached = pl.BlockSpec((None, None, past, wcols), lambda b, h, i: (b, l, 0, h))
        arrs += [past_kv[0], proj, past_kv[1], proj]
        specs += [cached, k_new, cached, v_new]
        seg_lens = [past, seqlen]
        cos, sin = _rope_tables(seqlen)
        arrs += [cos, sin, cos, sin]
        specs += [pl.BlockSpec((tq, DIFF_DV), lambda b, h, i: (i, 0))] * 2 + [
            pl.BlockSpec((seqlen, DIFF_DV), lambda b, h, i: (0, 0))] * 2
    arrs += [lam_vec, g.reshape(1, DIFF_DV)]
    specs += [pl.BlockSpec((4, DIFF_DQK), lambda b, h, i: (0, 0)),
              pl.BlockSpec((1, DIFF_DV), lambda b, h, i: (0, 0))]
    out_specs = [pl.BlockSpec((tq, wcols), lambda b, h, i: (b * nqb + i, h))]
    out_shape = [jax.ShapeDtypeStruct((nseq * seqlen, DIFF_W), BF16)]
    aliases, n_alias = {}, 0
    if cache_out is not None:
        depth, prev = cache_out
        entry = pl.BlockSpec((None, None, seqlen, wcols), lambda b, h, i: (b, l, 0, h))
        out_specs += [entry, entry]
        out_shape += [jax.ShapeDtypeStruct((nseq, depth, seqlen, DIFF_W), F32)] * 2
        if prev is not None:
            aliases = {len(arrs): 1, len(arrs) + 1: 2}
            n_alias = 2
            arrs += list(prev)
            specs += [pl.BlockSpec(memory_space=pl.ANY)] * 2
    lk = sum(seg_lens)
    outs = pl.pallas_call(
        functools.partial(_diff_attn_kernel, seg_lens=tuple(seg_lens), lam_init=_lambda_init(l),
                          rope=past_kv is not None, hp=hp, n_alias=n_alias, emit_cache=cache_out is not None),
        grid=(nseq, DIFF_HEADS // hp, nqb),
        in_specs=specs,
        out_specs=out_specs,
        out_shape=out_shape,
        input_output_aliases=aliases,
        scratch_shapes=[pltpu.VMEM((lk, wcols), BF16), pltpu.VMEM((lk, 2 * wcols), BF16)],
        compiler_params=_cparams("parallel", "parallel", "arbitrary"),
        name="diff_attn",
    )(*arrs)
    return outs[0], (tuple(outs[1:]) if cache_out is not None else None)


def _pool_kernel(x_ref, w_ref, sc_ref, o_ref):
    n = x_ref.shape[0]
    row = lax.broadcasted_iota(jnp.int32, (n, POOL_GW), 0)
    for gi, win in enumerate(POOL_WINDOWS):
        cols = slice(gi * POOL_GW, (gi + 1) * POOL_GW)
        x = x_ref[:, cols]
        a = win // 2
        b = win - a - 1
        acc = x
        for off in range(-a, b + 1):
            if off != 0:
                acc = acc + _shift_rows(x, off, row)
        cnt = (jnp.minimum(row + b + 1, n) - jnp.maximum(row - a, 0)).astype(F32)
        pooled = acc / cnt - x
        y = _dot(pooled.astype(BF16), w_ref[gi].astype(BF16))
        o_ref[:, cols] = (y * sc_ref[:, cols]).astype(o_ref.dtype)


def _pool(proj, pool_w, pool_scale, row0, nseq, seqlen):
    blk0 = row0 // seqlen
    return pl.pallas_call(
        _pool_kernel,
        grid=(nseq,),
        in_specs=[
            pl.BlockSpec((seqlen, POOL_W), lambda s: (blk0 + s, CB_PIN // len(POOL_WINDOWS))),
            pl.BlockSpec((len(POOL_WINDOWS), POOL_GW, POOL_GW), lambda s: (0, 0, 0)),
            pl.BlockSpec((1, POOL_W), lambda s: (0, 0)),
        ],
        out_specs=pl.BlockSpec((seqlen, POOL_W), lambda s: (s, 0)),
        out_shape=jax.ShapeDtypeStruct((nseq * seqlen, POOL_W), BF16),
        compiler_params=_cparams("parallel"),
        name="pool",
    )(proj, pool_w, pool_scale.reshape(1, POOL_W))


def _mix_group(proj, proj_ab, lp, l, row0, nseq, seqlen, ctx, cache_out):
    qkv = _gdn_prep(proj, lp["gdn_conv"], row0, nseq, seqlen)
    u, w, qg, kd, a, gl = _gdn_intra(qkv, proj_ab, lp["alog_row"], lp["dt_row"], row0)
    o_f, o_b, s_fin = _gdn_scan(None if ctx is None else ctx[2], u, w, qg, kd, a, gl, nseq, seqlen)
    o_gdn = _gdn_post(o_f, o_b, proj, lp["gdn_norm"], row0)

    o_diff, caches = _diff_attn(proj, row0, nseq, seqlen, None if ctx is None else ctx[:2], l,
                                lp["diff_lam"], lp["diff_norm"], cache_out)
    o_pool = _pool(proj, lp["pool_w"], lp["pool_scale"], row0, nseq, seqlen)
    return (o_gdn, o_diff, o_pool), s_fin, caches


def _split_w_in(w_in):
    depth, d, _ = w_in.shape
    o_ab = 4 * GDN_W
    o_dq = o_ab + 4 * GDN_HEADS
    pad = jnp.zeros((depth, d, LANE - 4 * GDN_HEADS), w_in.dtype)
    main = jnp.concatenate([w_in[..., :o_ab], w_in[..., o_dq:]], axis=-1)
    ab = jnp.concatenate([w_in[..., o_ab:o_dq], pad], axis=-1)
    return main.astype(BF16), ab.astype(BF16)


def kernel(x_prompt, x_sample, c, cache_k, cache_v, state_gdn, c_ctx, w_ada, b_ada, norm_ffn1, ffn1_in,
           ffn1_out, norm_mix, w_in, gdn_conv, gdn_a_log, gdn_dt_bias, gdn_norm, diff_lam, diff_norm,
           pool_w, pool_scale, w_out, norm_ffn2, ffn2_in, ffn2_out, final_norm):
    batch, seq, d = x_prompt.shape
    dec_batch, dec_seq, _ = x_sample.shape
    depth = w_ada.shape[0]
    n_ctx = batch * seq
    n_dec = dec_batch * dec_seq
    layout = (n_ctx, dec_seq)
    past = cache_k.shape[2]

    x = (x_prompt.reshape(n_ctx, d), x_sample.reshape(n_dec, d))
    cond8 = jnp.concatenate([c_ctx[None, :], c, jnp.zeros((8 - 1 - dec_batch, d), F32)], axis=0)
    ada = _ada(cond8, w_ada, b_ada)
    cache_k4 = cache_k.reshape(dec_batch, depth, past, DIFF_W)
    cache_v4 = cache_v.reshape(dec_batch, depth, past, DIFF_W)
    w_in_main, w_in_ab = _split_w_in(w_in)
    w_out_b = w_out.astype(BF16)
    ffn_w = [(ffn1_in, ffn1_out.astype(BF16)), (ffn2_in, ffn2_out.astype(BF16))]

    caches, ss = None, []
    for l in range(depth):
        ada3 = ada[l, :1 + dec_batch].reshape(1 + dec_batch, 1, N_MOD * d)
        lane_pad = jnp.zeros((LANE - 2 * GDN_HEADS,), F32)
        lp = dict(
            gdn_conv=gdn_conv[l], gdn_norm=gdn_norm[l], diff_lam=diff_lam[l], diff_norm=diff_norm[l],
            pool_w=pool_w[l], pool_scale=pool_scale[l],
            alog_row=jnp.concatenate([gdn_a_log[l].reshape(-1), lane_pad]).reshape(1, LANE),
            dt_row=jnp.concatenate([gdn_dt_bias[l].reshape(-1), lane_pad]).reshape(1, LANE),
        )
        act = _ffn_in(x, ada3, (0, 1), norm_ffn1[l], ffn_w[0][0], l, layout=layout)
        x = _ffn_out(act, ffn_w[0][1], l, x, ada3, 2, layout=layout)

        proj, proj_ab = _mix_in(x, ada3, (3, 4), norm_mix[l], w_in_main, w_in_ab, l, layout=layout)
        acts_c, s_c, caches = _mix_group(proj, proj_ab, lp, l, 0, batch, seq, None, (depth, caches))
        acts_d, _, _ = _mix_group(proj, proj_ab, lp, l, n_ctx, dec_batch, dec_seq,
                                  (cache_k4, cache_v4, state_gdn[:, l]), None)
        x = _mix_out(acts_c, w_out_b, l, x, ada3, 5, 0, layout=layout)
        x = _mix_out(acts_d, w_out_b, l, x, ada3, 5, n_ctx, layout=layout)
        ss.append(s_c)

        act = _ffn_in(x, ada3, (6, 7), norm_ffn2[l], ffn_w[1][0], l, layout=layout)
        x = _ffn_out(act, ffn_w[1][1], l, x, ada3, 8, layout=layout)

    y_prompt = _final_norm(x, final_norm, 0, n_ctx).reshape(batch, seq, d)
    y_sample = _final_norm(x, final_norm, n_ctx, n_dec).reshape(dec_batch, dec_seq, d)
    new_k, new_v = (t.reshape(batch, depth, seq, DIFF_HEADS, DIFF_DV) for t in caches)
    return (y_prompt, y_sample, new_k, new_v, jnp.stack(ss, axis=1))
```

```python
import functools
import math

import jax
import jax.numpy as jnp
from jax import lax
from jax.experimental import pallas as pl
from jax.experimental.pallas import tpu as pltpu

F32 = jnp.float32
BF16 = jnp.bfloat16

GRID_W = 64
GDN_HEADS = 4
GDN_DK = 128
GDN_DV = 128
GDN_W = GDN_HEADS * GDN_DV
CONV_K = 4
CHUNK = 64
DIFF_HEADS = 8
DIFF_DQK = 64
DIFF_DV = 2 * DIFF_DQK
DIFF_W = DIFF_HEADS * DIFF_DV
ROPE_BASE = 10000.0
POOL_WINDOWS = (2, 4, 8, 16)
POOL_GW = 128
POOL_W = len(POOL_WINDOWS) * POOL_GW
MIX_W = GDN_W + DIFF_W + POOL_W
N_MOD = 9
EPS = 1e-6

LANE = 128
GDN_BLK = 4 * CHUNK
ATTN_SUB = 256
ATTN_STEP_ROWS = 2048
PRO_SUB = 256
VMEM_LIMIT = 60 * 1024 * 1024

CB_Q, CB_K, CB_V, CB_Z = 0, 4, 8, 12
CB_DQ, CB_DK, CB_DV, CB_PIN = 16, 24, 32, 40
PROJ_MAIN = 44 * LANE


def _lambda_init(l):
    return 0.8 - 0.6 * math.exp(-0.3 * l)


def _cparams(*sem):
    return pltpu.CompilerParams(dimension_semantics=sem, vmem_limit_bytes=VMEM_LIMIT)


def _dot(a, b):
    return lax.dot_general(a, b, (((1,), (0,)), ((), ())), preferred_element_type=F32)


def _dot_nt(a, b):
    return lax.dot_general(a, b, (((1,), (1,)), ((), ())), preferred_element_type=F32)


def _dot_tn(a, b):
    return lax.dot_general(a, b, (((0,), (0,)), ((), ())), preferred_element_type=F32)


def _mm(a, b, dot=_dot):
    return dot(a.astype(BF16), b.astype(BF16))


def _dot_exact_lhs(m, b):
    b0 = b.astype(BF16)
    r = b - b0.astype(F32)
    b1 = r.astype(BF16)
    b2 = (r - b1.astype(F32)).astype(BF16)
    return _dot(m, b0) + (_dot(m, b1) + _dot(m, b2))


def _sigmoid(x):
    return 0.5 * (1.0 + jnp.tanh(0.5 * x))


def _silu(x):
    return x * _sigmoid(x)


def _ada_kernel(cond_ref, w_ref, b_ref, o_ref):
    s = _silu(cond_ref[...]).astype(BF16)
    o_ref[...] = _dot(s, w_ref[...].astype(BF16)) + b_ref[...]


def _ada(cond8, w_ada, b_ada):
    depth, d, n = w_ada.shape
    tn = 1024
    return pl.pallas_call(
        _ada_kernel,
        grid=(depth, n // tn),
        in_specs=[
            pl.BlockSpec((8, d), lambda l, j: (0, 0)),
            pl.BlockSpec((None, d, tn), lambda l, j: (l, 0, j)),
            pl.BlockSpec((None, 1, tn), lambda l, j: (l, 0, j)),
        ],
        out_specs=pl.BlockSpec((None, 8, tn), lambda l, j: (l, 0, j)),
        out_shape=jax.ShapeDtypeStruct((depth, 8, n), F32),
        compiler_params=_cparams("parallel", "parallel"),
        name="ada",
    )(cond8, w_ada, b_ada.reshape(depth, 1, n))


def _norm_mod(x, g, shift, scale):
    y = x * lax.rsqrt(jnp.mean(x * x, axis=-1, keepdims=True) + EPS) * g
    return y * (1.0 + scale) + shift


def _read_x(x_refs, n_first, idx):
    if len(x_refs) == 1:
        return x_refs[0][idx]
    return jnp.where(pl.program_id(0) < n_first, x_refs[0][idx], x_refs[1][idx])


def _norm_mod_rows(x_refs, n_first, sh_ref, sc_ref, g_ref, rows):
    x = _read_x(x_refs, n_first, (rows, slice(None)))
    return _norm_mod(x, g_ref[...], sh_ref[0], sc_ref[0]).astype(BF16)


def _row_subs(n):
    return [slice(r * PRO_SUB, (r + 1) * PRO_SUB) for r in range(n // PRO_SUB)]


def _mix_in_kernel(x_ref, sh_ref, sc_ref, g_ref, w_ref, wab_ref, o_ref, ab_ref, h_ref):
    @pl.when(pl.program_id(1) == 0)
    def _():
        for rows in _row_subs(x_ref.shape[0]):
            h = _norm_mod_rows((x_ref,), None, sh_ref, sc_ref, g_ref, rows)
            h_ref[rows, :] = h
            ab_ref[rows, :] = _dot(h, wab_ref[...])
            o_ref[rows, :] = _dot(h, w_ref[...])

    @pl.when(pl.program_id(1) != 0)
    def _():
        o_ref[...] = _dot(h_ref[...], w_ref[...])


def _ffn_in_kernel(*refs, n_first):
    x_refs = refs[:-7]
    sh_ref, sc_ref, g_ref, wg_ref, wu_ref, o_ref, h_ref = refs[-7:]

    def swiglu(h):
        gate = _dot(h, wg_ref[...].astype(BF16))
        up = _dot(h, wu_ref[...].astype(BF16))
        return (_silu(gate) * up).astype(o_ref.dtype)

    @pl.when(pl.program_id(1) == 0)
    def _():
        for rows in _row_subs(h_ref.shape[0]):
            h = _norm_mod_rows(x_refs, n_first, sh_ref, sc_ref, g_ref, rows)
            h_ref[rows, :] = h
            o_ref[rows, :] = swiglu(h)

    @pl.when(pl.program_id(1) != 0)
    def _():
        o_ref[...] = swiglu(h_ref[...])


def _row_tile(groups, cap):
    tm = cap
    while any(g % tm for g in groups):
        tm //= 2
    return tm


def _mod_row(n_ctx, dec_seq, tm, row0=0):
    def f(i):
        r = row0 + i * tm
        return jnp.where(r < n_ctx, 0, 1 + (r - n_ctx) // dec_seq)
    return f


def _x_parts(x, tm):
    if isinstance(x, tuple):
        return list(x), x[0].shape[0] // tm, sum(p.shape[0] for p in x)
    return [x], None, x.shape[0]


def _x_specs(n_first, tm, tn):
    if n_first is None:
        return [pl.BlockSpec((tm, tn), lambda i, j: (i, j))]
    return [pl.BlockSpec((tm, tn), lambda i, j: (jnp.where(i < n_first, i, n_first - 1), jnp.where(i < n_first, j, 0))),
            pl.BlockSpec((tm, tn), lambda i, j: (jnp.where(i < n_first, 0, i - n_first), jnp.where(i < n_first, 0, j)))]


def _pm_specs(d, tm, row, mod_cols, n_first=None):
    sh_c, sc_c = mod_cols
    x_specs = [pl.BlockSpec((tm, d), lambda i, j: (i, 0))] if n_first is None else [
        pl.BlockSpec((tm, d), lambda i, j: (jnp.minimum(i, n_first - 1), 0)),
        pl.BlockSpec((tm, d), lambda i, j: (jnp.maximum(i - n_first, 0), 0))]
    return x_specs + [
        pl.BlockSpec((1, 1, d), lambda i, j: (row(i), 0, sh_c)),
        pl.BlockSpec((1, 1, d), lambda i, j: (row(i), 0, sc_c)),
        pl.BlockSpec((1, d), lambda i, j: (0, 0)),
    ]


def _ffn_in(x, ada3, mod_cols, g, w, l, *, layout):
    d = w.shape[1]
    n_ctx, dec_seq = layout
    tm = _row_tile((n_ctx, dec_seq), 1024)
    xs, n_first, t = _x_parts(x, tm)
    row = _mod_row(n_ctx, dec_seq, tm)
    n = w.shape[2] // 2
    tn = 512 if n % 512 == 0 else n
    nj = n // tn
    return pl.pallas_call(
        functools.partial(_ffn_in_kernel, n_first=n_first),
        grid=(t // tm, nj),
        in_specs=_pm_specs(d, tm, row, mod_cols, n_first) + [
            pl.BlockSpec((None, d, tn), lambda i, j: (l, 0, j)),
            pl.BlockSpec((None, d, tn), lambda i, j: (l, 0, j + nj)),
        ],
        out_specs=pl.BlockSpec((tm, tn), lambda i, j: (i, j)),
        out_shape=jax.ShapeDtypeStruct((t, n), BF16),
        scratch_shapes=[pltpu.VMEM((tm, d), BF16)],
        compiler_params=_cparams("parallel", "arbitrary"),
        name="ffn_in",
    )(*xs, ada3, ada3, g.reshape(1, d), w, w)


def _mix_in(x, ada3, mod_cols, g, w_main, w_ab, l, *, layout):
    t, d = x.shape
    n_ctx, dec_seq = layout
    tm = _row_tile((n_ctx, dec_seq), 1024)
    row = _mod_row(n_ctx, dec_seq, tm)
    tn = 512
    return pl.pallas_call(
        _mix_in_kernel,
        grid=(t // tm, PROJ_MAIN // tn),
        in_specs=_pm_specs(d, tm, row, mod_cols) + [
            pl.BlockSpec((None, d, tn), lambda i, j: (l, 0, j)),
            pl.BlockSpec((None, d, LANE), lambda i, j: (l, 0, 0)),
        ],
        out_specs=[pl.BlockSpec((tm, tn), lambda i, j: (i, j)),
                   pl.BlockSpec((tm, LANE), lambda i, j: (i, 0))],
        out_shape=[jax.ShapeDtypeStruct((t, PROJ_MAIN), F32), jax.ShapeDtypeStruct((t, LANE), F32)],
        scratch_shapes=[pltpu.VMEM((tm, d), BF16)],
        compiler_params=_cparams("parallel", "arbitrary"),
        name="mix_in",
    )(x, ada3, ada3, g.reshape(1, d), w_main, w_ab)


def _ffn_out_kernel(*refs, n_first):
    a_ref, w_ref = refs[:2]
    x_refs = refs[2:-2]
    gate_ref, o_ref = refs[-2:]
    y = _dot(a_ref[...], w_ref[...])
    o_ref[...] = _read_x(x_refs, n_first, Ellipsis) + (0.5 * gate_ref[0]) * y


def _ffn_out(a, w, l, x, ada3, gate_col, *, layout):
    t, k = a.shape
    d = w.shape[2]
    n_ctx, dec_seq = layout
    tm = _row_tile((n_ctx, dec_seq), 1024)
    tn = min(512, d)
    xs, n_first, _ = _x_parts(x, tm)
    row = _mod_row(n_ctx, dec_seq, tm)
    gpb = d // tn
    return pl.pallas_call(
        functools.partial(_ffn_out_kernel, n_first=n_first),
        grid=(t // tm, d // tn),
        in_specs=[
            pl.BlockSpec((tm, k), lambda i, j: (i, 0)),
            pl.BlockSpec((None, k, tn), lambda i, j: (l, 0, j)),
            *_x_specs(n_first, tm, tn),
            pl.BlockSpec((1, 1, tn), lambda i, j: (row(i), 0, gate_col * gpb + j)),
        ],
        out_specs=pl.BlockSpec((tm, tn), lambda i, j: (i, j)),
        out_shape=jax.ShapeDtypeStruct((t, d), F32),
        compiler_params=_cparams("parallel", "parallel"),
        name="ffn_out",
    )(a, w, *xs, ada3)


def _mix_out_kernel(a1_ref, a2_ref, a3_ref, w_ref, x_ref, gate_ref, o_ref):
    y = (_dot(a1_ref[...], w_ref[0:GDN_W, :])
         + _dot(a2_ref[...], w_ref[GDN_W:GDN_W + DIFF_W, :])
         + _dot(a3_ref[...], w_ref[GDN_W + DIFF_W:MIX_W, :]))
    o_ref[...] = x_ref[...] + gate_ref[0] * y


def _mix_out(acts, w, l, x, ada3, gate_col, row0, *, layout):
    a1, a2, a3 = acts
    n = a1.shape[0]
    t, d = x.shape
    n_ctx, dec_seq = layout
    tm = _row_tile((n_ctx, dec_seq), 1024)
    tn = min(512, d)
    row = _mod_row(n_ctx, dec_seq, tm, row0)
    rb0 = row0 // tm
    gpb = d // tn
    return pl.pallas_call(
        _mix_out_kernel,
        grid=(n // tm, d // tn),
        in_specs=[
            pl.BlockSpec((tm, GDN_W), lambda i, j: (i, 0)),
            pl.BlockSpec((tm, DIFF_W), lambda i, j: (i, 0)),
            pl.BlockSpec((tm, POOL_W), lambda i, j: (i, 0)),
            pl.BlockSpec((None, MIX_W, tn), lambda i, j: (l, 0, j)),
            pl.BlockSpec((tm, tn), lambda i, j: (rb0 + i, j)),
            pl.BlockSpec((1, 1, tn), lambda i, j: (row(i), 0, gate_col * gpb + j)),
        ],
        out_specs=pl.BlockSpec((tm, tn), lambda i, j: (rb0 + i, j)),
        out_shape=jax.ShapeDtypeStruct((t, d), F32),
        input_output_aliases={4: 0},
        compiler_params=_cparams("parallel", "parallel"),
        name="mix_out",
    )(a1, a2, a3, w, x, ada3)


def _final_norm_kernel(x_ref, g_ref, o_ref):
    x = x_ref[...]
    o_ref[...] = x * lax.rsqrt(jnp.mean(x * x, axis=-1, keepdims=True) + EPS) * g_ref[...]


def _final_norm(x, g, row0, nrows):
    d = x.shape[1]
    tm = 256
    rb0 = row0 // tm
    return pl.pallas_call(
        _final_norm_kernel,
        grid=(nrows // tm,),
        in_specs=[pl.BlockSpec((tm, d), lambda i: (rb0 + i, 0)), pl.BlockSpec((1, d), lambda i: (0, 0))],
        out_specs=pl.BlockSpec((tm, d), lambda i: (i, 0)),
        out_shape=jax.ShapeDtypeStruct((nrows, d), F32),
        compiler_params=_cparams("parallel"),
        name="final_norm",
    )(x, g.reshape(1, d))


def _shift_rows(x, off, row):
    if off == 0:
        return x
    n = x.shape[0]
    y = pltpu.roll(x, (-off) % n, 0)
    ok = (row + off >= 0) & (row + off < n)
    return jnp.where(ok, y, 0.0)


def _gdn_prep_kernel(x_ref, w_ref, o_ref):
    n = x_ref.shape[0]
    left = CONV_K // 2
    edge = 16
    row = lax.broadcasted_iota(jnp.int32, (edge, LANE), 0)
    kind = pl.program_id(1)
    for h in range(GDN_HEADS):
        cols = slice(h * LANE, (h + 1) * LANE)
        x = x_ref[:, cols]

        def conv(xb, shift):
            acc = jnp.zeros_like(xb)
            for j in range(CONV_K):
                acc = acc + shift(xb, j - left) * w_ref[j:j + 1, cols]
            return acc

        acc = conv(x, lambda xb, off: xb if off == 0 else pltpu.roll(xb, (-off) % n, 0))
        head = conv(x[0:edge], lambda xb, off: _shift_rows(xb, off, row))
        tail = conv(x[n - edge:n], lambda xb, off: _shift_rows(xb, off, row))
        acc = jnp.concatenate([head[0:8], acc[8:n - 8], tail[edge - 8:edge]], axis=0)
        y = _silu(acc)
        inv = lax.rsqrt(jnp.sum(y * y, axis=-1, keepdims=True) + EPS)
        inv = inv * jnp.where(kind == 0, GDN_DK ** -0.5, 1.0)
        o_ref[:, cols] = jnp.where(kind < 2, y * inv, y)


def _gdn_prep(proj, conv_w, row0, nseq, seqlen):
    blk0 = row0 // seqlen
    return pl.pallas_call(
        _gdn_prep_kernel,
        grid=(nseq, 3),
        in_specs=[
            pl.BlockSpec((seqlen, GDN_W), lambda s, c: (blk0 + s, CB_Q // GDN_HEADS + c)),
            pl.BlockSpec((CONV_K, GDN_W), lambda s, c: (0, c)),
        ],
        out_specs=pl.BlockSpec((seqlen, GDN_W), lambda s, c: (s, c)),
        out_shape=jax.ShapeDtypeStruct((nseq * seqlen, 3 * GDN_W), F32),
        compiler_params=_cparams("parallel", "parallel"),
        name="gdn_prep",
    )(proj, conv_w)


def _unit_tri_solve_many(lmats, rhss, eye):
    xs = [eye - l for l in lmats]
    ps = [_mm(l, l) for l in lmats]
    steps = int(math.log2(CHUNK)) - 1
    for s in range(steps):
        xs = [x + _mm(x, p) for x, p in zip(xs, ps)]
        if s + 1 < steps:
            ps = [_mm(p, p) for p in ps]
    x0s = [_mm(x, r) for x, r in zip(xs, rhss)]
    resids = [r - (x0 + _mm(l, x0)) for l, r, x0 in zip(lmats, rhss, x0s)]
    return [x0 + _mm(x, rs) for x, x0, rs in zip(xs, x0s, resids)]


def _gdn_intra_kernel(q_ref, k_ref, v_ref, ab_ref, alog_ref, dt_ref,
                      u_ref, w_ref, qg_ref, kd_ref, a_ref, gl_ref):
    n = GDN_BLK
    ab = ab_ref[...]
    lane = lax.broadcasted_iota(jnp.int32, ab.shape, 1)
    sp = jnp.maximum(ab + dt_ref[...], 0.0) + jnp.log(1.0 + jnp.exp(-jnp.abs(ab + dt_ref[...])))
    gmat = -jnp.exp(alog_ref[...]) * sp
    bmat = _sigmoid(ab)

    ri = lax.broadcasted_iota(jnp.int32, (n, n), 0)
    ci = lax.broadcasted_iota(jnp.int32, (n, n), 1)
    same = (ri // CHUNK) == (ci // CHUNK)
    eye = (ri == ci).astype(F32)
    r8 = lax.broadcasted_iota(jnp.int32, (8, LANE), 0)
    gtot_all = _dot_exact_lhs(same.astype(BF16), gmat)
    gcf_all = _dot_exact_lhs((same & (ci <= ri)).astype(BF16), gmat)
    gc_dirs = (gcf_all, (gtot_all - gcf_all) + gmat)
    incl_dirs = (same & (ci <= ri), same & (ci >= ri))
    strict_dirs = (same & (ci < ri), same & (ci > ri))

    lmats, rhss, dests = [], [], []
    for h in range(GDN_HEADS):
        cols = slice(h * LANE, (h + 1) * LANE)
        q = q_ref[:, cols]
        k = k_ref[:, cols]
        v = v_ref[:, cols]
        kk = _mm(k, k, _dot_nt)
        qk = _mm(q, k, _dot_nt)
        for d in range(2):
            incl = incl_dirs[d]
            sel_g = lane == d * GDN_HEADS + h
            sel_b = lane == 2 * GDN_HEADS + d * GDN_HEADS + h
            gc = jnp.sum(jnp.where(sel_g, gc_dirs[d], 0.0), axis=1, keepdims=True)
            gtot = jnp.sum(jnp.where(sel_g, gtot_all, 0.0), axis=1, keepdims=True)
            beta = jnp.sum(jnp.where(sel_b, bmat, 0.0), axis=1, keepdims=True)
            gc_b = jnp.broadcast_to(gc, (n, LANE))
            gc_row = jnp.transpose(gc_b)[0:1, :]
            diff = jnp.where(incl, gc - gc_row, 0.0)
            decay = jnp.where(incl, jnp.exp(diff), 0.0)
            eg = jnp.exp(gc_b)
            lmats.append(jnp.where(strict_dirs[d], kk * beta * decay, 0.0))
            rhss.append(jnp.concatenate([v * beta, (k * beta) * eg], axis=1))
            dests.append((d, cols))
            qg_ref[d, :, cols] = (q * eg).astype(qg_ref.dtype)
            kd_ref[d, :, cols] = (k * jnp.exp(jnp.broadcast_to(gtot - gc, (n, LANE)))).astype(kd_ref.dtype)
            amat = qk * decay
            a_ref[d, h] = ((amat[:, 0:CHUNK] + amat[:, CHUNK:2 * CHUNK]) + (
                amat[:, 2 * CHUNK:3 * CHUNK] + amat[:, 3 * CHUNK:4 * CHUNK])).astype(a_ref.dtype)
            egt = jnp.exp(jnp.broadcast_to(gtot, (n, LANE)))
            gl8 = jnp.zeros((8, LANE), F32)
            for c in range(n // CHUNK):
                gl8 = jnp.where(r8 == c, egt[c * CHUNK:c * CHUNK + 8, :], gl8)
            gl_ref[d, :, cols] = gl8

    for (d, cols), uw in zip(dests, _unit_tri_solve_many(lmats, rhss, eye)):
        u_ref[d, :, cols] = uw[:, :GDN_DV]
        w_ref[d, :, cols] = uw[:, GDN_DV:].astype(w_ref.dtype)


def _gdn_intra(qkv, proj_ab, alog_row, dt_row, row0):
    t = qkv.shape[0]
    nb = t // GDN_BLK
    pb0 = row0 // GDN_BLK
    big = lambda dt: jax.ShapeDtypeStruct((2, t, GDN_W), dt)
    big_spec = pl.BlockSpec((2, GDN_BLK, GDN_W), lambda i: (0, i, 0))
    return pl.pallas_call(
        _gdn_intra_kernel,
        grid=(nb,),
        in_specs=[
            pl.BlockSpec((GDN_BLK, GDN_W), lambda i: (i, 0)),
            pl.BlockSpec((GDN_BLK, GDN_W), lambda i: (i, 1)),
            pl.BlockSpec((GDN_BLK, GDN_W), lambda i: (i, 2)),
            pl.BlockSpec((GDN_BLK, LANE), lambda i: (pb0 + i, 0)),
            pl.BlockSpec((1, LANE), lambda i: (0, 0)),
            pl.BlockSpec((1, LANE), lambda i: (0, 0)),
        ],
        out_specs=[big_spec, big_spec, big_spec, big_spec,
                   pl.BlockSpec((2, GDN_HEADS, GDN_BLK, CHUNK), lambda i: (0, 0, i, 0)),
                   pl.BlockSpec((2, 8, GDN_W), lambda i: (0, i, 0))],
        out_shape=[big(F32), big(BF16), big(BF16), big(BF16),
                   jax.ShapeDtypeStruct((2, GDN_HEADS, t, CHUNK), BF16),
                   jax.ShapeDtypeStruct((2, nb * 8, GDN_W), F32)],
        compiler_params=_cparams("parallel"),
        name="gdn_intra",
    )(qkv, qkv, qkv, proj_ab, alog_row, dt_row)


def _gdn_scan_kernel(*refs, has_init):
    if has_init:
        s0_ref, refs = refs[0], refs[1:]
    ins = (refs[0:6], refs[6:12])
    o_refs, sf_ref, s_ref = refs[12:14], refs[14], refs[15]
    blk = pl.program_id(1)
    nchunk = GDN_BLK // CHUNK

    @pl.when(blk == 0)
    def _():
        if has_init:
            s_ref[...] = s0_ref[...]
        else:
            s_ref[...] = jnp.zeros_like(s_ref)

    U, W, QG, KD, A, GL = range(6)
    chains = [(d, h) for d in range(2) for h in range(GDN_HEADS)]
    states = [s_ref[d, h] for d, h in chains]
    for c in range(nchunk):
        ccs = [c if d == 0 else nchunk - 1 - c for d, _ in chains]
        rows = [slice(cc * CHUNK, (cc + 1) * CHUNK) for cc in ccs]
        cols = [slice(h * LANE, (h + 1) * LANE) for _, h in chains]
        n = range(len(chains))
        ws = [_mm(ins[chains[i][0]][W][rows[i], cols[i]], states[i]) for i in n]
        qs = [_mm(ins[chains[i][0]][QG][rows[i], cols[i]], states[i]) for i in n]
        v_news = [ins[chains[i][0]][U][rows[i], cols[i]] - ws[i] for i in n]
        for i in n:
            d, h = chains[i]
            o_refs[d][rows[i], cols[i]] = qs[i] + _mm(ins[d][A][h, rows[i], :], v_news[i])
        states = [states[i] * ins[chains[i][0]][GL][ccs[i]:ccs[i] + 1, cols[i]]
                  + _mm(ins[chains[i][0]][KD][rows[i], cols[i]], v_news[i], _dot_tn) for i in n]
    for (d, h), s in zip(chains, states):
        s_ref[d, h] = s

    @pl.when(blk == pl.num_programs(1) - 1)
    def _():
        sf_ref[...] = s_ref[...]


def _gdn_scan(s0, u, w, qg, kd, a, gl, nseq, seqlen):
    t = u.shape[1]
    nblk = seqlen // GDN_BLK
    in_specs, args = [], []
    for d in range(2):
        rb = (lambda s, b: s * nblk + b) if d == 0 else (lambda s, b: s * nblk + nblk - 1 - b)
        big = pl.BlockSpec((None, GDN_BLK, GDN_W), lambda s, b, d=d, rb=rb: (d, rb(s, b), 0))
        in_specs += [big, big, big, big,
                     pl.BlockSpec((None, GDN_HEADS, GDN_BLK, CHUNK), lambda s, b, d=d, rb=rb: (d, 0, rb(s, b), 0)),
                     pl.BlockSpec((None, 8, GDN_W), lambda s, b, d=d, rb=rb: (d, rb(s, b), 0))]
        args += [u, w, qg, kd, a, gl]
    state_spec = pl.BlockSpec((None, 2, GDN_HEADS, GDN_DK, GDN_DV), lambda s, b: (s, 0, 0, 0, 0))
    if s0 is not None:
        in_specs.insert(0, state_spec)
        args.insert(0, s0)
    out_specs = [pl.BlockSpec((GDN_BLK, GDN_W), lambda s, b: (s * nblk + b, 0)),
                 pl.BlockSpec((GDN_BLK, GDN_W), lambda s, b: (s * nblk + nblk - 1 - b, 0)),
                 state_spec]
    return pl.pallas_call(
        functools.partial(_gdn_scan_kernel, has_init=s0 is not None),
        grid=(nseq, nblk),
        in_specs=in_specs,
        out_specs=out_specs,
        out_shape=[jax.ShapeDtypeStruct((t, GDN_W), F32), jax.ShapeDtypeStruct((t, GDN_W), F32),
                   jax.ShapeDtypeStruct((nseq, 2, GDN_HEADS, GDN_DK, GDN_DV), F32)],
        scratch_shapes=[pltpu.VMEM((2, GDN_HEADS, GDN_DK, GDN_DV), F32)],
        compiler_params=_cparams("parallel", "arbitrary"),
        name="gdn_scan",
    )(*args)


def _gdn_post_kernel(of_ref, ob_ref, z_ref, g_ref, out_ref):
    for h in range(GDN_HEADS):
        cols = slice(h * LANE, (h + 1) * LANE)
        o = of_ref[:, cols] + ob_ref[:, cols]
        y = o * lax.rsqrt(jnp.mean(o * o, axis=-1, keepdims=True) + EPS) * g_ref[...]
        out_ref[:, cols] = (y * _silu(z_ref[:, cols])).astype(out_ref.dtype)


def _gdn_post(o_f, o_b, proj, g, row0):
    t = o_f.shape[0]
    tm = _row_tile((t, row0 + t), 1024)
    pb0 = row0 // tm
    return pl.pallas_call(
        _gdn_post_kernel,
        grid=(t // tm,),
        in_specs=[
            pl.BlockSpec((tm, GDN_W), lambda i: (i, 0)),
            pl.BlockSpec((tm, GDN_W), lambda i: (i, 0)),
            pl.BlockSpec((tm, GDN_W), lambda i: (pb0 + i, CB_Z // GDN_HEADS)),
            pl.BlockSpec((1, GDN_DV), lambda i: (0, 0)),
        ],
        out_specs=pl.BlockSpec((tm, GDN_W), lambda i: (i, 0)),
        out_shape=jax.ShapeDtypeStruct((t, GDN_W), BF16),
        compiler_params=_cparams("parallel"),
        name="gdn_post",
    )(o_f, o_b, proj, g.reshape(1, GDN_DV))


def _rope_rotate(x, cos, sin_signed):
    lane = lax.broadcasted_iota(jnp.int32, x.shape, 1)
    first = (lane % 32) < 16
    partner = jnp.where(first, pltpu.roll(x, LANE - 16, 1), pltpu.roll(x, 16, 1))
    return x * cos + partner * sin_signed


def _rope_tables(seqlen):
    pos = jnp.arange(seqlen)
    rowp = (pos // GRID_W).astype(F32)
    colp = (pos % GRID_W).astype(F32)
    half = DIFF_DQK // 2
    inv_freq = ROPE_BASE ** (-jnp.arange(0, half, 2, dtype=F32) / half)
    c = jnp.arange(LANE)
    c64 = c % DIFF_DQK
    p = jnp.where((c64 // half)[None, :] == 0, rowp[:, None], colp[:, None])
    ang = p * inv_freq[c64 % (half // 2)][None, :]
    sign = jnp.where((c64 % half) < half // 2, -1.0, 1.0).astype(F32)
    return jnp.cos(ang), jnp.sin(ang) * sign[None, :]


def _diff_attn_kernel(*refs, seg_lens, lam_init, rope, hp, n_alias, emit_cache):
    nseg = len(seg_lens)
    q_ref = refs[0]
    k_refs = refs[1:1 + nseg]
    v_refs = refs[1 + nseg:1 + 2 * nseg]
    pos = 1 + 2 * nseg
    if rope:
        cosq_ref, sinq_ref, cosk_ref, sink_ref = refs[pos:pos + 4]
        pos += 4
    lam_ref, g_ref = refs[pos:pos + 2]
    pos += 2 + n_alias
    o_ref = refs[pos]
    if emit_cache:
        kc_ref, vc_ref = refs[pos + 1:pos + 3]
    kb_ref, vb_ref = refs[-2:]
    offs = [sum(seg_lens[:s]) for s in range(nseg)]
    hcols = [slice(hh * DIFF_DV, (hh + 1) * DIFF_DV) for hh in range(hp)]

    @pl.when(pl.program_id(2) == 0)
    def _():
        if emit_cache:
            kc_ref[...] = k_refs[-1][...]
            vc_ref[...] = v_refs[-1][...]
        for hh in range(hp):
            vb_ref[:, (2 * hh + 1) * DIFF_DV:(2 * hh + 2) * DIFF_DV] = jnp.ones((vb_ref.shape[0], DIFF_DV), BF16)
            for s in range(nseg):
                rows = slice(offs[s], offs[s] + seg_lens[s])
                k = k_refs[s][:, hcols[hh]]
                if rope and s == nseg - 1:
                    k = _rope_rotate(k, cosk_ref[...], sink_ref[...])
                kb_ref[rows, hcols[hh]] = k.astype(BF16)
                vb_ref[rows, 2 * hh * DIFF_DV:(2 * hh + 1) * DIFF_DV] = v_refs[s][:, hcols[hh]].astype(BF16)

    lv = lam_ref[...]
    lam = (jnp.exp(jnp.sum(lv[0:1] * lv[1:2], axis=1, keepdims=True))
           - jnp.exp(jnp.sum(lv[2:3] * lv[3:4], axis=1, keepdims=True)) + lam_init)
    lane = lax.broadcasted_iota(jnp.int32, (ATTN_SUB, DIFF_DV), 1)

    def scores(hh, rows):
        q = q_ref[rows, hcols[hh]]
        if rope:
            q = _rope_rotate(q, cosq_ref[rows, :], sinq_ref[rows, :])
        q = q * (DIFF_DQK ** -0.5)
        kb = kb_ref[:, hcols[hh]]
        return [_dot_nt(jnp.where((lane // DIFF_DQK) == m, q, 0.0).astype(BF16), kb) for m in range(2)]

    units = [(hh, slice(r * ATTN_SUB, (r + 1) * ATTN_SUB))
             for hh in range(hp) for r in range(q_ref.shape[0] // ATTN_SUB)]
    ss_next = scores(*units[0])
    for n, (hh, rows) in enumerate(units):
        ss = ss_next
        if n + 1 < len(units):
            ss_next = scores(*units[n + 1])
        vb = vb_ref[:, 2 * hh * DIFF_DV:(2 * hh + 2) * DIFF_DV]
        es = [jnp.exp(s - jnp.max(s, axis=1, keepdims=True)).astype(BF16) for s in ss]
        accs = [_dot(e, vb) for e in es]
        outs = [acc[:, 0:DIFF_DV] / acc[:, DIFF_DV:] for acc in accs]
        o = outs[0] - lam * outs[1]
        y = o * lax.rsqrt(jnp.mean(o * o, axis=-1, keepdims=True) + EPS) * g_ref[...]
        o_ref[rows, hcols[hh]] = (y * (1.0 - lam_init)).astype(o_ref.dtype)


def _diff_attn(proj, row0, nseq, seqlen, past_kv, l, lam_vec, g, cache_out):
    tq = min(4 * ATTN_SUB, seqlen)
    hp = min(DIFF_HEADS, max(1, ATTN_STEP_ROWS // seqlen))
    nqb = seqlen // tq
    qb0 = row0 // tq
    sb0 = row0 // seqlen
    wcols = hp * DIFF_DV
    q_spec = pl.BlockSpec((tq, wcols), lambda b, h, i: (qb0 + b * nqb + i, CB_DQ // hp + h))
    k_new = pl.BlockSpec((seqlen, wcols), lambda b, h, i: (sb0 + b, CB_DK // hp + h))
    v_new = pl.BlockSpec((seqlen, wcols), lambda b, h, i: (sb0 + b, CB_DV // hp + h))
    arrs, specs, seg_lens = [proj], [q_spec], [seqlen]
    if past_kv is None:
        arrs += [proj, proj]
        specs += [k_new, v_new]
    else:
        past = past_kv[0].shape[2]
        cached = pl.BlockSpec((None, None, past, wcols), lambda b, h, i: (b, l, 0, h))
        arrs += [past_kv[0], proj, past_kv[1], proj]
        specs += [cached, k_new, cached, v_new]
        seg_lens = [past, seqlen]
        cos, sin = _rope_tables(seqlen)
        arrs += [cos, sin, cos, sin]
        specs += [pl.BlockSpec((tq, DIFF_DV), lambda b, h, i: (i, 0))] * 2 + [
            pl.BlockSpec((seqlen, DIFF_DV), lambda b, h, i: (0, 0))] * 2
    arrs += [lam_vec, g.reshape(1, DIFF_DV)]
    specs += [pl.BlockSpec((4, DIFF_DQK), lambda b, h, i: (0, 0)),
              pl.BlockSpec((1, DIFF_DV), lambda b, h, i: (0, 0))]
    out_specs = [pl.BlockSpec((tq, wcols), lambda b, h, i: (b * nqb + i, h))]
    out_shape = [jax.ShapeDtypeStruct((nseq * seqlen, DIFF_W), BF16)]
    aliases, n_alias = {}, 0
    if cache_out is not None:
        depth, prev = cache_out
        entry = pl.BlockSpec((None, None, seqlen, wcols), lambda b, h, i: (b, l, 0, h))
        out_specs += [entry, entry]
        out_shape += [jax.ShapeDtypeStruct((nseq, depth, seqlen, DIFF_W), F32)] * 2
        if prev is not None:
            aliases = {len(arrs): 1, len(arrs) + 1: 2}
            n_alias = 2
            arrs += list(prev)
            specs += [pl.BlockSpec(memory_space=pl.ANY)] * 2
    lk = sum(seg_lens)
    outs = pl.pallas_call(
        functools.partial(_diff_attn_kernel, seg_lens=tuple(seg_lens), lam_init=_lambda_init(l),
                          rope=past_kv is not None, hp=hp, n_alias=n_alias, emit_cache=cache_out is not None),
        grid=(nseq, DIFF_HEADS // hp, nqb),
        in_specs=specs,
        out_specs=out_specs,
        out_shape=out_shape,
        input_output_aliases=aliases,
        scratch_shapes=[pltpu.VMEM((lk, wcols), BF16), pltpu.VMEM((lk, 2 * wcols), BF16)],
        compiler_params=_cparams("parallel", "parallel", "arbitrary"),
        name="diff_attn",
    )(*arrs)
    return outs[0], (tuple(outs[1:]) if cache_out is not None else None)


def _pool_kernel(x_ref, w_ref, sc_ref, o_ref):
    n = x_ref.shape[0]
    row = lax.broadcasted_iota(jnp.int32, (n, POOL_GW), 0)
    for gi, win in enumerate(POOL_WINDOWS):
        cols = slice(gi * POOL_GW, (gi + 1) * POOL_GW)
        x = x_ref[:, cols]
        a = win // 2
        b = win - a - 1
        acc = x
        for off in range(-a, b + 1):
            if off != 0:
                acc = acc + _shift_rows(x, off, row)
        cnt = (jnp.minimum(row + b + 1, n) - jnp.maximum(row - a, 0)).astype(F32)
        pooled = acc / cnt - x
        y = _dot(pooled.astype(BF16), w_ref[gi].astype(BF16))
        o_ref[:, cols] = (y * sc_ref[:, cols]).astype(o_ref.dtype)


def _pool(proj, pool_w, pool_scale, row0, nseq, seqlen):
    blk0 = row0 // seqlen
    return pl.pallas_call(
        _pool_kernel,
        grid=(nseq,),
        in_specs=[
            pl.BlockSpec((seqlen, POOL_W), lambda s: (blk0 + s, CB_PIN // len(POOL_WINDOWS))),
            pl.BlockSpec((len(POOL_WINDOWS), POOL_GW, POOL_GW), lambda s: (0, 0, 0)),
            pl.BlockSpec((1, POOL_W), lambda s: (0, 0)),
        ],
        out_specs=pl.BlockSpec((seqlen, POOL_W), lambda s: (s, 0)),
        out_shape=jax.ShapeDtypeStruct((nseq * seqlen, POOL_W), BF16),
        compiler_params=_cparams("parallel"),
        name="pool",
    )(proj, pool_w, pool_scale.reshape(1, POOL_W))


def _mix_group(proj, proj_ab, lp, l, row0, nseq, seqlen, ctx, cache_out):
    qkv = _gdn_prep(proj, lp["gdn_conv"], row0, nseq, seqlen)
    u, w, qg, kd, a, gl = _gdn_intra(qkv, proj_ab, lp["alog_row"], lp["dt_row"], row0)
    o_f, o_b, s_fin = _gdn_scan(None if ctx is None else ctx[2], u, w, qg, kd, a, gl, nseq, seqlen)
    o_gdn = _gdn_post(o_f, o_b, proj, lp["gdn_norm"], row0)

    o_diff, caches = _diff_attn(proj, row0, nseq, seqlen, None if ctx is None else ctx[:2], l,
                                lp["diff_lam"], lp["diff_norm"], cache_out)
    o_pool = _pool(proj, lp["pool_w"], lp["pool_scale"], row0, nseq, seqlen)
    return (o_gdn, o_diff, o_pool), s_fin, caches


def _split_w_in(w_in):
    depth, d, _ = w_in.shape
    o_ab = 4 * GDN_W
    o_dq = o_ab + 4 * GDN_HEADS
    pad = jnp.zeros((depth, d, LANE - 4 * GDN_HEADS), w_in.dtype)
    main = jnp.concatenate([w_in[..., :o_ab], w_in[..., o_dq:]], axis=-1)
    ab = jnp.concatenate([w_in[..., o_ab:o_dq], pad], axis=-1)
    return main.astype(BF16), ab.astype(BF16)


def kernel(x_prompt, x_sample, c, cache_k, cache_v, state_gdn, c_ctx, w_ada, b_ada, norm_ffn1, ffn1_in,
           ffn1_out, norm_mix, w_in, gdn_conv, gdn_a_log, gdn_dt_bias, gdn_norm, diff_lam, diff_norm,
           pool_w, pool_scale, w_out, norm_ffn2, ffn2_in, ffn2_out, final_norm):
    batch, seq, d = x_prompt.shape
    dec_batch, dec_seq, _ = x_sample.shape
    depth = w_ada.shape[0]
    n_ctx = batch * seq
    n_dec = dec_batch * dec_seq
    layout = (n_ctx, dec_seq)
    past = cache_k.shape[2]

    x = (x_prompt.reshape(n_ctx, d), x_sample.reshape(n_dec, d))
    cond8 = jnp.concatenate([c_ctx[None, :], c, jnp.zeros((8 - 1 - dec_batch, d), F32)], axis=0)
    ada = _ada(cond8, w_ada, b_ada)
    cache_k4 = cache_k.reshape(dec_batch, depth, past, DIFF_W)
    cache_v4 = cache_v.reshape(dec_batch, depth, past, DIFF_W)
    w_in_main, w_in_ab = _split_w_in(w_in)
    w_out_b = w_out.astype(BF16)
    ffn_w = [(ffn1_in, ffn1_out.astype(BF16)), (ffn2_in, ffn2_out.astype(BF16))]

    caches, ss = None, []
    for l in range(depth):
        ada3 = ada[l, :1 + dec_batch].reshape(1 + dec_batch, 1, N_MOD * d)
        lane_pad = jnp.zeros((LANE - 2 * GDN_HEADS,), F32)
        lp = dict(
            gdn_conv=gdn_conv[l], gdn_norm=gdn_norm[l], diff_lam=diff_lam[l], diff_norm=diff_norm[l],
            pool_w=pool_w[l], pool_scale=pool_scale[l],
            alog_row=jnp.concatenate([gdn_a_log[l].reshape(-1), lane_pad]).reshape(1, LANE),
            dt_row=jnp.concatenate([gdn_dt_bias[l].reshape(-1), lane_pad]).reshape(1, LANE),
        )
        act = _ffn_in(x, ada3, (0, 1), norm_ffn1[l], ffn_w[0][0], l, layout=layout)
        x = _ffn_out(act, ffn_w[0][1], l, x, ada3, 2, layout=layout)

        proj, proj_ab = _mix_in(x, ada3, (3, 4), norm_mix[l], w_in_main, w_in_ab, l, layout=layout)
        acts_c, s_c, caches = _mix_group(proj, proj_ab, lp, l, 0, batch, seq, None, (depth, caches))
        acts_d, _, _ = _mix_group(proj, proj_ab, lp, l, n_ctx, dec_batch, dec_seq,
                                  (cache_k4, cache_v4, state_gdn[:, l]), None)
        x = _mix_out(acts_c, w_out_b, l, x, ada3, 5, 0, layout=layout)
        x = _mix_out(acts_d, w_out_b, l, x, ada3, 5, n_ctx, layout=layout)
        ss.append(s_c)

        act = _ffn_in(x, ada3, (6, 7), norm_ffn2[l], ffn_w[1][0], l, layout=layout)
        x = _ffn_out(act, ffn_w[1][1], l, x, ada3, 8, layout=layout)

    y_prompt = _final_norm(x, final_norm, 0, n_ctx).reshape(batch, seq, d)
    y_sample = _final_norm(x, final_norm, n_ctx, n_dec).reshape(dec_batch, dec_seq, d)
    new_k, new_v = (t.reshape(batch, depth, seq, DIFF_HEADS, DIFF_DV) for t in caches)
    return (y_prompt, y_sample, new_k, new_v, jnp.stack(ss, axis=1))
```

```python
import functools
import math

import jax
import jax.numpy as jnp
from jax import lax
from jax.experimental import pallas as pl
from jax.experimental.pallas import tpu as pltpu

F32 = jnp.float32
BF16 = jnp.bfloat16

GRID_W = 64
GDN_HEADS = 4
GDN_DK = 128
GDN_DV = 128
GDN_W = GDN_HEADS * GDN_DV
CONV_K = 4
CHUNK = 64
DIFF_HEADS = 8
DIFF_DQK = 64
DIFF_DV = 2 * DIFF_DQK
DIFF_W = DIFF_HEADS * DIFF_DV
ROPE_BASE = 10000.0
POOL_WINDOWS = (2, 4, 8, 16)
POOL_GW = 128
POOL_W = len(POOL_WINDOWS) * POOL_GW
MIX_W = GDN_W + DIFF_W + POOL_W
N_MOD = 9
EPS = 1e-6

LANE = 128
GDN_BLK = 4 * CHUNK
ATTN_SUB = 256
ATTN_STEP_ROWS = 2048
PRO_SUB = 256
PREP_STEP_ROWS = 1024
VMEM_LIMIT = 60 * 1024 * 1024

CB_Q, CB_K, CB_V, CB_Z = 0, 4, 8, 12
CB_DQ, CB_DK, CB_DV, CB_PIN = 16, 24, 32, 40
PROJ_MAIN = 44 * LANE


def _lambda_init(l):
    return 0.8 - 0.6 * math.exp(-0.3 * l)


def _cparams(*sem):
    return pltpu.CompilerParams(dimension_semantics=sem, vmem_limit_bytes=VMEM_LIMIT)


def _dot(a, b):
    return lax.dot_general(a, b, (((1,), (0,)), ((), ())), preferred_element_type=F32)


def _dot_nt(a, b):
    return lax.dot_general(a, b, (((1,), (1,)), ((), ())), preferred_element_type=F32)


def _dot_tn(a, b):
    return lax.dot_general(a, b, (((0,), (0,)), ((), ())), preferred_element_type=F32)


def _mm(a, b, dot=_dot):
    return dot(a.astype(BF16), b.astype(BF16))


def _dot_exact_lhs(m, b):
    b0 = b.astype(BF16)
    r = b - b0.astype(F32)
    b1 = r.astype(BF16)
    b2 = (r - b1.astype(F32)).astype(BF16)
    return _dot(m, b0) + (_dot(m, b1) + _dot(m, b2))


def _sigmoid(x):
    return 0.5 * (1.0 + jnp.tanh(0.5 * x))


def _silu(x):
    return x * _sigmoid(x)


def _ada_kernel(cond_ref, w_ref, b_ref, o_ref):
    s = _silu(cond_ref[...]).astype(BF16)
    o_ref[...] = _dot(s, w_ref[...].astype(BF16)) + b_ref[...]


def _ada(cond8, w_ada, b_ada):
    depth, d, n = w_ada.shape
    tn = 1024
    return pl.pallas_call(
        _ada_kernel,
        grid=(depth, n // tn),
        in_specs=[
            pl.BlockSpec((8, d), lambda l, j: (0, 0)),
            pl.BlockSpec((None, d, tn), lambda l, j: (l, 0, j)),
            pl.BlockSpec((None, 1, tn), lambda l, j: (l, 0, j)),
        ],
        out_specs=pl.BlockSpec((None, 8, tn), lambda l, j: (l, 0, j)),
        out_shape=jax.ShapeDtypeStruct((depth, 8, n), F32),
        compiler_params=_cparams("parallel", "parallel"),
        name="ada",
    )(cond8, w_ada, b_ada.reshape(depth, 1, n))


def _norm_mod(x, g, shift, scale):
    y = x * lax.rsqrt(jnp.mean(x * x, axis=-1, keepdims=True) + EPS) * g
    return y * (1.0 + scale) + shift


def _read_x(x_refs, n_first, idx):
    if len(x_refs) == 1:
        return x_refs[0][idx]
    return jnp.where(pl.program_id(0) < n_first, x_refs[0][idx], x_refs[1][idx])


def _norm_mod_rows(x_refs, n_first, sh_ref, sc_ref, g_ref, rows):
    x = _read_x(x_refs, n_first, (rows, slice(None)))
    return _norm_mod(x, g_ref[...], sh_ref[0], sc_ref[0]).astype(BF16)


def _row_subs(n):
    return [slice(r * PRO_SUB, (r + 1) * PRO_SUB) for r in range(n // PRO_SUB)]


def _mix_in_kernel(x_ref, sh_ref, sc_ref, g_ref, w_ref, wab_ref, o_ref, ab_ref, h_ref):
    @pl.when(pl.program_id(1) == 0)
    def _():
        for rows in _row_subs(x_ref.shape[0]):
            h = _norm_mod_rows((x_ref,), None, sh_ref, sc_ref, g_ref, rows)
            h_ref[rows, :] = h
            ab_ref[rows, :] = _dot(h, wab_ref[...])
            o_ref[rows, :] = _dot(h, w_ref[...])

    @pl.when(pl.program_id(1) != 0)
    def _():
        o_ref[...] = _dot(h_ref[...], w_ref[...])


def _ffn_in_kernel(*refs, n_first):
    x_refs = refs[:-7]
    sh_ref, sc_ref, g_ref, wg_ref, wu_ref, o_ref, h_ref = refs[-7:]

    def swiglu(h):
        gate = _dot(h, wg_ref[...].astype(BF16))
        up = _dot(h, wu_ref[...].astype(BF16))
        return (_silu(gate) * up).astype(o_ref.dtype)

    @pl.when(pl.program_id(1) == 0)
    def _():
        for rows in _row_subs(h_ref.shape[0]):
            h = _norm_mod_rows(x_refs, n_first, sh_ref, sc_ref, g_ref, rows)
            h_ref[rows, :] = h
            o_ref[rows, :] = swiglu(h)

    @pl.when(pl.program_id(1) != 0)
    def _():
        o_ref[...] = swiglu(h_ref[...])


def _row_tile(groups, cap):
    tm = cap
    while any(g % tm for g in groups):
        tm //= 2
    return tm


def _mod_row(n_ctx, dec_seq, tm, row0=0):
    def f(i):
        r = row0 + i * tm
        return jnp.where(r < n_ctx, 0, 1 + (r - n_ctx) // dec_seq)
    return f


def _x_parts(x, tm):
    if isinstance(x, tuple):
        return list(x), x[0].shape[0] // tm, sum(p.shape[0] for p in x)
    return [x], None, x.shape[0]


def _x_specs(n_first, tm, tn):
    if n_first is None:
        return [pl.BlockSpec((tm, tn), lambda i, j: (i, j))]
    return [pl.BlockSpec((tm, tn), lambda i, j: (jnp.where(i < n_first, i, n_first - 1), jnp.where(i < n_first, j, 0))),
            pl.BlockSpec((tm, tn), lambda i, j: (jnp.where(i < n_first, 0, i - n_first), jnp.where(i < n_first, 0, j)))]


def _pm_specs(d, tm, row, mod_cols, n_first=None):
    sh_c, sc_c = mod_cols
    x_specs = [pl.BlockSpec((tm, d), lambda i, j: (i, 0))] if n_first is None else [
        pl.BlockSpec((tm, d), lambda i, j: (jnp.minimum(i, n_first - 1), 0)),
        pl.BlockSpec((tm, d), lambda i, j: (jnp.maximum(i - n_first, 0), 0))]
    return x_specs + [
        pl.BlockSpec((1, 1, d), lambda i, j: (row(i), 0, sh_c)),
        pl.BlockSpec((1, 1, d), lambda i, j: (row(i), 0, sc_c)),
        pl.BlockSpec((1, d), lambda i, j: (0, 0)),
    ]


def _ffn_in(x, ada3, mod_cols, g, w, l, *, layout):
    d = w.shape[1]
    n_ctx, dec_seq = layout
    tm = _row_tile((n_ctx, dec_seq), 1024)
    xs, n_first, t = _x_parts(x, tm)
    row = _mod_row(n_ctx, dec_seq, tm)
    n = w.shape[2] // 2
    tn = 512 if n % 512 == 0 else n
    nj = n // tn
    return pl.pallas_call(
        functools.partial(_ffn_in_kernel, n_first=n_first),
        grid=(t // tm, nj),
        in_specs=_pm_specs(d, tm, row, mod_cols, n_first) + [
            pl.BlockSpec((None, d, tn), lambda i, j: (l, 0, j)),
            pl.BlockSpec((None, d, tn), lambda i, j: (l, 0, j + nj)),
        ],
        out_specs=pl.BlockSpec((tm, tn), lambda i, j: (i, j)),
        out_shape=jax.ShapeDtypeStruct((t, n), BF16),
        scratch_shapes=[pltpu.VMEM((tm, d), BF16)],
        compiler_params=_cparams("parallel", "arbitrary"),
        name="ffn_in",
    )(*xs, ada3, ada3, g.reshape(1, d), w, w)


def _mix_in(x, ada3, mod_cols, g, w_main, w_ab, l, *, layout):
    t, d = x.shape
    n_ctx, dec_seq = layout
    tm = _row_tile((n_ctx, dec_seq), 1024)
    row = _mod_row(n_ctx, dec_seq, tm)
    tn = 512
    return pl.pallas_call(
        _mix_in_kernel,
        grid=(t // tm, PROJ_MAIN // tn),
        in_specs=_pm_specs(d, tm, row, mod_cols) + [
            pl.BlockSpec((None, d, tn), lambda i, j: (l, 0, j)),
            pl.BlockSpec((None, d, LANE), lambda i, j: (l, 0, 0)),
        ],
        out_specs=[pl.BlockSpec((tm, tn), lambda i, j: (i, j)),
                   pl.BlockSpec((tm, LANE), lambda i, j: (i, 0))],
        out_shape=[jax.ShapeDtypeStruct((t, PROJ_MAIN), F32), jax.ShapeDtypeStruct((t, LANE), F32)],
        scratch_shapes=[pltpu.VMEM((tm, d), BF16)],
        compiler_params=_cparams("parallel", "arbitrary"),
        name="mix_in",
    )(x, ada3, ada3, g.reshape(1, d), w_main, w_ab)


def _ffn_out_kernel(*refs, n_first):
    a_ref, w_ref = refs[:2]
    x_refs = refs[2:-2]
    gate_ref, o_ref = refs[-2:]
    y = _dot(a_ref[...], w_ref[...])
    o_ref[...] = _read_x(x_refs, n_first, Ellipsis) + (0.5 * gate_ref[0]) * y


def _ffn_out(a, w, l, x, ada3, gate_col, *, layout):
    t, k = a.shape
    d = w.shape[2]
    n_ctx, dec_seq = layout
    tm = _row_tile((n_ctx, dec_seq), 1024)
    tn = min(512, d)
    xs, n_first, _ = _x_parts(x, tm)
    row = _mod_row(n_ctx, dec_seq, tm)
    gpb = d // tn
    return pl.pallas_call(
        functools.partial(_ffn_out_kernel, n_first=n_first),
        grid=(t // tm, d // tn),
        in_specs=[
            pl.BlockSpec((tm, k), lambda i, j: (i, 0)),
            pl.BlockSpec((None, k, tn), lambda i, j: (l, 0, j)),
            *_x_specs(n_first, tm, tn),
            pl.BlockSpec((1, 1, tn), lambda i, j: (row(i), 0, gate_col * gpb + j)),
        ],
        out_specs=pl.BlockSpec((tm, tn), lambda i, j: (i, j)),
        out_shape=jax.ShapeDtypeStruct((t, d), F32),
        compiler_params=_cparams("parallel", "parallel"),
        name="ffn_out",
    )(a, w, *xs, ada3)


def _mix_out_kernel(a1_ref, a2_ref, a3_ref, w_ref, x_ref, gate_ref, o_ref):
    y = (_dot(a1_ref[...], w_ref[0:GDN_W, :])
         + _dot(a2_ref[...], w_ref[GDN_W:GDN_W + DIFF_W, :])
         + _dot(a3_ref[...], w_ref[GDN_W + DIFF_W:MIX_W, :]))
    o_ref[...] = x_ref[...] + gate_ref[0] * y


def _mix_out(acts, w, l, x, ada3, gate_col, row0, *, layout):
    a1, a2, a3 = acts
    n = a1.shape[0]
    t, d = x.shape
    n_ctx, dec_seq = layout
    tm = _row_tile((n_ctx, dec_seq), 1024)
    tn = min(512, d)
    row = _mod_row(n_ctx, dec_seq, tm, row0)
    rb0 = row0 // tm
    gpb = d // tn
    return pl.pallas_call(
        _mix_out_kernel,
        grid=(n // tm, d // tn),
        in_specs=[
            pl.BlockSpec((tm, GDN_W), lambda i, j: (i, 0)),
            pl.BlockSpec((tm, DIFF_W), lambda i, j: (i, 0)),
            pl.BlockSpec((tm, POOL_W), lambda i, j: (i, 0)),
            pl.BlockSpec((None, MIX_W, tn), lambda i, j: (l, 0, j)),
            pl.BlockSpec((tm, tn), lambda i, j: (rb0 + i, j)),
            pl.BlockSpec((1, 1, tn), lambda i, j: (row(i), 0, gate_col * gpb + j)),
        ],
        out_specs=pl.BlockSpec((tm, tn), lambda i, j: (rb0 + i, j)),
        out_shape=jax.ShapeDtypeStruct((t, d), F32),
        input_output_aliases={4: 0},
        compiler_params=_cparams("parallel", "parallel"),
        name="mix_out",
    )(a1, a2, a3, w, x, ada3)


def _final_norm_kernel(x_ref, g_ref, o_ref):
    x = x_ref[...]
    o_ref[...] = x * lax.rsqrt(jnp.mean(x * x, axis=-1, keepdims=True) + EPS) * g_ref[...]


def _final_norm(x, g, row0, nrows):
    d = x.shape[1]
    tm = 256
    rb0 = row0 // tm
    return pl.pallas_call(
        _final_norm_kernel,
        grid=(nrows // tm,),
        in_specs=[pl.BlockSpec((tm, d), lambda i: (rb0 + i, 0)), pl.BlockSpec((1, d), lambda i: (0, 0))],
        out_specs=pl.BlockSpec((tm, d), lambda i: (i, 0)),
        out_shape=jax.ShapeDtypeStruct((nrows, d), F32),
        compiler_params=_cparams("parallel"),
        name="final_norm",
    )(x, g.reshape(1, d))


def _shift_rows(x, off, row):
    if off == 0:
        return x
    n = x.shape[0]
    y = pltpu.roll(x, (-off) % n, 0)
    ok = (row + off >= 0) & (row + off < n)
    return jnp.where(ok, y, 0.0)


def _gdn_prep_kernel(x_ref, w_ref, o_ref, *, seqlen):
    left = CONV_K // 2
    edge = 16
    row = lax.broadcasted_iota(jnp.int32, (edge, LANE), 0)
    kind = pl.program_id(1)
    for s in range(x_ref.shape[0] // seqlen):
        for h in range(GDN_HEADS):
            rows = slice(s * seqlen, (s + 1) * seqlen)
            cols = slice(h * LANE, (h + 1) * LANE)
            x = x_ref[rows, cols]

            def conv(xb, shift):
                acc = jnp.zeros_like(xb)
                for j in range(CONV_K):
                    acc = acc + shift(xb, j - left) * w_ref[j:j + 1, cols]
                return acc

            acc = conv(x, lambda xb, off: xb if off == 0 else pltpu.roll(xb, (-off) % seqlen, 0))
            head = conv(x[0:edge], lambda xb, off: _shift_rows(xb, off, row))
            tail = conv(x[seqlen - edge:seqlen], lambda xb, off: _shift_rows(xb, off, row))
            acc = jnp.concatenate([head[0:8], acc[8:seqlen - 8], tail[edge - 8:edge]], axis=0)
            y = _silu(acc)
            inv = lax.rsqrt(jnp.sum(y * y, axis=-1, keepdims=True) + EPS)
            inv = inv * jnp.where(kind == 0, GDN_DK ** -0.5, 1.0)
            o_ref[rows, cols] = jnp.where(kind < 2, y * inv, y)


def _gdn_prep(proj, conv_w, row0, nseq, seqlen):
    ns = max(1, PREP_STEP_ROWS // seqlen)
    while nseq % ns or (row0 // seqlen) % ns:
        ns -= 1
    blk0 = row0 // (ns * seqlen)
    return pl.pallas_call(
        functools.partial(_gdn_prep_kernel, seqlen=seqlen),
        grid=(nseq // ns, 3),
        in_specs=[
            pl.BlockSpec((ns * seqlen, GDN_W), lambda s, c: (blk0 + s, CB_Q // GDN_HEADS + c)),
            pl.BlockSpec((CONV_K, GDN_W), lambda s, c: (0, c)),
        ],
        out_specs=pl.BlockSpec((ns * seqlen, GDN_W), lambda s, c: (s, c)),
        out_shape=jax.ShapeDtypeStruct((nseq * seqlen, 3 * GDN_W), F32),
        compiler_params=_cparams("parallel", "parallel"),
        name="gdn_prep",
    )(proj, conv_w)


def _unit_tri_solve_many(lmats, rhss, eye):
    xs = [eye - l for l in lmats]
    ps = [_mm(l, l) for l in lmats]
    steps = int(math.log2(CHUNK)) - 1
    for s in range(steps):
        xs = [x + _mm(x, p) for x, p in zip(xs, ps)]
        if s + 1 < steps:
            ps = [_mm(p, p) for p in ps]
    x0s = [_mm(x, r) for x, r in zip(xs, rhss)]
    resids = [r - (x0 + _mm(l, x0)) for l, r, x0 in zip(lmats, rhss, x0s)]
    return [x0 + _mm(x, rs) for x, x0, rs in zip(xs, x0s, resids)]


def _gdn_intra_kernel(q_ref, k_ref, v_ref, ab_ref, alog_ref, dt_ref,
                      u_ref, w_ref, qg_ref, kd_ref, a_ref, gl_ref):
    n = GDN_BLK
    ab = ab_ref[...]
    lane = lax.broadcasted_iota(jnp.int32, ab.shape, 1)
    sp = jnp.maximum(ab + dt_ref[...], 0.0) + jnp.log(1.0 + jnp.exp(-jnp.abs(ab + dt_ref[...])))
    gmat = -jnp.exp(alog_ref[...]) * sp
    bmat = _sigmoid(ab)

    ri = lax.broadcasted_iota(jnp.int32, (n, n), 0)
    ci = lax.broadcasted_iota(jnp.int32, (n, n), 1)
    same = (ri // CHUNK) == (ci // CHUNK)
    eye = (ri == ci).astype(F32)
    r8 = lax.broadcasted_iota(jnp.int32, (8, LANE), 0)
    gtot_all = _dot_exact_lhs(same.astype(BF16), gmat)
    gcf_all = _dot_exact_lhs((same & (ci <= ri)).astype(BF16), gmat)
    gc_dirs = (gcf_all, (gtot_all - gcf_all) + gmat)
    incl_dirs = (same & (ci <= ri), same & (ci >= ri))
    strict_dirs = (same & (ci < ri), same & (ci > ri))

    lmats, rhss, dests = [], [], []
    for h in range(GDN_HEADS):
        cols = slice(h * LANE, (h + 1) * LANE)
        q = q_ref[:, cols]
        k = k_ref[:, cols]
        v = v_ref[:, cols]
        kk = _mm(k, k, _dot_nt)
        qk = _mm(q, k, _dot_nt)
        for d in range(2):
            incl = incl_dirs[d]
            sel_g = lane == d * GDN_HEADS + h
            sel_b = lane == 2 * GDN_HEADS + d * GDN_HEADS + h
            gc = jnp.sum(jnp.where(sel_g, gc_dirs[d], 0.0), axis=1, keepdims=True)
            gtot = jnp.sum(jnp.where(sel_g, gtot_all, 0.0), axis=1, keepdims=True)
            beta = jnp.sum(jnp.where(sel_b, bmat, 0.0), axis=1, keepdims=True)
            gc_b = jnp.broadcast_to(gc, (n, LANE))
            gc_row = jnp.transpose(gc_b)[0:1, :]
            diff = jnp.where(incl, gc - gc_row, 0.0)
            decay = jnp.where(incl, jnp.exp(diff), 0.0)
            eg = jnp.exp(gc_b)
            lmats.append(jnp.where(strict_dirs[d], kk * beta * decay, 0.0))
            rhss.append(jnp.concatenate([v * beta, (k * beta) * eg], axis=1))
            dests.append((d, cols))
            qg_ref[d, :, cols] = (q * eg).astype(qg_ref.dtype)
            kd_ref[d, :, cols] = (k * jnp.exp(jnp.broadcast_to(gtot - gc, (n, LANE)))).astype(kd_ref.dtype)
            amat = qk * decay
            a_ref[d, h] = ((amat[:, 0:CHUNK] + amat[:, CHUNK:2 * CHUNK]) + (
                amat[:, 2 * CHUNK:3 * CHUNK] + amat[:, 3 * CHUNK:4 * CHUNK])).astype(a_ref.dtype)
            egt = jnp.exp(jnp.broadcast_to(gtot, (n, LANE)))
            gl8 = jnp.zeros((8, LANE), F32)
            for c in range(n // CHUNK):
                gl8 = jnp.where(r8 == c, egt[c * CHUNK:c * CHUNK + 8, :], gl8)
            gl_ref[d, :, cols] = gl8

    for (d, cols), uw in zip(dests, _unit_tri_solve_many(lmats, rhss, eye)):
        u_ref[d, :, cols] = uw[:, :GDN_DV]
        w_ref[d, :, cols] = uw[:, GDN_DV:].astype(w_ref.dtype)


def _gdn_intra(qkv, proj_ab, alog_row, dt_row, row0):
    t = qkv.shape[0]
    nb = t // GDN_BLK
    pb0 = row0 // GDN_BLK
    big = lambda dt: jax.ShapeDtypeStruct((2, t, GDN_W), dt)
    big_spec = pl.BlockSpec((2, GDN_BLK, GDN_W), lambda i: (0, i, 0))
    return pl.pallas_call(
        _gdn_intra_kernel,
        grid=(nb,),
        in_specs=[
            pl.BlockSpec((GDN_BLK, GDN_W), lambda i: (i, 0)),
            pl.BlockSpec((GDN_BLK, GDN_W), lambda i: (i, 1)),
            pl.BlockSpec((GDN_BLK, GDN_W), lambda i: (i, 2)),
            pl.BlockSpec((GDN_BLK, LANE), lambda i: (pb0 + i, 0)),
            pl.BlockSpec((1, LANE), lambda i: (0, 0)),
            pl.BlockSpec((1, LANE), lambda i: (0, 0)),
        ],
        out_specs=[big_spec, big_spec, big_spec, big_spec,
                   pl.BlockSpec((2, GDN_HEADS, GDN_BLK, CHUNK), lambda i: (0, 0, i, 0)),
                   pl.BlockSpec((2, 8, GDN_W), lambda i: (0, i, 0))],
        out_shape=[big(F32), big(BF16), big(BF16), big(BF16),
                   jax.ShapeDtypeStruct((2, GDN_HEADS, t, CHUNK), BF16),
                   jax.ShapeDtypeStruct((2, nb * 8, GDN_W), F32)],
        compiler_params=_cparams("parallel"),
        name="gdn_intra",
    )(qkv, qkv, qkv, proj_ab, alog_row, dt_row)


def _gdn_scan_kernel(*refs, has_init):
    if has_init:
        s0_ref, refs = refs[0], refs[1:]
    ins = (refs[0:6], refs[6:12])
    o_refs, sf_ref, s_ref = refs[12:14], refs[14], refs[15]
    blk = pl.program_id(1)
    nchunk = GDN_BLK // CHUNK

    @pl.when(blk == 0)
    def _():
        if has_init:
            s_ref[...] = s0_ref[...]
        else:
            s_ref[...] = jnp.zeros_like(s_ref)

    U, W, QG, KD, A, GL = range(6)
    chains = [(d, h) for d in range(2) for h in range(GDN_HEADS)]
    states = [s_ref[d, h] for d, h in chains]
    for c in range(nchunk):
        ccs = [c if d == 0 else nchunk - 1 - c for d, _ in chains]
        rows = [slice(cc * CHUNK, (cc + 1) * CHUNK) for cc in ccs]
        cols = [slice(h * LANE, (h + 1) * LANE) for _, h in chains]
        n = range(len(chains))
        ws = [_mm(ins[chains[i][0]][W][rows[i], cols[i]], states[i]) for i in n]
        qs = [_mm(ins[chains[i][0]][QG][rows[i], cols[i]], states[i]) for i in n]
        v_news = [ins[chains[i][0]][U][rows[i], cols[i]] - ws[i] for i in n]
        for i in n:
            d, h = chains[i]
            o_refs[d][rows[i], cols[i]] = qs[i] + _mm(ins[d][A][h, rows[i], :], v_news[i])
        states = [states[i] * ins[chains[i][0]][GL][ccs[i]:ccs[i] + 1, cols[i]]
                  + _mm(ins[chains[i][0]][KD][rows[i], cols[i]], v_news[i], _dot_tn) for i in n]
    for (d, h), s in zip(chains, states):
        s_ref[d, h] = s

    @pl.when(blk == pl.num_programs(1) - 1)
    def _():
        sf_ref[...] = s_ref[...]


def _gdn_scan(s0, u, w, qg, kd, a, gl, nseq, seqlen):
    t = u.shape[1]
    nblk = seqlen // GDN_BLK
    in_specs, args = [], []
    for d in range(2):
        rb = (lambda s, b: s * nblk + b) if d == 0 else (lambda s, b: s * nblk + nblk - 1 - b)
        big = pl.BlockSpec((None, GDN_BLK, GDN_W), lambda s, b, d=d, rb=rb: (d, rb(s, b), 0))
        in_specs += [big, big, big, big,
                     pl.BlockSpec((None, GDN_HEADS, GDN_BLK, CHUNK), lambda s, b, d=d, rb=rb: (d, 0, rb(s, b), 0)),
                     pl.BlockSpec((None, 8, GDN_W), lambda s, b, d=d, rb=rb: (d, rb(s, b), 0))]
        args += [u, w, qg, kd, a, gl]
    state_spec = pl.BlockSpec((None, 2, GDN_HEADS, GDN_DK, GDN_DV), lambda s, b: (s, 0, 0, 0, 0))
    if s0 is not None:
        in_specs.insert(0, state_spec)
        args.insert(0, s0)
    out_specs = [pl.BlockSpec((GDN_BLK, GDN_W), lambda s, b: (s * nblk + b, 0)),
                 pl.BlockSpec((GDN_BLK, GDN_W), lambda s, b: (s * nblk + nblk - 1 - b, 0)),
                 state_spec]
    return pl.pallas_call(
        functools.partial(_gdn_scan_kernel, has_init=s0 is not None),
        grid=(nseq, nblk),
        in_specs=in_specs,
        out_specs=out_specs,
        out_shape=[jax.ShapeDtypeStruct((t, GDN_W), F32), jax.ShapeDtypeStruct((t, GDN_W), F32),
                   jax.ShapeDtypeStruct((nseq, 2, GDN_HEADS, GDN_DK, GDN_DV), F32)],
        scratch_shapes=[pltpu.VMEM((2, GDN_HEADS, GDN_DK, GDN_DV), F32)],
        compiler_params=_cparams("parallel", "arbitrary"),
        name="gdn_scan",
    )(*args)


def _gdn_post_kernel(of_ref, ob_ref, z_ref, g_ref, out_ref):
    for h in range(GDN_HEADS):
        cols = slice(h * LANE, (h + 1) * LANE)
        o = of_ref[:, cols] + ob_ref[:, cols]
        y = o * lax.rsqrt(jnp.mean(o * o, axis=-1, keepdims=True) + EPS) * g_ref[...]
        out_ref[:, cols] = (y * _silu(z_ref[:, cols])).astype(out_ref.dtype)


def _gdn_post(o_f, o_b, proj, g, row0):
    t = o_f.shape[0]
    tm = _row_tile((t, row0 + t), 1024)
    pb0 = row0 // tm
    return pl.pallas_call(
        _gdn_post_kernel,
        grid=(t // tm,),
        in_specs=[
            pl.BlockSpec((tm, GDN_W), lambda i: (i, 0)),
            pl.BlockSpec((tm, GDN_W), lambda i: (i, 0)),
            pl.BlockSpec((tm, GDN_W), lambda i: (pb0 + i, CB_Z // GDN_HEADS)),
            pl.BlockSpec((1, GDN_DV), lambda i: (0, 0)),
        ],
        out_specs=pl.BlockSpec((tm, GDN_W), lambda i: (i, 0)),
        out_shape=jax.ShapeDtypeStruct((t, GDN_W), BF16),
        compiler_params=_cparams("parallel"),
        name="gdn_post",
    )(o_f, o_b, proj, g.reshape(1, GDN_DV))


def _rope_rotate(x, cos, sin_signed):
    lane = lax.broadcasted_iota(jnp.int32, x.shape, 1)
    first = (lane % 32) < 16
    partner = jnp.where(first, pltpu.roll(x, LANE - 16, 1), pltpu.roll(x, 16, 1))
    return x * cos + partner * sin_signed


def _rope_tables(seqlen):
    pos = jnp.arange(seqlen)
    rowp = (pos // GRID_W).astype(F32)
    colp = (pos % GRID_W).astype(F32)
    half = DIFF_DQK // 2
    inv_freq = ROPE_BASE ** (-jnp.arange(0, half, 2, dtype=F32) / half)
    c = jnp.arange(LANE)
    c64 = c % DIFF_DQK
    p = jnp.where((c64 // half)[None, :] == 0, rowp[:, None], colp[:, None])
    ang = p * inv_freq[c64 % (half // 2)][None, :]
    sign = jnp.where((c64 % half) < half // 2, -1.0, 1.0).astype(F32)
    return jnp.cos(ang), jnp.sin(ang) * sign[None, :]


def _diff_attn_kernel(*refs, seg_lens, lam_init, rope, hp, n_alias, emit_cache):
    nseg = len(seg_lens)
    q_ref = refs[0]
    k_refs = refs[1:1 + nseg]
    v_refs = refs[1 + nseg:1 + 2 * nseg]
    pos = 1 + 2 * nseg
    if rope:
        cosq_ref, sinq_ref, cosk_ref, sink_ref = refs[pos:pos + 4]
        pos += 4
    lam_ref, g_ref = refs[pos:pos + 2]
    pos += 2 + n_alias
    o_ref = refs[pos]
    if emit_cache:
        kc_ref, vc_ref = refs[pos + 1:pos + 3]
    kb_ref, vb_ref = refs[-2:]
    offs = [sum(seg_lens[:s]) for s in range(nseg)]
    hcols = [slice(hh * DIFF_DV, (hh + 1) * DIFF_DV) for hh in range(hp)]

    @pl.when(pl.program_id(2) == 0)
    def _():
        if emit_cache:
            kc_ref[...] = k_refs[-1][...]
            vc_ref[...] = v_refs[-1][...]
        for hh in range(hp):
            vb_ref[:, (2 * hh + 1) * DIFF_DV:(2 * hh + 2) * DIFF_DV] = jnp.ones((vb_ref.shape[0], DIFF_DV), BF16)
            for s in range(nseg):
                rows = slice(offs[s], offs[s] + seg_lens[s])
                k = k_refs[s][:, hcols[hh]]
                if rope and s == nseg - 1:
                    k = _rope_rotate(k, cosk_ref[...], sink_ref[...])
                kb_ref[rows, hcols[hh]] = k.astype(BF16)
                vb_ref[rows, 2 * hh * DIFF_DV:(2 * hh + 1) * DIFF_DV] = v_refs[s][:, hcols[hh]].astype(BF16)

    lv = lam_ref[...]
    lam = (jnp.exp(jnp.sum(lv[0:1] * lv[1:2], axis=1, keepdims=True))
           - jnp.exp(jnp.sum(lv[2:3] * lv[3:4], axis=1, keepdims=True)) + lam_init)
    lane = lax.broadcasted_iota(jnp.int32, (ATTN_SUB, DIFF_DV), 1)

    def scores(hh, rows):
        q = q_ref[rows, hcols[hh]]
        if rope:
            q = _rope_rotate(q, cosq_ref[rows, :], sinq_ref[rows, :])
        q = q * (DIFF_DQK ** -0.5)
        kb = kb_ref[:, hcols[hh]]
        return [_dot_nt(jnp.where((lane // DIFF_DQK) == m, q, 0.0).astype(BF16), kb) for m in range(2)]

    units = [(hh, slice(r * ATTN_SUB, (r + 1) * ATTN_SUB))
             for hh in range(hp) for r in range(q_ref.shape[0] // ATTN_SUB)]
    ss_next = scores(*units[0])
    for n, (hh, rows) in enumerate(units):
        ss = ss_next
        if n + 1 < len(units):
            ss_next = scores(*units[n + 1])
        vb = vb_ref[:, 2 * hh * DIFF_DV:(2 * hh + 2) * DIFF_DV]
        es = [jnp.exp(s - jnp.max(s, axis=1, keepdims=True)).astype(BF16) for s in ss]
        accs = [_dot(e, vb) for e in es]
        outs = [acc[:, 0:DIFF_DV] / acc[:, DIFF_DV:] for acc in accs]
        o = outs[0] - lam * outs[1]
        y = o * lax.rsqrt(jnp.mean(o * o, axis=-1, keepdims=True) + EPS) * g_ref[...]
        o_ref[rows, hcols[hh]] = (y * (1.0 - lam_init)).astype(o_ref.dtype)


def _diff_attn(proj, row0, nseq, seqlen, past_kv, l, lam_vec, g, cache_out):
    tq = min(4 * ATTN_SUB, seqlen)
    hp = min(DIFF_HEADS, max(1, ATTN_STEP_ROWS // seqlen))
    nqb = seqlen // tq
    qb0 = row0 // tq
    sb0 = row0 // seqlen
    wcols = hp * DIFF_DV
    q_spec = pl.BlockSpec((tq, wcols), lambda b, h, i: (qb0 + b * nqb + i, CB_DQ // hp + h))
    k_new = pl.BlockSpec((seqlen, wcols), lambda b, h, i: (sb0 + b, CB_DK // hp + h))
    v_new = pl.BlockSpec((seqlen, wcols), lambda b, h, i: (sb0 + b, CB_DV // hp + h))
    arrs, specs, seg_lens = [proj], [q_spec], [seqlen]
    if past_kv is None:
        arrs += [proj, proj]
        specs += [k_new, v_new]
    else:
        past = past_kv[0].shape[2]
        cached = pl.BlockSpec((None, None, past, wcols), lambda b, h, i: (b, l, 0, h))
        arrs += [past_kv[0], proj, past_kv[1], proj]
        specs += [cached, k_new, cached, v_new]
        seg_lens = [past, seqlen]
        cos, sin = _rope_tables(seqlen)
        arrs += [cos, sin, cos, sin]
        specs += [pl.BlockSpec((tq, DIFF_DV), lambda b, h, i: (i, 0))] * 2 + [
            pl.BlockSpec((seqlen, DIFF_DV), lambda b, h, i: (0, 0))] * 2
    arrs += [lam_vec, g.reshape(1, DIFF_DV)]
    specs += [pl.BlockSpec((4, DIFF_DQK), lambda b, h, i: (0, 0)),
              pl.BlockSpec((1, DIFF_DV), lambda b, h, i: (0, 0))]
    out_specs = [pl.BlockSpec((tq, wcols), lambda b, h, i: (b * nqb + i, h))]
    out_shape = [jax.ShapeDtypeStruct((nseq * seqlen, DIFF_W), BF16)]
    aliases, n_alias = {}, 0
    if cache_out is not None:
        depth, prev = cache_out
        entry = pl.BlockSpec((None, None, seqlen, wcols), lambda b, h, i: (b, l, 0, h))
        out_specs += [entry, entry]
        out_shape += [jax.ShapeDtypeStruct((nseq, depth, seqlen, DIFF_W), F32)] * 2
        if prev is not None:
            aliases = {len(arrs): 1, len(arrs) + 1: 2}
            n_alias = 2
            arrs += list(prev)
            specs += [pl.BlockSpec(memory_space=pl.ANY)] * 2
    lk = sum(seg_lens)
    outs = pl.pallas_call(
        functools.partial(_diff_attn_kernel, seg_lens=tuple(seg_lens), lam_init=_lambda_init(l),
                          rope=past_kv is not None, hp=hp, n_alias=n_alias, emit_cache=cache_out is not None),
        grid=(nseq, DIFF_HEADS // hp, nqb),
        in_specs=specs,
        out_specs=out_specs,
        out_shape=out_shape,
        input_output_aliases=aliases,
        scratch_shapes=[pltpu.VMEM((lk, wcols), BF16), pltpu.VMEM((lk, 2 * wcols), BF16)],
        compiler_params=_cparams("parallel", "parallel", "arbitrary"),
        name="diff_attn",
    )(*arrs)
    return outs[0], (tuple(outs[1:]) if cache_out is not None else None)


def _pool_kernel(x_ref, w_ref, sc_ref, o_ref):
    n = x_ref.shape[0]
    row = lax.broadcasted_iota(jnp.int32, (n, POOL_GW), 0)
    for gi, win in enumerate(POOL_WINDOWS):
        cols = slice(gi * POOL_GW, (gi + 1) * POOL_GW)
        x = x_ref[:, cols]
        a = win // 2
        b = win - a - 1
        acc = x
        for off in range(-a, b + 1):
            if off != 0:
                acc = acc + _shift_rows(x, off, row)
        cnt = (jnp.minimum(row + b + 1, n) - jnp.maximum(row - a, 0)).astype(F32)
        pooled = acc / cnt - x
        y = _dot(pooled.astype(BF16), w_ref[gi].astype(BF16))
        o_ref[:, cols] = (y * sc_ref[:, cols]).astype(o_ref.dtype)


def _pool(proj, pool_w, pool_scale, row0, nseq, seqlen):
    blk0 = row0 // seqlen
    return pl.pallas_call(
        _pool_kernel,
        grid=(nseq,),
        in_specs=[
            pl.BlockSpec((seqlen, POOL_W), lambda s: (blk0 + s, CB_PIN // len(POOL_WINDOWS))),
            pl.BlockSpec((len(POOL_WINDOWS), POOL_GW, POOL_GW), lambda s: (0, 0, 0)),
            pl.BlockSpec((1, POOL_W), lambda s: (0, 0)),
        ],
        out_specs=pl.BlockSpec((seqlen, POOL_W), lambda s: (s, 0)),
        out_shape=jax.ShapeDtypeStruct((nseq * seqlen, POOL_W), BF16),
        compiler_params=_cparams("parallel"),
        name="pool",
    )(proj, pool_w, pool_scale.reshape(1, POOL_W))


def _mix_group(proj, proj_ab, lp, l, row0, nseq, seqlen, ctx, cache_out):
    qkv = _gdn_prep(proj, lp["gdn_conv"], row0, nseq, seqlen)
    u, w, qg, kd, a, gl = _gdn_intra(qkv, proj_ab, lp["alog_row"], lp["dt_row"], row0)
    o_f, o_b, s_fin = _gdn_scan(None if ctx is None else ctx[2], u, w, qg, kd, a, gl, nseq, seqlen)
    o_gdn = _gdn_post(o_f, o_b, proj, lp["gdn_norm"], row0)

    o_diff, caches = _diff_attn(proj, row0, nseq, seqlen, None if ctx is None else ctx[:2], l,
                                lp["diff_lam"], lp["diff_norm"], cache_out)
    o_pool = _pool(proj, lp["pool_w"], lp["pool_scale"], row0, nseq, seqlen)
    return (o_gdn, o_diff, o_pool), s_fin, caches


def _split_w_in(w_in):
    depth, d, _ = w_in.shape
    o_ab = 4 * GDN_W
    o_dq = o_ab + 4 * GDN_HEADS
    pad = jnp.zeros((depth, d, LANE - 4 * GDN_HEADS), w_in.dtype)
    main = jnp.concatenate([w_in[..., :o_ab], w_in[..., o_dq:]], axis=-1)
    ab = jnp.concatenate([w_in[..., o_ab:o_dq], pad], axis=-1)
    return main.astype(BF16), ab.astype(BF16)


def kernel(x_prompt, x_sample, c, cache_k, cache_v, state_gdn, c_ctx, w_ada, b_ada, norm_ffn1, ffn1_in,
           ffn1_out, norm_mix, w_in, gdn_conv, gdn_a_log, gdn_dt_bias, gdn_norm, diff_lam, diff_norm,
           pool_w, pool_scale, w_out, norm_ffn2, ffn2_in, ffn2_out, final_norm):
    batch, seq, d = x_prompt.shape
    dec_batch, dec_seq, _ = x_sample.shape
    depth = w_ada.shape[0]
    n_ctx = batch * seq
    n_dec = dec_batch * dec_seq
    layout = (n_ctx, dec_seq)
    past = cache_k.shape[2]

    x = (x_prompt.reshape(n_ctx, d), x_sample.reshape(n_dec, d))
    cond8 = jnp.concatenate([c_ctx[None, :], c, jnp.zeros((8 - 1 - dec_batch, d), F32)], axis=0)
    ada = _ada(cond8, w_ada, b_ada)
    cache_k4 = cache_k.reshape(dec_batch, depth, past, DIFF_W)
    cache_v4 = cache_v.reshape(dec_batch, depth, past, DIFF_W)
    w_in_main, w_in_ab = _split_w_in(w_in)
    w_out_b = w_out.astype(BF16)
    ffn_w = [(ffn1_in, ffn1_out.astype(BF16)), (ffn2_in, ffn2_out.astype(BF16))]

    caches, ss = None, []
    for l in range(depth):
        ada3 = ada[l, :1 + dec_batch].reshape(1 + dec_batch, 1, N_MOD * d)
        lane_pad = jnp.zeros((LANE - 2 * GDN_HEADS,), F32)
        lp = dict(
            gdn_conv=gdn_conv[l], gdn_norm=gdn_norm[l], diff_lam=diff_lam[l], diff_norm=diff_norm[l],
            pool_w=pool_w[l], pool_scale=pool_scale[l],
            alog_row=jnp.concatenate([gdn_a_log[l].reshape(-1), lane_pad]).reshape(1, LANE),
            dt_row=jnp.concatenate([gdn_dt_bias[l].reshape(-1), lane_pad]).reshape(1, LANE),
        )
        act = _ffn_in(x, ada3, (0, 1), norm_ffn1[l], ffn_w[0][0], l, layout=layout)
        x = _ffn_out(act, ffn_w[0][1], l, x, ada3, 2, layout=layout)

        proj, proj_ab = _mix_in(x, ada3, (3, 4), norm_mix[l], w_in_main, w_in_ab, l, layout=layout)
        acts_c, s_c, caches = _mix_group(proj, proj_ab, lp, l, 0, batch, seq, None, (depth, caches))
        acts_d, _, _ = _mix_group(proj, proj_ab, lp, l, n_ctx, dec_batch, dec_seq,
                                  (cache_k4, cache_v4, state_gdn[:, l]), None)
        x = _mix_out(acts_c, w_out_b, l, x, ada3, 5, 0, layout=layout)
        x = _mix_out(acts_d, w_out_b, l, x, ada3, 5, n_ctx, layout=layout)
        ss.append(s_c)

        act = _ffn_in(x, ada3, (6, 7), norm_ffn2[l], ffn_w[1][0], l, layout=layout)
        x = _ffn_out(act, ffn_w[1][1], l, x, ada3, 8, layout=layout)

    y_prompt = _final_norm(x, final_norm, 0, n_ctx).reshape(batch, seq, d)
    y_sample = _final_norm(x, final_norm, n_ctx, n_dec).reshape(dec_batch, dec_seq, d)
    new_k, new_v = (t.reshape(batch, depth, seq, DIFF_HEADS, DIFF_DV) for t in caches)
    return (y_prompt, y_sample, new_k, new_v, jnp.stack(ss, axis=1))
```

```python
import functools
import math

import jax
import jax.numpy as jnp
from jax import lax
from jax.experimental import pallas as pl
from jax.experimental.pallas import tpu as pltpu

F32 = jnp.float32
BF16 = jnp.bfloat16

GRID_W = 64
GDN_HEADS = 4
GDN_DK = 128
GDN_DV = 128
GDN_W = GDN_HEADS * GDN_DV
CONV_K = 4
CHUNK = 64
DIFF_HEADS = 8
DIFF_DQK = 64
DIFF_DV = 2 * DIFF_DQK
DIFF_W = DIFF_HEADS * DIFF_DV
ROPE_BASE = 10000.0
POOL_WINDOWS = (2, 4, 8, 16)
POOL_GW = 128
POOL_W = len(POOL_WINDOWS) * POOL_GW
MIX_W = GDN_W + DIFF_W + POOL_W
N_MOD = 9
EPS = 1e-6

LANE = 128
GDN_BLK = 4 * CHUNK
ATTN_SUB = 256
ATTN_STEP_ROWS = 2048
PRO_SUB = 256
PREP_STEP_ROWS = 1024
VMEM_LIMIT = 60 * 1024 * 1024

CB_Q, CB_K, CB_V, CB_Z = 0, 4, 8, 12
CB_DQ, CB_DK, CB_DV, CB_PIN = 16, 24, 32, 40
PROJ_MAIN = 44 * LANE


def _lambda_init(l):
    return 0.8 - 0.6 * math.exp(-0.3 * l)


def _cparams(*sem):
    return pltpu.CompilerParams(dimension_semantics=sem, vmem_limit_bytes=VMEM_LIMIT)


def _dot(a, b):
    return lax.dot_general(a, b, (((1,), (0,)), ((), ())), preferred_element_type=F32)


def _dot_nt(a, b):
    return lax.dot_general(a, b, (((1,), (1,)), ((), ())), preferred_element_type=F32)


def _dot_tn(a, b):
    return lax.dot_general(a, b, (((0,), (0,)), ((), ())), preferred_element_type=F32)


def _mm(a, b, dot=_dot):
    return dot(a.astype(BF16), b.astype(BF16))


def _dot_exact_lhs(m, b):
    b0 = b.astype(BF16)
    r = b - b0.astype(F32)
    b1 = r.astype(BF16)
    b2 = (r - b1.astype(F32)).astype(BF16)
    return _dot(m, b0) + (_dot(m, b1) + _dot(m, b2))


def _sigmoid(x):
    return 0.5 * (1.0 + jnp.tanh(0.5 * x))


def _silu(x):
    return x * _sigmoid(x)


def _ada_kernel(cond_ref, w_ref, b_ref, o_ref):
    s = _silu(cond_ref[...]).astype(BF16)
    o_ref[...] = _dot(s, w_ref[...].astype(BF16)) + b_ref[...]


def _ada(cond8, w_ada, b_ada):
    depth, d, n = w_ada.shape
    tn = 1024
    return pl.pallas_call(
        _ada_kernel,
        grid=(depth, n // tn),
        in_specs=[
            pl.BlockSpec((8, d), lambda l, j: (0, 0)),
            pl.BlockSpec((None, d, tn), lambda l, j: (l, 0, j)),
            pl.BlockSpec((None, 1, tn), lambda l, j: (l, 0, j)),
        ],
        out_specs=pl.BlockSpec((None, 8, tn), lambda l, j: (l, 0, j)),
        out_shape=jax.ShapeDtypeStruct((depth, 8, n), F32),
        compiler_params=_cparams("parallel", "parallel"),
        name="ada",
    )(cond8, w_ada, b_ada.reshape(depth, 1, n))


def _norm_mod(x, g, shift, scale):
    y = x * lax.rsqrt(jnp.mean(x * x, axis=-1, keepdims=True) + EPS) * g
    return y * (1.0 + scale) + shift


def _read_x(x_refs, n_first, idx):
    if len(x_refs) == 1:
        return x_refs[0][idx]
    return jnp.where(pl.program_id(0) < n_first, x_refs[0][idx], x_refs[1][idx])


def _norm_mod_rows(x_refs, n_first, sh_ref, sc_ref, g_ref, rows):
    x = _read_x(x_refs, n_first, (rows, slice(None)))
    return _norm_mod(x, g_ref[...], sh_ref[0], sc_ref[0]).astype(BF16)


def _row_subs(n):
    return [slice(r * PRO_SUB, (r + 1) * PRO_SUB) for r in range(n // PRO_SUB)]


def _mix_in_kernel(x_ref, sh_ref, sc_ref, g_ref, wlo_ref, whi_ref, wab_ref, o_ref, ab_ref, h_ref, *, n_lo):
    j = pl.program_id(1)

    @pl.when(j == 0)
    def _():
        w = wlo_ref[...].astype(BF16)
        for rows in _row_subs(x_ref.shape[0]):
            h = _norm_mod_rows((x_ref,), None, sh_ref, sc_ref, g_ref, rows)
            h_ref[rows, :] = h
            ab_ref[rows, :] = _dot(h, wab_ref[...])
            o_ref[rows, :] = _dot(h, w)

    @pl.when((j != 0) & (j < n_lo))
    def _():
        o_ref[...] = _dot(h_ref[...], wlo_ref[...].astype(BF16))

    @pl.when(j >= n_lo)
    def _():
        o_ref[...] = _dot(h_ref[...], whi_ref[...])


def _ffn_in_kernel(*refs, n_first):
    x_refs = refs[:-7]
    sh_ref, sc_ref, g_ref, wg_ref, wu_ref, o_ref, h_ref = refs[-7:]

    def swiglu(h):
        gate = _dot(h, wg_ref[...].astype(BF16))
        up = _dot(h, wu_ref[...].astype(BF16))
        return (_silu(gate) * up).astype(o_ref.dtype)

    @pl.when(pl.program_id(1) == 0)
    def _():
        for rows in _row_subs(h_ref.shape[0]):
            h = _norm_mod_rows(x_refs, n_first, sh_ref, sc_ref, g_ref, rows)
            h_ref[rows, :] = h
            o_ref[rows, :] = swiglu(h)

    @pl.when(pl.program_id(1) != 0)
    def _():
        o_ref[...] = swiglu(h_ref[...])


def _row_tile(groups, cap):
    tm = cap
    while any(g % tm for g in groups):
        tm //= 2
    return tm


def _mod_row(n_ctx, dec_seq, tm, row0=0):
    def f(i):
        r = row0 + i * tm
        return jnp.where(r < n_ctx, 0, 1 + (r - n_ctx) // dec_seq)
    return f


def _x_parts(x, tm):
    if isinstance(x, tuple):
        return list(x), x[0].shape[0] // tm, sum(p.shape[0] for p in x)
    return [x], None, x.shape[0]


def _x_specs(n_first, tm, tn):
    if n_first is None:
        return [pl.BlockSpec((tm, tn), lambda i, j: (i, j))]
    return [pl.BlockSpec((tm, tn), lambda i, j: (jnp.where(i < n_first, i, n_first - 1), jnp.where(i < n_first, j, 0))),
            pl.BlockSpec((tm, tn), lambda i, j: (jnp.where(i < n_first, 0, i - n_first), jnp.where(i < n_first, 0, j)))]


def _pm_specs(d, tm, row, mod_cols, n_first=None):
    sh_c, sc_c = mod_cols
    x_specs = [pl.BlockSpec((tm, d), lambda i, j: (i, 0))] if n_first is None else [
        pl.BlockSpec((tm, d), lambda i, j: (jnp.minimum(i, n_first - 1), 0)),
        pl.BlockSpec((tm, d), lambda i, j: (jnp.maximum(i - n_first, 0), 0))]
    return x_specs + [
        pl.BlockSpec((1, 1, d), lambda i, j: (row(i), 0, sh_c)),
        pl.BlockSpec((1, 1, d), lambda i, j: (row(i), 0, sc_c)),
        pl.BlockSpec((1, d), lambda i, j: (0, 0)),
    ]


def _ffn_in(x, ada3, mod_cols, g, w, l, *, layout):
    d = w.shape[1]
    n_ctx, dec_seq = layout
    tm = _row_tile((n_ctx, dec_seq), 1024)
    xs, n_first, t = _x_parts(x, tm)
    row = _mod_row(n_ctx, dec_seq, tm)
    n = w.shape[2] // 2
    tn = 512 if n % 512 == 0 else n
    nj = n // tn
    return pl.pallas_call(
        functools.partial(_ffn_in_kernel, n_first=n_first),
        grid=(t // tm, nj),
        in_specs=_pm_specs(d, tm, row, mod_cols, n_first) + [
            pl.BlockSpec((None, d, tn), lambda i, j: (l, 0, j)),
            pl.BlockSpec((None, d, tn), lambda i, j: (l, 0, j + nj)),
        ],
        out_specs=pl.BlockSpec((tm, tn), lambda i, j: (i, j)),
        out_shape=jax.ShapeDtypeStruct((t, n), BF16),
        scratch_shapes=[pltpu.VMEM((tm, d), BF16)],
        compiler_params=_cparams("parallel", "arbitrary"),
        name="ffn_in",
    )(*xs, ada3, ada3, g.reshape(1, d), w, w)


def _mix_in(x, ada3, mod_cols, g, w_in, w_hi, w_ab, l, *, layout):
    t, d = x.shape
    n_ctx, dec_seq = layout
    tm = _row_tile((n_ctx, dec_seq), 1024)
    row = _mod_row(n_ctx, dec_seq, tm)
    tn = 512
    n_lo = (PROJ_MAIN - w_hi.shape[2]) // tn
    return pl.pallas_call(
        functools.partial(_mix_in_kernel, n_lo=n_lo),
        grid=(t // tm, PROJ_MAIN // tn),
        in_specs=_pm_specs(d, tm, row, mod_cols) + [
            pl.BlockSpec((None, d, tn), lambda i, j: (l, 0, jnp.minimum(j, n_lo - 1))),
            pl.BlockSpec((None, d, tn), lambda i, j: (l, 0, jnp.maximum(j - n_lo, 0))),
            pl.BlockSpec((None, d, LANE), lambda i, j: (l, 0, 0)),
        ],
        out_specs=[pl.BlockSpec((tm, tn), lambda i, j: (i, j)),
                   pl.BlockSpec((tm, LANE), lambda i, j: (i, 0))],
        out_shape=[jax.ShapeDtypeStruct((t, PROJ_MAIN), F32), jax.ShapeDtypeStruct((t, LANE), F32)],
        scratch_shapes=[pltpu.VMEM((tm, d), BF16)],
        compiler_params=_cparams("parallel", "arbitrary"),
        name="mix_in",
    )(x, ada3, ada3, g.reshape(1, d), w_in, w_hi, w_ab)


def _ffn_out_kernel(*refs, n_first):
    a_ref, w_ref = refs[:2]
    x_refs = refs[2:-2]
    gate_ref, o_ref = refs[-2:]
    y = _dot(a_ref[...], w_ref[...])
    o_ref[...] = _read_x(x_refs, n_first, Ellipsis) + (0.5 * gate_ref[0]) * y


def _ffn_out(a, w, l, x, ada3, gate_col, *, layout):
    t, k = a.shape
    d = w.shape[2]
    n_ctx, dec_seq = layout
    tm = _row_tile((n_ctx, dec_seq), 1024)
    tn = min(512, d)
    xs, n_first, _ = _x_parts(x, tm)
    row = _mod_row(n_ctx, dec_seq, tm)
    gpb = d // tn
    return pl.pallas_call(
        functools.partial(_ffn_out_kernel, n_first=n_first),
        grid=(t // tm, d // tn),
        in_specs=[
            pl.BlockSpec((tm, k), lambda i, j: (i, 0)),
            pl.BlockSpec((None, k, tn), lambda i, j: (l, 0, j)),
            *_x_specs(n_first, tm, tn),
            pl.BlockSpec((1, 1, tn), lambda i, j: (row(i), 0, gate_col * gpb + j)),
        ],
        out_specs=pl.BlockSpec((tm, tn), lambda i, j: (i, j)),
        out_shape=jax.ShapeDtypeStruct((t, d), F32),
        compiler_params=_cparams("parallel", "parallel"),
        name="ffn_out",
    )(a, w, *xs, ada3)


def _mix_out_kernel(a1_ref, a2_ref, a3_ref, w_ref, x_ref, gate_ref, o_ref):
    y = (_dot(a1_ref[...], w_ref[0:GDN_W, :])
         + _dot(a2_ref[...], w_ref[GDN_W:GDN_W + DIFF_W, :])
         + _dot(a3_ref[...], w_ref[GDN_W + DIFF_W:MIX_W, :]))
    o_ref[...] = x_ref[...] + gate_ref[0] * y


def _mix_out(acts, w, l, x, ada3, gate_col, row0, *, layout):
    a1, a2, a3 = acts
    n = a1.shape[0]
    t, d = x.shape
    n_ctx, dec_seq = layout
    tm = _row_tile((n_ctx, dec_seq), 1024)
    tn = min(512, d)
    row = _mod_row(n_ctx, dec_seq, tm, row0)
    rb0 = row0 // tm
    gpb = d // tn
    return pl.pallas_call(
        _mix_out_kernel,
        grid=(n // tm, d // tn),
        in_specs=[
            pl.BlockSpec((tm, GDN_W), lambda i, j: (i, 0)),
            pl.BlockSpec((tm, DIFF_W), lambda i, j: (i, 0)),
            pl.BlockSpec((tm, POOL_W), lambda i, j: (i, 0)),
            pl.BlockSpec((None, MIX_W, tn), lambda i, j: (l, 0, j)),
            pl.BlockSpec((tm, tn), lambda i, j: (rb0 + i, j)),
            pl.BlockSpec((1, 1, tn), lambda i, j: (row(i), 0, gate_col * gpb + j)),
        ],
        out_specs=pl.BlockSpec((tm, tn), lambda i, j: (rb0 + i, j)),
        out_shape=jax.ShapeDtypeStruct((t, d), F32),
        input_output_aliases={4: 0},
        compiler_params=_cparams("parallel", "parallel"),
        name="mix_out",
    )(a1, a2, a3, w, x, ada3)


def _final_norm_kernel(x_ref, g_ref, o_ref):
    x = x_ref[...]
    o_ref[...] = x * lax.rsqrt(jnp.mean(x * x, axis=-1, keepdims=True) + EPS) * g_ref[...]


def _final_norm(x, g, row0, nrows):
    d = x.shape[1]
    tm = 256
    rb0 = row0 // tm
    return pl.pallas_call(
        _final_norm_kernel,
        grid=(nrows // tm,),
        in_specs=[pl.BlockSpec((tm, d), lambda i: (rb0 + i, 0)), pl.BlockSpec((1, d), lambda i: (0, 0))],
        out_specs=pl.BlockSpec((tm, d), lambda i: (i, 0)),
        out_shape=jax.ShapeDtypeStruct((nrows, d), F32),
        compiler_params=_cparams("parallel"),
        name="final_norm",
    )(x, g.reshape(1, d))


def _shift_rows(x, off, row):
    if off == 0:
        return x
    n = x.shape[0]
    y = pltpu.roll(x, (-off) % n, 0)
    ok = (row + off >= 0) & (row + off < n)
    return jnp.where(ok, y, 0.0)


def _gdn_prep_kernel(x_ref, w_ref, o_ref, *, seqlen):
    left = CONV_K // 2
    edge = 16
    row = lax.broadcasted_iota(jnp.int32, (edge, LANE), 0)
    kind = pl.program_id(1)
    for s in range(x_ref.shape[0] // seqlen):
        for h in range(GDN_HEADS):
            rows = slice(s * seqlen, (s + 1) * seqlen)
            cols = slice(h * LANE, (h + 1) * LANE)
            x = x_ref[rows, cols]

            def conv(xb, shift):
                acc = jnp.zeros_like(xb)
                for j in range(CONV_K):
                    acc = acc + shift(xb, j - left) * w_ref[j:j + 1, cols]
                return acc

            acc = conv(x, lambda xb, off: xb if off == 0 else pltpu.roll(xb, (-off) % seqlen, 0))
            head = conv(x[0:edge], lambda xb, off: _shift_rows(xb, off, row))
            tail = conv(x[seqlen - edge:seqlen], lambda xb, off: _shift_rows(xb, off, row))
            acc = jnp.concatenate([head[0:8], acc[8:seqlen - 8], tail[edge - 8:edge]], axis=0)
            y = _silu(acc)
            inv = lax.rsqrt(jnp.sum(y * y, axis=-1, keepdims=True) + EPS)
            inv = inv * jnp.where(kind == 0, GDN_DK ** -0.5, 1.0)
            o_ref[rows, cols] = jnp.where(kind < 2, y * inv, y)


def _gdn_prep(proj, conv_w, row0, nseq, seqlen):
    ns = max(1, PREP_STEP_ROWS // seqlen)
    while nseq % ns or (row0 // seqlen) % ns:
        ns -= 1
    blk0 = row0 // (ns * seqlen)
    return pl.pallas_call(
        functools.partial(_gdn_prep_kernel, seqlen=seqlen),
        grid=(nseq // ns, 3),
        in_specs=[
            pl.BlockSpec((ns * seqlen, GDN_W), lambda s, c: (blk0 + s, CB_Q // GDN_HEADS + c)),
            pl.BlockSpec((CONV_K, GDN_W), lambda s, c: (0, c)),
        ],
        out_specs=pl.BlockSpec((ns * seqlen, GDN_W), lambda s, c: (s, c)),
        out_shape=jax.ShapeDtypeStruct((nseq * seqlen, 3 * GDN_W), F32),
        compiler_params=_cparams("parallel", "parallel"),
        name="gdn_prep",
    )(proj, conv_w)


def _unit_tri_solve_many(lmats, rhss, eye):
    xs = [eye - l for l in lmats]
    ps = [_mm(l, l) for l in lmats]
    steps = int(math.log2(CHUNK)) - 1
    for s in range(steps):
        xs = [x + _mm(x, p) for x, p in zip(xs, ps)]
        if s + 1 < steps:
            ps = [_mm(p, p) for p in ps]
    x0s = [_mm(x, r) for x, r in zip(xs, rhss)]
    resids = [r - (x0 + _mm(l, x0)) for l, r, x0 in zip(lmats, rhss, x0s)]
    return [x0 + _mm(x, rs) for x, x0, rs in zip(xs, x0s, resids)]


def _gdn_intra_kernel(q_ref, k_ref, v_ref, ab_ref, alog_ref, dt_ref,
                      u_ref, w_ref, qg_ref, kd_ref, a_ref, gl_ref):
    n = GDN_BLK
    ab = ab_ref[...]
    lane = lax.broadcasted_iota(jnp.int32, ab.shape, 1)
    sp = jnp.maximum(ab + dt_ref[...], 0.0) + jnp.log(1.0 + jnp.exp(-jnp.abs(ab + dt_ref[...])))
    gmat = -jnp.exp(alog_ref[...]) * sp
    bmat = _sigmoid(ab)

    ri = lax.broadcasted_iota(jnp.int32, (n, n), 0)
    ci = lax.broadcasted_iota(jnp.int32, (n, n), 1)
    same = (ri // CHUNK) == (ci // CHUNK)
    eye = (ri == ci).astype(F32)
    r8 = lax.broadcasted_iota(jnp.int32, (8, LANE), 0)
    gtot_all = _dot_exact_lhs(same.astype(BF16), gmat)
    gcf_all = _dot_exact_lhs((same & (ci <= ri)).astype(BF16), gmat)
    gc_dirs = (gcf_all, (gtot_all - gcf_all) + gmat)
    incl_dirs = (same & (ci <= ri), same & (ci >= ri))
    strict_dirs = (same & (ci < ri), same & (ci > ri))

    lmats, rhss, dests = [], [], []
    for h in range(GDN_HEADS):
        cols = slice(h * LANE, (h + 1) * LANE)
        q = q_ref[:, cols]
        k = k_ref[:, cols]
        v = v_ref[:, cols]
        kk = _mm(k, k, _dot_nt)
        qk = _mm(q, k, _dot_nt)
        for d in range(2):
            incl = incl_dirs[d]
            sel_g = lane == d * GDN_HEADS + h
            sel_b = lane == 2 * GDN_HEADS + d * GDN_HEADS + h
            gc = jnp.sum(jnp.where(sel_g, gc_dirs[d], 0.0), axis=1, keepdims=True)
            gtot = jnp.sum(jnp.where(sel_g, gtot_all, 0.0), axis=1, keepdims=True)
            beta = jnp.sum(jnp.where(sel_b, bmat, 0.0), axis=1, keepdims=True)
            gc_b = jnp.broadcast_to(gc, (n, LANE))
            gc_row = jnp.transpose(gc_b)[0:1, :]
            diff = jnp.where(incl, gc - gc_row, 0.0)
            decay = jnp.where(incl, jnp.exp(diff), 0.0)
            eg = jnp.exp(gc_b)
            lmats.append(jnp.where(strict_dirs[d], kk * beta * decay, 0.0))
            rhss.append(jnp.concatenate([v * beta, (k * beta) * eg], axis=1))
            dests.append((d, cols))
            qg_ref[d, :, cols] = (q * eg).astype(qg_ref.dtype)
            kd_ref[d, :, cols] = (k * jnp.exp(jnp.broadcast_to(gtot - gc, (n, LANE)))).astype(kd_ref.dtype)
            amat = qk * decay
            a_ref[d, h] = ((amat[:, 0:CHUNK] + amat[:, CHUNK:2 * CHUNK]) + (
                amat[:, 2 * CHUNK:3 * CHUNK] + amat[:, 3 * CHUNK:4 * CHUNK])).astype(a_ref.dtype)
            egt = jnp.exp(jnp.broadcast_to(gtot, (n, LANE)))
            gl8 = jnp.zeros((8, LANE), F32)
            for c in range(n // CHUNK):
                gl8 = jnp.where(r8 == c, egt[c * CHUNK:c * CHUNK + 8, :], gl8)
            gl_ref[d, :, cols] = gl8

    for (d, cols), uw in zip(dests, _unit_tri_solve_many(lmats, rhss, eye)):
        u_ref[d, :, cols] = uw[:, :GDN_DV]
        w_ref[d, :, cols] = uw[:, GDN_DV:].astype(w_ref.dtype)


def _gdn_intra(qkv, proj_ab, alog_row, dt_row, row0):
    t = qkv.shape[0]
    nb = t // GDN_BLK
    pb0 = row0 // GDN_BLK
    big = lambda dt: jax.ShapeDtypeStruct((2, t, GDN_W), dt)
    big_spec = pl.BlockSpec((2, GDN_BLK, GDN_W), lambda i: (0, i, 0))
    return pl.pallas_call(
        _gdn_intra_kernel,
        grid=(nb,),
        in_specs=[
            pl.BlockSpec((GDN_BLK, GDN_W), lambda i: (i, 0)),
            pl.BlockSpec((GDN_BLK, GDN_W), lambda i: (i, 1)),
            pl.BlockSpec((GDN_BLK, GDN_W), lambda i: (i, 2)),
            pl.BlockSpec((GDN_BLK, LANE), lambda i: (pb0 + i, 0)),
            pl.BlockSpec((1, LANE), lambda i: (0, 0)),
            pl.BlockSpec((1, LANE), lambda i: (0, 0)),
        ],
        out_specs=[big_spec, big_spec, big_spec, big_spec,
                   pl.BlockSpec((2, GDN_HEADS, GDN_BLK, CHUNK), lambda i: (0, 0, i, 0)),
                   pl.BlockSpec((2, 8, GDN_W), lambda i: (0, i, 0))],
        out_shape=[big(F32), big(BF16), big(BF16), big(BF16),
                   jax.ShapeDtypeStruct((2, GDN_HEADS, t, CHUNK), BF16),
                   jax.ShapeDtypeStruct((2, nb * 8, GDN_W), F32)],
        compiler_params=_cparams("parallel"),
        name="gdn_intra",
    )(qkv, qkv, qkv, proj_ab, alog_row, dt_row)


def _gdn_scan_kernel(*refs, has_init):
    if has_init:
        s0_ref, refs = refs[0], refs[1:]
    ins = (refs[0:6], refs[6:12])
    o_refs, sf_ref, s_ref = refs[12:14], refs[14], refs[15]
    blk = pl.program_id(1)
    nchunk = GDN_BLK // CHUNK

    @pl.when(blk == 0)
    def _():
        if has_init:
            s_ref[...] = s0_ref[...]
        else:
            s_ref[...] = jnp.zeros_like(s_ref)

    U, W, QG, KD, A, GL = range(6)
    chains = [(d, h) for d in range(2) for h in range(GDN_HEADS)]
    states = [s_ref[d, h] for d, h in chains]
    for c in range(nchunk):
        ccs = [c if d == 0 else nchunk - 1 - c for d, _ in chains]
        rows = [slice(cc * CHUNK, (cc + 1) * CHUNK) for cc in ccs]
        cols = [slice(h * LANE, (h + 1) * LANE) for _, h in chains]
        n = range(len(chains))
        ws = [_mm(ins[chains[i][0]][W][rows[i], cols[i]], states[i]) for i in n]
        qs = [_mm(ins[chains[i][0]][QG][rows[i], cols[i]], states[i]) for i in n]
        v_news = [ins[chains[i][0]][U][rows[i], cols[i]] - ws[i] for i in n]
        for i in n:
            d, h = chains[i]
            o_refs[d][rows[i], cols[i]] = qs[i] + _mm(ins[d][A][h, rows[i], :], v_news[i])
        states = [states[i] * ins[chains[i][0]][GL][ccs[i]:ccs[i] + 1, cols[i]]
                  + _mm(ins[chains[i][0]][KD][rows[i], cols[i]], v_news[i], _dot_tn) for i in n]
    for (d, h), s in zip(chains, states):
        s_ref[d, h] = s

    @pl.when(blk == pl.num_programs(1) - 1)
    def _():
        sf_ref[...] = s_ref[...]


def _gdn_scan(s0, u, w, qg, kd, a, gl, nseq, seqlen):
    t = u.shape[1]
    nblk = seqlen // GDN_BLK
    in_specs, args = [], []
    for d in range(2):
        rb = (lambda s, b: s * nblk + b) if d == 0 else (lambda s, b: s * nblk + nblk - 1 - b)
        big = pl.BlockSpec((None, GDN_BLK, GDN_W), lambda s, b, d=d, rb=rb: (d, rb(s, b), 0))
        in_specs += [big, big, big, big,
                     pl.BlockSpec((None, GDN_HEADS, GDN_BLK, CHUNK), lambda s, b, d=d, rb=rb: (d, 0, rb(s, b), 0)),
                     pl.BlockSpec((None, 8, GDN_W), lambda s, b, d=d, rb=rb: (d, rb(s, b), 0))]
        args += [u, w, qg, kd, a, gl]
    state_spec = pl.BlockSpec((None, 2, GDN_HEADS, GDN_DK, GDN_DV), lambda s, b: (s, 0, 0, 0, 0))
    if s0 is not None:
        in_specs.insert(0, state_spec)
        args.insert(0, s0)
    out_specs = [pl.BlockSpec((GDN_BLK, GDN_W), lambda s, b: (s * nblk + b, 0)),
                 pl.BlockSpec((GDN_BLK, GDN_W), lambda s, b: (s * nblk + nblk - 1 - b, 0)),
                 state_spec]
    return pl.pallas_call(
        functools.partial(_gdn_scan_kernel, has_init=s0 is not None),
        grid=(nseq, nblk),
        in_specs=in_specs,
        out_specs=out_specs,
        out_shape=[jax.ShapeDtypeStruct((t, GDN_W), F32), jax.ShapeDtypeStruct((t, GDN_W), F32),
                   jax.ShapeDtypeStruct((nseq, 2, GDN_HEADS, GDN_DK, GDN_DV), F32)],
        scratch_shapes=[pltpu.VMEM((2, GDN_HEADS, GDN_DK, GDN_DV), F32)],
        compiler_params=_cparams("parallel", "arbitrary"),
        name="gdn_scan",
    )(*args)


def _gdn_post_kernel(of_ref, ob_ref, z_ref, g_ref, out_ref):
    for h in range(GDN_HEADS):
        cols = slice(h * LANE, (h + 1) * LANE)
        o = of_ref[:, cols] + ob_ref[:, cols]
        y = o * lax.rsqrt(jnp.mean(o * o, axis=-1, keepdims=True) + EPS) * g_ref[...]
        out_ref[:, cols] = (y * _silu(z_ref[:, cols])).astype(out_ref.dtype)


def _gdn_post(o_f, o_b, proj, g, row0):
    t = o_f.shape[0]
    tm = _row_tile((t, row0 + t), 1024)
    pb0 = row0 // tm
    return pl.pallas_call(
        _gdn_post_kernel,
        grid=(t // tm,),
        in_specs=[
            pl.BlockSpec((tm, GDN_W), lambda i: (i, 0)),
            pl.BlockSpec((tm, GDN_W), lambda i: (i, 0)),
            pl.BlockSpec((tm, GDN_W), lambda i: (pb0 + i, CB_Z // GDN_HEADS)),
            pl.BlockSpec((1, GDN_DV), lambda i: (0, 0)),
        ],
        out_specs=pl.BlockSpec((tm, GDN_W), lambda i: (i, 0)),
        out_shape=jax.ShapeDtypeStruct((t, GDN_W), BF16),
        compiler_params=_cparams("parallel"),
        name="gdn_post",
    )(o_f, o_b, proj, g.reshape(1, GDN_DV))


def _rope_rotate(x, cos, sin_signed):
    lane = lax.broadcasted_iota(jnp.int32, x.shape, 1)
    first = (lane % 32) < 16
    partner = jnp.where(first, pltpu.roll(x, LANE - 16, 1), pltpu.roll(x, 16, 1))
    return x * cos + partner * sin_signed


def _rope_tables(seqlen):
    pos = jnp.arange(seqlen)
    rowp = (pos // GRID_W).astype(F32)
    colp = (pos % GRID_W).astype(F32)
    half = DIFF_DQK // 2
    inv_freq = ROPE_BASE ** (-jnp.arange(0, half, 2, dtype=F32) / half)
    c = jnp.arange(LANE)
    c64 = c % DIFF_DQK
    p = jnp.where((c64 // half)[None, :] == 0, rowp[:, None], colp[:, None])
    ang = p * inv_freq[c64 % (half // 2)][None, :]
    sign = jnp.where((c64 % half) < half // 2, -1.0, 1.0).astype(F32)
    return jnp.cos(ang), jnp.sin(ang) * sign[None, :]


def _diff_attn_kernel(*refs, seg_lens, lam_init, rope, hp, n_alias, emit_cache):
    nseg = len(seg_lens)
    q_ref = refs[0]
    k_refs = refs[1:1 + nseg]
    v_refs = refs[1 + nseg:1 + 2 * nseg]
    pos = 1 + 2 * nseg
    if rope:
        cosq_ref, sinq_ref, cosk_ref, sink_ref = refs[pos:pos + 4]
        pos += 4
    lam_ref, g_ref = refs[pos:pos + 2]
    pos += 2 + n_alias
    o_ref = refs[pos]
    if emit_cache:
        kc_ref, vc_ref = refs[pos + 1:pos + 3]
    kb_ref, vb_ref = refs[-2:]
    offs = [sum(seg_lens[:s]) for s in range(nseg)]
    hcols = [slice(hh * DIFF_DV, (hh + 1) * DIFF_DV) for hh in range(hp)]

    @pl.when(pl.program_id(2) == 0)
    def _():
        if emit_cache:
            kc_ref[...] = k_refs[-1][...]
            vc_ref[...] = v_refs[-1][...]
        for hh in range(hp):
            vb_ref[:, (2 * hh + 1) * DIFF_DV:(2 * hh + 2) * DIFF_DV] = jnp.ones((vb_ref.shape[0], DIFF_DV), BF16)
            for s in range(nseg):
                rows = slice(offs[s], offs[s] + seg_lens[s])
                k = k_refs[s][:, hcols[hh]]
                if rope and s == nseg - 1:
                    k = _rope_rotate(k, cosk_ref[...], sink_ref[...])
                kb_ref[rows, hcols[hh]] = k.astype(BF16)
                vb_ref[rows, 2 * hh * DIFF_DV:(2 * hh + 1) * DIFF_DV] = v_refs[s][:, hcols[hh]].astype(BF16)

    lv = lam_ref[...]
    lam = (jnp.exp(jnp.sum(lv[0:1] * lv[1:2], axis=1, keepdims=True))
           - jnp.exp(jnp.sum(lv[2:3] * lv[3:4], axis=1, keepdims=True)) + lam_init)
    lane = lax.broadcasted_iota(jnp.int32, (ATTN_SUB, DIFF_DV), 1)

    def scores(hh, rows):
        q = q_ref[rows, hcols[hh]]
        if rope:
            q = _rope_rotate(q, cosq_ref[rows, :], sinq_ref[rows, :])
        q = q * (DIFF_DQK ** -0.5)
        kb = kb_ref[:, hcols[hh]]
        return [_dot_nt(jnp.where((lane // DIFF_DQK) == m, q, 0.0).astype(BF16), kb) for m in range(2)]

    units = [(hh, slice(r * ATTN_SUB, (r + 1) * ATTN_SUB))
             for hh in range(hp) for r in range(q_ref.shape[0] // ATTN_SUB)]
    ss_next = scores(*units[0])
    for n, (hh, rows) in enumerate(units):
        ss = ss_next
        if n + 1 < len(units):
            ss_next = scores(*units[n + 1])
        vb = vb_ref[:, 2 * hh * DIFF_DV:(2 * hh + 2) * DIFF_DV]
        es = [jnp.exp(s - jnp.max(s, axis=1, keepdims=True)).astype(BF16) for s in ss]
        accs = [_dot(e, vb) for e in es]
        outs = [acc[:, 0:DIFF_DV] / acc[:, DIFF_DV:] for acc in accs]
        o = outs[0] - lam * outs[1]
        y = o * lax.rsqrt(jnp.mean(o * o, axis=-1, keepdims=True) + EPS) * g_ref[...]
        o_ref[rows, hcols[hh]] = (y * (1.0 - lam_init)).astype(o_ref.dtype)


def _diff_attn(proj, row0, nseq, seqlen, past_kv, l, lam_vec, g, cache_out):
    tq = min(4 * ATTN_SUB, seqlen)
    hp = min(DIFF_HEADS, max(1, ATTN_STEP_ROWS // seqlen))
    nqb = seqlen // tq
    qb0 = row0 // tq
    sb0 = row0 // seqlen
    wcols = hp * DIFF_DV
    q_spec = pl.BlockSpec((tq, wcols), lambda b, h, i: (qb0 + b * nqb + i, CB_DQ // hp + h))
    k_new = pl.BlockSpec((seqlen, wcols), lambda b, h, i: (sb0 + b, CB_DK // hp + h))
    v_new = pl.BlockSpec((seqlen, wcols), lambda b, h, i: (sb0 + b, CB_DV // hp + h))
    arrs, specs, seg_lens = [proj], [q_spec], [seqlen]
    if past_kv is None:
        arrs += [proj, proj]
        specs += [k_new, v_new]
    else:
        past = past_kv[0].shape[2]
        cached = pl.BlockSpec((None, None, past, wcols), lambda b, h, i: (b, l, 0, h))
        arrs += [past_kv[0], proj, past_kv[1], proj]
        specs += [cached, k_new, cached, v_new]
        seg_lens = [past, seqlen]
        cos, sin = _rope_tables(seqlen)
        arrs += [cos, sin, cos, sin]
        specs += [pl.BlockSpec((tq, DIFF_DV), lambda b, h, i: (i, 0))] * 2 + [
            pl.BlockSpec((seqlen, DIFF_DV), lambda b, h, i: (0, 0))] * 2
    arrs += [lam_vec, g.reshape(1, DIFF_DV)]
    specs += [pl.BlockSpec((4, DIFF_DQK), lambda b, h, i: (0, 0)),
              pl.BlockSpec((1, DIFF_DV), lambda b, h, i: (0, 0))]
    out_specs = [pl.BlockSpec((tq, wcols), lambda b, h, i: (b * nqb + i, h))]
    out_shape = [jax.ShapeDtypeStruct((nseq * seqlen, DIFF_W), BF16)]
    aliases, n_alias = {}, 0
    if cache_out is not None:
        depth, prev = cache_out
        entry = pl.BlockSpec((None, None, seqlen, wcols), lambda b, h, i: (b, l, 0, h))
        out_specs += [entry, entry]
        out_shape += [jax.ShapeDtypeStruct((nseq, depth, seqlen, DIFF_W), F32)] * 2
        if prev is not None:
            aliases = {len(arrs): 1, len(arrs) + 1: 2}
            n_alias = 2
            arrs += list(prev)
            specs += [pl.BlockSpec(memory_space=pl.ANY)] * 2
    lk = sum(seg_lens)
    outs = pl.pallas_call(
        functools.partial(_diff_attn_kernel, seg_lens=tuple(seg_lens), lam_init=_lambda_init(l),
                          rope=past_kv is not None, hp=hp, n_alias=n_alias, emit_cache=cache_out is not None),
        grid=(nseq, DIFF_HEADS // hp, nqb),
        in_specs=specs,
        out_specs=out_specs,
        out_shape=out_shape,
        input_output_aliases=aliases,
        scratch_shapes=[pltpu.VMEM((lk, wcols), BF16), pltpu.VMEM((lk, 2 * wcols), BF16)],
        compiler_params=_cparams("parallel", "parallel", "arbitrary"),
        name="diff_attn",
    )(*arrs)
    return outs[0], (tuple(outs[1:]) if cache_out is not None else None)


def _pool_kernel(x_ref, w_ref, sc_ref, o_ref):
    n = x_ref.shape[0]
    row = lax.broadcasted_iota(jnp.int32, (n, POOL_GW), 0)
    for gi, win in enumerate(POOL_WINDOWS):
        cols = slice(gi * POOL_GW, (gi + 1) * POOL_GW)
        x = x_ref[:, cols]
        a = win // 2
        b = win - a - 1
        acc = x
        for off in range(-a, b + 1):
            if off != 0:
                acc = acc + _shift_rows(x, off, row)
        cnt = (jnp.minimum(row + b + 1, n) - jnp.maximum(row - a, 0)).astype(F32)
        pooled = acc / cnt - x
        y = _dot(pooled.astype(BF16), w_ref[gi].astype(BF16))
        o_ref[:, cols] = (y * sc_ref[:, cols]).astype(o_ref.dtype)


def _pool(proj, pool_w, pool_scale, row0, nseq, seqlen):
    blk0 = row0 // seqlen
    return pl.pallas_call(
        _pool_kernel,
        grid=(nseq,),
        in_specs=[
            pl.BlockSpec((seqlen, POOL_W), lambda s: (blk0 + s, CB_PIN // len(POOL_WINDOWS))),
            pl.BlockSpec((len(POOL_WINDOWS), POOL_GW, POOL_GW), lambda s: (0, 0, 0)),
            pl.BlockSpec((1, POOL_W), lambda s: (0, 0)),
        ],
        out_specs=pl.BlockSpec((seqlen, POOL_W), lambda s: (s, 0)),
        out_shape=jax.ShapeDtypeStruct((nseq * seqlen, POOL_W), BF16),
        compiler_params=_cparams("parallel"),
        name="pool",
    )(proj, pool_w, pool_scale.reshape(1, POOL_W))


def _mix_group(proj, proj_ab, lp, l, row0, nseq, seqlen, ctx, cache_out):
    qkv = _gdn_prep(proj, lp["gdn_conv"], row0, nseq, seqlen)
    u, w, qg, kd, a, gl = _gdn_intra(qkv, proj_ab, lp["alog_row"], lp["dt_row"], row0)
    o_f, o_b, s_fin = _gdn_scan(None if ctx is None else ctx[2], u, w, qg, kd, a, gl, nseq, seqlen)
    o_gdn = _gdn_post(o_f, o_b, proj, lp["gdn_norm"], row0)

    o_diff, caches = _diff_attn(proj, row0, nseq, seqlen, None if ctx is None else ctx[:2], l,
                                lp["diff_lam"], lp["diff_norm"], cache_out)
    o_pool = _pool(proj, lp["pool_w"], lp["pool_scale"], row0, nseq, seqlen)
    return (o_gdn, o_diff, o_pool), s_fin, caches


def _split_w_in(w_in):
    depth, d, _ = w_in.shape
    o_ab = 4 * GDN_W
    o_dq = o_ab + 4 * GDN_HEADS
    pad = jnp.zeros((depth, d, LANE - 4 * GDN_HEADS), w_in.dtype)
    ab = jnp.concatenate([w_in[..., o_ab:o_dq], pad], axis=-1)
    return w_in[..., o_dq:].astype(BF16), ab.astype(BF16)


def kernel(x_prompt, x_sample, c, cache_k, cache_v, state_gdn, c_ctx, w_ada, b_ada, norm_ffn1, ffn1_in,
           ffn1_out, norm_mix, w_in, gdn_conv, gdn_a_log, gdn_dt_bias, gdn_norm, diff_lam, diff_norm,
           pool_w, pool_scale, w_out, norm_ffn2, ffn2_in, ffn2_out, final_norm):
    batch, seq, d = x_prompt.shape
    dec_batch, dec_seq, _ = x_sample.shape
    depth = w_ada.shape[0]
    n_ctx = batch * seq
    n_dec = dec_batch * dec_seq
    layout = (n_ctx, dec_seq)
    past = cache_k.shape[2]

    x = (x_prompt.reshape(n_ctx, d), x_sample.reshape(n_dec, d))
    cond8 = jnp.concatenate([c_ctx[None, :], c, jnp.zeros((8 - 1 - dec_batch, d), F32)], axis=0)
    ada = _ada(cond8, w_ada, b_ada)
    cache_k4 = cache_k.reshape(dec_batch, depth, past, DIFF_W)
    cache_v4 = cache_v.reshape(dec_batch, depth, past, DIFF_W)
    w_in_hi, w_in_ab = _split_w_in(w_in)
    w_out_b = w_out.astype(BF16)
    ffn_w = [(ffn1_in, ffn1_out.astype(BF16)), (ffn2_in, ffn2_out.astype(BF16))]

    caches, ss = None, []
    for l in range(depth):
        ada3 = ada[l, :1 + dec_batch].reshape(1 + dec_batch, 1, N_MOD * d)
        lane_pad = jnp.zeros((LANE - 2 * GDN_HEADS,), F32)
        lp = dict(
            gdn_conv=gdn_conv[l], gdn_norm=gdn_norm[l], diff_lam=diff_lam[l], diff_norm=diff_norm[l],
            pool_w=pool_w[l], pool_scale=pool_scale[l],
            alog_row=jnp.concatenate([gdn_a_log[l].reshape(-1), lane_pad]).reshape(1, LANE),
            dt_row=jnp.concatenate([gdn_dt_bias[l].reshape(-1), lane_pad]).reshape(1, LANE),
        )
        act = _ffn_in(x, ada3, (0, 1), norm_ffn1[l], ffn_w[0][0], l, layout=layout)
        x = _ffn_out(act, ffn_w[0][1], l, x, ada3, 2, layout=layout)

        proj, proj_ab = _mix_in(x, ada3, (3, 4), norm_mix[l], w_in, w_in_hi, w_in_ab, l, layout=layout)
        acts_c, s_c, caches = _mix_group(proj, proj_ab, lp, l, 0, batch, seq, None, (depth, caches))
        acts_d, _, _ = _mix_group(proj, proj_ab, lp, l, n_ctx, dec_batch, dec_seq,
                                  (cache_k4, cache_v4, state_gdn[:, l]), None)
        x = _mix_out(acts_c, w_out_b, l, x, ada3, 5, 0, layout=layout)
        x = _mix_out(acts_d, w_out_b, l, x, ada3, 5, n_ctx, layout=layout)
        ss.append(s_c)

        act = _ffn_in(x, ada3, (6, 7), norm_ffn2[l], ffn_w[1][0], l, layout=layout)
        x = _ffn_out(act, ffn_w[1][1], l, x, ada3, 8, layout=layout)

    y_prompt = _final_norm(x, final_norm, 0, n_ctx).reshape(batch, seq, d)
    y_sample = _final_norm(x, final_norm, n_ctx, n_dec).reshape(dec_batch, dec_seq, d)
    new_k, new_v = (t.reshape(batch, depth, seq, DIFF_HEADS, DIFF_DV) for t in caches)
    return (y_prompt, y_sample, new_k, new_v, jnp.stack(ss, axis=1))
```

```python
import functools
import math

import jax
import jax.numpy as jnp
from jax import lax
from jax.experimental import pallas as pl
from jax.experimental.pallas import tpu as pltpu

F32 = jnp.float32
BF16 = jnp.bfloat16

GRID_W = 64
GDN_HEADS = 4
GDN_DK = 128
GDN_DV = 128
GDN_W = GDN_HEADS * GDN_DV
CONV_K = 4
CHUNK = 64
DIFF_HEADS = 8
DIFF_DQK = 64
DIFF_DV = 2 * DIFF_DQK
DIFF_W = DIFF_HEADS * DIFF_DV
ROPE_BASE = 10000.0
POOL_WINDOWS = (2, 4, 8, 16)
POOL_GW = 128
POOL_W = len(POOL_WINDOWS) * POOL_GW
MIX_W = GDN_W + DIFF_W + POOL_W
N_MOD = 9
EPS = 1e-6

LANE = 128
GDN_BLK = 4 * CHUNK
ATTN_SUB = 256
ATTN_STEP_ROWS = 2048
PRO_SUB = 256
PREP_STEP_ROWS = 1024
VMEM_LIMIT = 60 * 1024 * 1024

CB_Q, CB_K, CB_V, CB_Z = 0, 4, 8, 12
CB_DQ, CB_DK, CB_DV, CB_PIN = 16, 24, 32, 40
PROJ_MAIN = 44 * LANE


def _lambda_init(l):
    return 0.8 - 0.6 * math.exp(-0.3 * l)


def _cparams(*sem):
    return pltpu.CompilerParams(dimension_semantics=sem, vmem_limit_bytes=VMEM_LIMIT)


def _dot(a, b):
    return lax.dot_general(a, b, (((1,), (0,)), ((), ())), preferred_element_type=F32)


def _dot_nt(a, b):
    return lax.dot_general(a, b, (((1,), (1,)), ((), ())), preferred_element_type=F32)


def _dot_tn(a, b):
    return lax.dot_general(a, b, (((0,), (0,)), ((), ())), preferred_element_type=F32)


def _mm(a, b, dot=_dot):
    return dot(a.astype(BF16), b.astype(BF16))


def _dot_exact_lhs(m, b):
    b0 = b.astype(BF16)
    r = b - b0.astype(F32)
    b1 = r.astype(BF16)
    b2 = (r - b1.astype(F32)).astype(BF16)
    return _dot(m, b0) + (_dot(m, b1) + _dot(m, b2))


def _sigmoid(x):
    return 0.5 * (1.0 + jnp.tanh(0.5 * x))


def _silu(x):
    return x * _sigmoid(x)


def _ada_kernel(cond_ref, w_ref, b_ref, o_ref):
    s = _silu(cond_ref[...]).astype(BF16)
    o_ref[...] = _dot(s, w_ref[...].astype(BF16)) + b_ref[...]


def _ada(cond8, w_ada, b_ada):
    depth, d, n = w_ada.shape
    tn = 1024
    return pl.pallas_call(
        _ada_kernel,
        grid=(depth, n // tn),
        in_specs=[
            pl.BlockSpec((8, d), lambda l, j: (0, 0)),
            pl.BlockSpec((None, d, tn), lambda l, j: (l, 0, j)),
            pl.BlockSpec((None, 1, tn), lambda l, j: (l, 0, j)),
        ],
        out_specs=pl.BlockSpec((None, 8, tn), lambda l, j: (l, 0, j)),
        out_shape=jax.ShapeDtypeStruct((depth, 8, n), F32),
        compiler_params=_cparams("parallel", "parallel"),
        name="ada",
    )(cond8, w_ada, b_ada.reshape(depth, 1, n))


def _norm_mod(x, g, shift, scale):
    y = x * lax.rsqrt(jnp.mean(x * x, axis=-1, keepdims=True) + EPS) * g
    return y * (1.0 + scale) + shift


def _read_x(x_refs, n_first, idx):
    if len(x_refs) == 1:
        return x_refs[0][idx]
    return jnp.where(pl.program_id(0) < n_first, x_refs[0][idx], x_refs[1][idx])


def _norm_mod_rows(x_refs, n_first, sh_ref, sc_ref, g_ref, rows):
    x = _read_x(x_refs, n_first, (rows, slice(None)))
    return _norm_mod(x, g_ref[...], sh_ref[0], sc_ref[0]).astype(BF16)


def _row_subs(n):
    return [slice(r * PRO_SUB, (r + 1) * PRO_SUB) for r in range(n // PRO_SUB)]


def _mix_in_kernel(x_ref, sh_ref, sc_ref, g_ref, w_ref, wab_ref, o_ref, ab_ref, h_ref):
    @pl.when(pl.program_id(1) == 0)
    def _():
        for rows in _row_subs(x_ref.shape[0]):
            h = _norm_mod_rows((x_ref,), None, sh_ref, sc_ref, g_ref, rows)
            h_ref[rows, :] = h
            ab_ref[rows, :] = _dot(h, wab_ref[...])
            o_ref[rows, :] = _dot(h, w_ref[...])

    @pl.when(pl.program_id(1) != 0)
    def _():
        o_ref[...] = _dot(h_ref[...], w_ref[...])


def _ffn_in_kernel(*refs, n_first):
    x_refs = refs[:-7]
    sh_ref, sc_ref, g_ref, wg_ref, wu_ref, o_ref, h_ref = refs[-7:]

    def swiglu(h):
        gate = _dot(h, wg_ref[...].astype(BF16))
        up = _dot(h, wu_ref[...].astype(BF16))
        return (_silu(gate) * up).astype(o_ref.dtype)

    @pl.when(pl.program_id(1) == 0)
    def _():
        for rows in _row_subs(h_ref.shape[0]):
            h = _norm_mod_rows(x_refs, n_first, sh_ref, sc_ref, g_ref, rows)
            h_ref[rows, :] = h
            o_ref[rows, :] = swiglu(h)

    @pl.when(pl.program_id(1) != 0)
    def _():
        o_ref[...] = swiglu(h_ref[...])


def _row_tile(groups, cap):
    tm = cap
    while any(g % tm for g in groups):
        tm //= 2
    return tm


def _mod_row(n_ctx, dec_seq, tm, row0=0):
    def f(i):
        r = row0 + i * tm
        return jnp.where(r < n_ctx, 0, 1 + (r - n_ctx) // dec_seq)
    return f


def _x_parts(x, tm):
    if isinstance(x, tuple):
        return list(x), x[0].shape[0] // tm, sum(p.shape[0] for p in x)
    return [x], None, x.shape[0]


def _x_specs(n_first, tm, tn):
    if n_first is None:
        return [pl.BlockSpec((tm, tn), lambda i, j: (i, j))]
    return [pl.BlockSpec((tm, tn), lambda i, j: (jnp.where(i < n_first, i, n_first - 1), jnp.where(i < n_first, j, 0))),
            pl.BlockSpec((tm, tn), lambda i, j: (jnp.where(i < n_first, 0, i - n_first), jnp.where(i < n_first, 0, j)))]


def _pm_specs(d, tm, row, mod_cols, n_first=None):
    sh_c, sc_c = mod_cols
    x_specs = [pl.BlockSpec((tm, d), lambda i, j: (i, 0))] if n_first is None else [
        pl.BlockSpec((tm, d), lambda i, j: (jnp.minimum(i, n_first - 1), 0)),
        pl.BlockSpec((tm, d), lambda i, j: (jnp.maximum(i - n_first, 0), 0))]
    return x_specs + [
        pl.BlockSpec((1, 1, d), lambda i, j: (row(i), 0, sh_c)),
        pl.BlockSpec((1, 1, d), lambda i, j: (row(i), 0, sc_c)),
        pl.BlockSpec((1, d), lambda i, j: (0, 0)),
    ]


def _ffn_in(x, ada3, mod_cols, g, w, l, *, layout):
    d = w.shape[1]
    n_ctx, dec_seq = layout
    tm = _row_tile((n_ctx, dec_seq), 1024)
    xs, n_first, t = _x_parts(x, tm)
    row = _mod_row(n_ctx, dec_seq, tm)
    n = w.shape[2] // 2
    tn = 512 if n % 512 == 0 else n
    nj = n // tn
    return pl.pallas_call(
        functools.partial(_ffn_in_kernel, n_first=n_first),
        grid=(t // tm, nj),
        in_specs=_pm_specs(d, tm, row, mod_cols, n_first) + [
            pl.BlockSpec((None, d, tn), lambda i, j: (l, 0, j)),
            pl.BlockSpec((None, d, tn), lambda i, j: (l, 0, j + nj)),
        ],
        out_specs=pl.BlockSpec((tm, tn), lambda i, j: (i, j)),
        out_shape=jax.ShapeDtypeStruct((t, n), BF16),
        scratch_shapes=[pltpu.VMEM((tm, d), BF16)],
        compiler_params=_cparams("parallel", "arbitrary"),
        name="ffn_in",
    )(*xs, ada3, ada3, g.reshape(1, d), w, w)


def _mix_in(x, ada3, mod_cols, g, w_main, w_ab, l, *, layout):
    t, d = x.shape
    n_ctx, dec_seq = layout
    tm = _row_tile((n_ctx, dec_seq), 1024)
    row = _mod_row(n_ctx, dec_seq, tm)
    tn = 512
    return pl.pallas_call(
        _mix_in_kernel,
        grid=(t // tm, PROJ_MAIN // tn),
        in_specs=_pm_specs(d, tm, row, mod_cols) + [
            pl.BlockSpec((None, d, tn), lambda i, j: (l, 0, j)),
            pl.BlockSpec((None, d, LANE), lambda i, j: (l, 0, 0)),
        ],
        out_specs=[pl.BlockSpec((tm, tn), lambda i, j: (i, j)),
                   pl.BlockSpec((tm, LANE), lambda i, j: (i, 0))],
        out_shape=[jax.ShapeDtypeStruct((t, PROJ_MAIN), F32), jax.ShapeDtypeStruct((t, LANE), F32)],
        scratch_shapes=[pltpu.VMEM((tm, d), BF16)],
        compiler_params=_cparams("parallel", "arbitrary"),
        name="mix_in",
    )(x, ada3, ada3, g.reshape(1, d), w_main, w_ab)


def _ffn_out_kernel(*refs, n_first):
    a_ref, w_ref = refs[:2]
    x_refs = refs[2:-2]
    gate_ref, o_ref = refs[-2:]
    y = _dot(a_ref[...], w_ref[...])
    o_ref[...] = _read_x(x_refs, n_first, Ellipsis) + (0.5 * gate_ref[0]) * y


def _ffn_out(a, w, l, x, ada3, gate_col, *, layout):
    t, k = a.shape
    d = w.shape[2]
    n_ctx, dec_seq = layout
    tm = _row_tile((n_ctx, dec_seq), 1024)
    tn = min(512, d)
    xs, n_first, _ = _x_parts(x, tm)
    row = _mod_row(n_ctx, dec_seq, tm)
    gpb = d // tn
    return pl.pallas_call(
        functools.partial(_ffn_out_kernel, n_first=n_first),
        grid=(t // tm, d // tn),
        in_specs=[
            pl.BlockSpec((tm, k), lambda i, j: (i, 0)),
            pl.BlockSpec((None, k, tn), lambda i, j: (l, 0, j)),
            *_x_specs(n_first, tm, tn),
            pl.BlockSpec((1, 1, tn), lambda i, j: (row(i), 0, gate_col * gpb + j)),
        ],
        out_specs=pl.BlockSpec((tm, tn), lambda i, j: (i, j)),
        out_shape=jax.ShapeDtypeStruct((t, d), F32),
        compiler_params=_cparams("parallel", "parallel"),
        name="ffn_out",
    )(a, w, *xs, ada3)


def _mix_out_kernel(of_ref, ob_ref, z_ref, gn_ref, a2_ref, a3_ref, w_ref, x_ref, gate_ref, o_ref, a1_ref):
    @pl.when(pl.program_id(1) == 0)
    def _():
        for h in range(GDN_HEADS):
            cols = slice(h * LANE, (h + 1) * LANE)
            o = of_ref[:, cols] + ob_ref[:, cols]
            y = o * lax.rsqrt(jnp.mean(o * o, axis=-1, keepdims=True) + EPS) * gn_ref[...]
            a1_ref[:, cols] = (y * _silu(z_ref[:, cols])).astype(BF16)

    y = (_dot(a1_ref[...], w_ref[0:GDN_W, :])
         + _dot(a2_ref[...], w_ref[GDN_W:GDN_W + DIFF_W, :])
         + _dot(a3_ref[...], w_ref[GDN_W + DIFF_W:MIX_W, :]))
    o_ref[...] = x_ref[...] + gate_ref[0] * y


def _mix_out(acts, proj, gn, w, l, x, ada3, gate_col, row0, *, layout):
    o_f, o_b, a2, a3 = acts
    n = a2.shape[0]
    t, d = x.shape
    n_ctx, dec_seq = layout
    tm = _row_tile((n_ctx, dec_seq), 1024)
    tn = min(512, d)
    row = _mod_row(n_ctx, dec_seq, tm, row0)
    rb0 = row0 // tm
    gpb = d // tn
    return pl.pallas_call(
        _mix_out_kernel,
        grid=(n // tm, d // tn),
        in_specs=[
            pl.BlockSpec((tm, GDN_W), lambda i, j: (i, 0)),
            pl.BlockSpec((tm, GDN_W), lambda i, j: (i, 0)),
            pl.BlockSpec((tm, GDN_W), lambda i, j: (rb0 + i, CB_Z // GDN_HEADS)),
            pl.BlockSpec((1, GDN_DV), lambda i, j: (0, 0)),
            pl.BlockSpec((tm, DIFF_W), lambda i, j: (i, 0)),
            pl.BlockSpec((tm, POOL_W), lambda i, j: (i, 0)),
            pl.BlockSpec((None, MIX_W, tn), lambda i, j: (l, 0, j)),
            pl.BlockSpec((tm, tn), lambda i, j: (rb0 + i, j)),
            pl.BlockSpec((1, 1, tn), lambda i, j: (row(i), 0, gate_col * gpb + j)),
        ],
        out_specs=pl.BlockSpec((tm, tn), lambda i, j: (rb0 + i, j)),
        out_shape=jax.ShapeDtypeStruct((t, d), F32),
        input_output_aliases={7: 0},
        scratch_shapes=[pltpu.VMEM((tm, GDN_W), BF16)],
        compiler_params=_cparams("parallel", "arbitrary"),
        name="mix_out",
    )(o_f, o_b, proj, gn.reshape(1, GDN_DV), a2, a3, w, x, ada3)


def _final_norm_kernel(x_ref, g_ref, o_ref):
    x = x_ref[...]
    o_ref[...] = x * lax.rsqrt(jnp.mean(x * x, axis=-1, keepdims=True) + EPS) * g_ref[...]


def _final_norm(x, g, row0, nrows):
    d = x.shape[1]
    tm = 256
    rb0 = row0 // tm
    return pl.pallas_call(
        _final_norm_kernel,
        grid=(nrows // tm,),
        in_specs=[pl.BlockSpec((tm, d), lambda i: (rb0 + i, 0)), pl.BlockSpec((1, d), lambda i: (0, 0))],
        out_specs=pl.BlockSpec((tm, d), lambda i: (i, 0)),
        out_shape=jax.ShapeDtypeStruct((nrows, d), F32),
        compiler_params=_cparams("parallel"),
        name="final_norm",
    )(x, g.reshape(1, d))


def _shift_rows(x, off, row):
    if off == 0:
        return x
    n = x.shape[0]
    y = pltpu.roll(x, (-off) % n, 0)
    ok = (row + off >= 0) & (row + off < n)
    return jnp.where(ok, y, 0.0)


def _gdn_prep_kernel(x_ref, w_ref, o_ref, *, seqlen):
    left = CONV_K // 2
    edge = 16
    row = lax.broadcasted_iota(jnp.int32, (edge, LANE), 0)
    kind = pl.program_id(1)
    for s in range(x_ref.shape[0] // seqlen):
        for h in range(GDN_HEADS):
            rows = slice(s * seqlen, (s + 1) * seqlen)
            cols = slice(h * LANE, (h + 1) * LANE)
            x = x_ref[rows, cols]

            def conv(xb, shift):
                acc = jnp.zeros_like(xb)
                for j in range(CONV_K):
                    acc = acc + shift(xb, j - left) * w_ref[j:j + 1, cols]
                return acc

            acc = conv(x, lambda xb, off: xb if off == 0 else pltpu.roll(xb, (-off) % seqlen, 0))
            head = conv(x[0:edge], lambda xb, off: _shift_rows(xb, off, row))
            tail = conv(x[seqlen - edge:seqlen], lambda xb, off: _shift_rows(xb, off, row))
            acc = jnp.concatenate([head[0:8], acc[8:seqlen - 8], tail[edge - 8:edge]], axis=0)
            y = _silu(acc)
            inv = lax.rsqrt(jnp.sum(y * y, axis=-1, keepdims=True) + EPS)
            inv = inv * jnp.where(kind == 0, GDN_DK ** -0.5, 1.0)
            o_ref[rows, cols] = jnp.where(kind < 2, y * inv, y)


def _gdn_prep(proj, conv_w, row0, nseq, seqlen):
    ns = max(1, PREP_STEP_ROWS // seqlen)
    while nseq % ns or (row0 // seqlen) % ns:
        ns -= 1
    blk0 = row0 // (ns * seqlen)
    return pl.pallas_call(
        functools.partial(_gdn_prep_kernel, seqlen=seqlen),
        grid=(nseq // ns, 3),
        in_specs=[
            pl.BlockSpec((ns * seqlen, GDN_W), lambda s, c: (blk0 + s, CB_Q // GDN_HEADS + c)),
            pl.BlockSpec((CONV_K, GDN_W), lambda s, c: (0, c)),
        ],
        out_specs=pl.BlockSpec((ns * seqlen, GDN_W), lambda s, c: (s, c)),
        out_shape=jax.ShapeDtypeStruct((nseq * seqlen, 3 * GDN_W), F32),
        compiler_params=_cparams("parallel", "parallel"),
        name="gdn_prep",
    )(proj, conv_w)


def _unit_tri_solve_many(lmats, rhss, eye):
    xs = [eye - l for l in lmats]
    ps = [_mm(l, l) for l in lmats]
    steps = int(math.log2(CHUNK)) - 1
    for s in range(steps):
        xs = [x + _mm(x, p) for x, p in zip(xs, ps)]
        if s + 1 < steps:
            ps = [_mm(p, p) for p in ps]
    x0s = [_mm(x, r) for x, r in zip(xs, rhss)]
    resids = [r - (x0 + _mm(l, x0)) for l, r, x0 in zip(lmats, rhss, x0s)]
    return [x0 + _mm(x, rs) for x, x0, rs in zip(xs, x0s, resids)]


def _gdn_intra_kernel(q_ref, k_ref, v_ref, ab_ref, alog_ref, dt_ref,
                      u_ref, w_ref, qg_ref, kd_ref, a_ref, gl_ref):
    n = GDN_BLK
    ab = ab_ref[...]
    lane = lax.broadcasted_iota(jnp.int32, ab.shape, 1)
    sp = jnp.maximum(ab + dt_ref[...], 0.0) + jnp.log(1.0 + jnp.exp(-jnp.abs(ab + dt_ref[...])))
    gmat = -jnp.exp(alog_ref[...]) * sp
    bmat = _sigmoid(ab)

    ri = lax.broadcasted_iota(jnp.int32, (n, n), 0)
    ci = lax.broadcasted_iota(jnp.int32, (n, n), 1)
    same = (ri // CHUNK) == (ci // CHUNK)
    eye = (ri == ci).astype(F32)
    r8 = lax.broadcasted_iota(jnp.int32, (8, LANE), 0)
    gtot_all = _dot_exact_lhs(same.astype(BF16), gmat)
    gcf_all = _dot_exact_lhs((same & (ci <= ri)).astype(BF16), gmat)
    gc_dirs = (gcf_all, (gtot_all - gcf_all) + gmat)
    incl_dirs = (same & (ci <= ri), same & (ci >= ri))
    strict_dirs = (same & (ci < ri), same & (ci > ri))

    lmats, rhss, dests = [], [], []
    for h in range(GDN_HEADS):
        cols = slice(h * LANE, (h + 1) * LANE)
        q = q_ref[:, cols]
        k = k_ref[:, cols]
        v = v_ref[:, cols]
        kk = _mm(k, k, _dot_nt)
        qk = _mm(q, k, _dot_nt)
        for d in range(2):
            incl = incl_dirs[d]
            sel_g = lane == d * GDN_HEADS + h
            sel_b = lane == 2 * GDN_HEADS + d * GDN_HEADS + h
            gc = jnp.sum(jnp.where(sel_g, gc_dirs[d], 0.0), axis=1, keepdims=True)
            gtot = jnp.sum(jnp.where(sel_g, gtot_all, 0.0), axis=1, keepdims=True)
            beta = jnp.sum(jnp.where(sel_b, bmat, 0.0), axis=1, keepdims=True)
            gc_b = jnp.broadcast_to(gc, (n, LANE))
            gc_row = jnp.transpose(gc_b)[0:1, :]
            diff = jnp.where(incl, gc - gc_row, 0.0)
            decay = jnp.where(incl, jnp.exp(diff), 0.0)
            eg = jnp.exp(gc_b)
            lmats.append(jnp.where(strict_dirs[d], kk * beta * decay, 0.0))
            rhss.append(jnp.concatenate([v * beta, (k * beta) * eg], axis=1))
            dests.append((d, cols))
            qg_ref[d, :, cols] = (q * eg).astype(qg_ref.dtype)
            kd_ref[d, :, cols] = (k * jnp.exp(jnp.broadcast_to(gtot - gc, (n, LANE)))).astype(kd_ref.dtype)
            amat = qk * decay
            a_ref[d, h] = ((amat[:, 0:CHUNK] + amat[:, CHUNK:2 * CHUNK]) + (
                amat[:, 2 * CHUNK:3 * CHUNK] + amat[:, 3 * CHUNK:4 * CHUNK])).astype(a_ref.dtype)
            egt = jnp.exp(jnp.broadcast_to(gtot, (n, LANE)))
            gl8 = jnp.zeros((8, LANE), F32)
            for c in range(n // CHUNK):
                gl8 = jnp.where(r8 == c, egt[c * CHUNK:c * CHUNK + 8, :], gl8)
            gl_ref[d, :, cols] = gl8

    for (d, cols), uw in zip(dests, _unit_tri_solve_many(lmats, rhss, eye)):
        u_ref[d, :, cols] = uw[:, :GDN_DV]
        w_ref[d, :, cols] = uw[:, GDN_DV:].astype(w_ref.dtype)


def _gdn_intra(qkv, proj_ab, alog_row, dt_row, row0):
    t = qkv.shape[0]
    nb = t // GDN_BLK
    pb0 = row0 // GDN_BLK
    big = lambda dt: jax.ShapeDtypeStruct((2, t, GDN_W), dt)
    big_spec = pl.BlockSpec((2, GDN_BLK, GDN_W), lambda i: (0, i, 0))
    return pl.pallas_call(
        _gdn_intra_kernel,
        grid=(nb,),
        in_specs=[
            pl.BlockSpec((GDN_BLK, GDN_W), lambda i: (i, 0)),
            pl.BlockSpec((GDN_BLK, GDN_W), lambda i: (i, 1)),
            pl.BlockSpec((GDN_BLK, GDN_W), lambda i: (i, 2)),
            pl.BlockSpec((GDN_BLK, LANE), lambda i: (pb0 + i, 0)),
            pl.BlockSpec((1, LANE), lambda i: (0, 0)),
            pl.BlockSpec((1, LANE), lambda i: (0, 0)),
        ],
        out_specs=[big_spec, big_spec, big_spec, big_spec,
                   pl.BlockSpec((2, GDN_HEADS, GDN_BLK, CHUNK), lambda i: (0, 0, i, 0)),
                   pl.BlockSpec((2, 8, GDN_W), lambda i: (0, i, 0))],
        out_shape=[big(F32), big(BF16), big(BF16), big(BF16),
                   jax.ShapeDtypeStruct((2, GDN_HEADS, t, CHUNK), BF16),
                   jax.ShapeDtypeStruct((2, nb * 8, GDN_W), F32)],
        compiler_params=_cparams("parallel"),
        name="gdn_intra",
    )(qkv, qkv, qkv, proj_ab, alog_row, dt_row)


def _gdn_scan_kernel(*refs, has_init):
    if has_init:
        s0_ref, refs = refs[0], refs[1:]
    ins = (refs[0:6], refs[6:12])
    o_refs, sf_ref, s_ref = refs[12:14], refs[14], refs[15]
    blk = pl.program_id(1)
    nchunk = GDN_BLK // CHUNK

    @pl.when(blk == 0)
    def _():
        if has_init:
            s_ref[...] = s0_ref[...]
        else:
            s_ref[...] = jnp.zeros_like(s_ref)

    U, W, QG, KD, A, GL = range(6)
    chains = [(d, h) for d in range(2) for h in range(GDN_HEADS)]
    states = [s_ref[d, h] for d, h in chains]
    for c in range(nchunk):
        ccs = [c if d == 0 else nchunk - 1 - c for d, _ in chains]
        rows = [slice(cc * CHUNK, (cc + 1) * CHUNK) for cc in ccs]
        cols = [slice(h * LANE, (h + 1) * LANE) for _, h in chains]
        n = range(len(chains))
        ws = [_mm(ins[chains[i][0]][W][rows[i], cols[i]], states[i]) for i in n]
        qs = [_mm(ins[chains[i][0]][QG][rows[i], cols[i]], states[i]) for i in n]
        v_news = [ins[chains[i][0]][U][rows[i], cols[i]] - ws[i] for i in n]
        for i in n:
            d, h = chains[i]
            o_refs[d][rows[i], cols[i]] = qs[i] + _mm(ins[d][A][h, rows[i], :], v_news[i])
        states = [states[i] * ins[chains[i][0]][GL][ccs[i]:ccs[i] + 1, cols[i]]
                  + _mm(ins[chains[i][0]][KD][rows[i], cols[i]], v_news[i], _dot_tn) for i in n]
    for (d, h), s in zip(chains, states):
        s_ref[d, h] = s

    @pl.when(blk == pl.num_programs(1) - 1)
    def _():
        sf_ref[...] = s_ref[...]


def _gdn_scan(s0, u, w, qg, kd, a, gl, nseq, seqlen):
    t = u.shape[1]
    nblk = seqlen // GDN_BLK
    in_specs, args = [], []
    for d in range(2):
        rb = (lambda s, b: s * nblk + b) if d == 0 else (lambda s, b: s * nblk + nblk - 1 - b)
        big = pl.BlockSpec((None, GDN_BLK, GDN_W), lambda s, b, d=d, rb=rb: (d, rb(s, b), 0))
        in_specs += [big, big, big, big,
                     pl.BlockSpec((None, GDN_HEADS, GDN_BLK, CHUNK), lambda s, b, d=d, rb=rb: (d, 0, rb(s, b), 0)),
                     pl.BlockSpec((None, 8, GDN_W), lambda s, b, d=d, rb=rb: (d, rb(s, b), 0))]
        args += [u, w, qg, kd, a, gl]
    state_spec = pl.BlockSpec((None, 2, GDN_HEADS, GDN_DK, GDN_DV), lambda s, b: (s, 0, 0, 0, 0))
    if s0 is not None:
        in_specs.insert(0, state_spec)
        args.insert(0, s0)
    out_specs = [pl.BlockSpec((GDN_BLK, GDN_W), lambda s, b: (s * nblk + b, 0)),
                 pl.BlockSpec((GDN_BLK, GDN_W), lambda s, b: (s * nblk + nblk - 1 - b, 0)),
                 state_spec]
    return pl.pallas_call(
        functools.partial(_gdn_scan_kernel, has_init=s0 is not None),
        grid=(nseq, nblk),
        in_specs=in_specs,
        out_specs=out_specs,
        out_shape=[jax.ShapeDtypeStruct((t, GDN_W), F32), jax.ShapeDtypeStruct((t, GDN_W), F32),
                   jax.ShapeDtypeStruct((nseq, 2, GDN_HEADS, GDN_DK, GDN_DV), F32)],
        scratch_shapes=[pltpu.VMEM((2, GDN_HEADS, GDN_DK, GDN_DV), F32)],
        compiler_params=_cparams("parallel", "arbitrary"),
        name="gdn_scan",
    )(*args)


def _rope_rotate(x, cos, sin_signed):
    lane = lax.broadcasted_iota(jnp.int32, x.shape, 1)
    first = (lane % 32) < 16
    partner = jnp.where(first, pltpu.roll(x, LANE - 16, 1), pltpu.roll(x, 16, 1))
    return x * cos + partner * sin_signed


def _rope_tables(seqlen):
    pos = jnp.arange(seqlen)
    rowp = (pos // GRID_W).astype(F32)
    colp = (pos % GRID_W).astype(F32)
    half = DIFF_DQK // 2
    inv_freq = ROPE_BASE ** (-jnp.arange(0, half, 2, dtype=F32) / half)
    c = jnp.arange(LANE)
    c64 = c % DIFF_DQK
    p = jnp.where((c64 // half)[None, :] == 0, rowp[:, None], colp[:, None])
    ang = p * inv_freq[c64 % (half // 2)][None, :]
    sign = jnp.where((c64 % half) < half // 2, -1.0, 1.0).astype(F32)
    return jnp.cos(ang), jnp.sin(ang) * sign[None, :]


def _diff_attn_kernel(*refs, seg_lens, lam_init, rope, hp, n_alias, emit_cache):
    nseg = len(seg_lens)
    q_ref = refs[0]
    k_refs = refs[1:1 + nseg]
    v_refs = refs[1 + nseg:1 + 2 * nseg]
    pos = 1 + 2 * nseg
    if rope:
        cosq_ref, sinq_ref, cosk_ref, sink_ref = refs[pos:pos + 4]
        pos += 4
    lam_ref, g_ref = refs[pos:pos + 2]
    pos += 2 + n_alias
    o_ref = refs[pos]
    if emit_cache:
        kc_ref, vc_ref = refs[pos + 1:pos + 3]
    kb_ref, vb_ref = refs[-2:]
    offs = [sum(seg_lens[:s]) for s in range(nseg)]
    hcols = [slice(hh * DIFF_DV, (hh + 1) * DIFF_DV) for hh in range(hp)]

    @pl.when(pl.program_id(2) == 0)
    def _():
        if emit_cache:
            kc_ref[...] = k_refs[-1][...]
            vc_ref[...] = v_refs[-1][...]
        for hh in range(hp):
            vb_ref[:, (2 * hh + 1) * DIFF_DV:(2 * hh + 2) * DIFF_DV] = jnp.ones((vb_ref.shape[0], DIFF_DV), BF16)
            for s in range(nseg):
                rows = slice(offs[s], offs[s] + seg_lens[s])
                k = k_refs[s][:, hcols[hh]]
                if rope and s == nseg - 1:
                    k = _rope_rotate(k, cosk_ref[...], sink_ref[...])
                kb_ref[rows, hcols[hh]] = k.astype(BF16)
                vb_ref[rows, 2 * hh * DIFF_DV:(2 * hh + 1) * DIFF_DV] = v_refs[s][:, hcols[hh]].astype(BF16)

    lv = lam_ref[...]
    lam = (jnp.exp(jnp.sum(lv[0:1] * lv[1:2], axis=1, keepdims=True))
           - jnp.exp(jnp.sum(lv[2:3] * lv[3:4], axis=1, keepdims=True)) + lam_init)
    lane = lax.broadcasted_iota(jnp.int32, (ATTN_SUB, DIFF_DV), 1)

    def scores(hh, rows):
        q = q_ref[rows, hcols[hh]]
        if rope:
            q = _rope_rotate(q, cosq_ref[rows, :], sinq_ref[rows, :])
        q = q * (DIFF_DQK ** -0.5)
        kb = kb_ref[:, hcols[hh]]
        return [_dot_nt(jnp.where((lane // DIFF_DQK) == m, q, 0.0).astype(BF16), kb) for m in range(2)]

    units = [(hh, slice(r * ATTN_SUB, (r + 1) * ATTN_SUB))
             for hh in range(hp) for r in range(q_ref.shape[0] // ATTN_SUB)]
    ss_next = scores(*units[0])
    for n, (hh, rows) in enumerate(units):
        ss = ss_next
        if n + 1 < len(units):
            ss_next = scores(*units[n + 1])
        vb = vb_ref[:, 2 * hh * DIFF_DV:(2 * hh + 2) * DIFF_DV]
        es = [jnp.exp(s - jnp.max(s, axis=1, keepdims=True)).astype(BF16) for s in ss]
        accs = [_dot(e, vb) for e in es]
        outs = [acc[:, 0:DIFF_DV] / acc[:, DIFF_DV:] for acc in accs]
        o = outs[0] - lam * outs[1]
        y = o * lax.rsqrt(jnp.mean(o * o, axis=-1, keepdims=True) + EPS) * g_ref[...]
        o_ref[rows, hcols[hh]] = (y * (1.0 - lam_init)).astype(o_ref.dtype)


def _diff_attn(proj, row0, nseq, seqlen, past_kv, l, lam_vec, g, cache_out):
    tq = min(4 * ATTN_SUB, seqlen)
    hp = min(DIFF_HEADS, max(1, ATTN_STEP_ROWS // seqlen))
    nqb = seqlen // tq
    qb0 = row0 // tq
    sb0 = row0 // seqlen
    wcols = hp * DIFF_DV
    q_spec = pl.BlockSpec((tq, wcols), lambda b, h, i: (qb0 + b * nqb + i, CB_DQ // hp + h))
    k_new = pl.BlockSpec((seqlen, wcols), lambda b, h, i: (sb0 + b, CB_DK // hp + h))
    v_new = pl.BlockSpec((seqlen, wcols), lambda b, h, i: (sb0 + b, CB_DV // hp + h))
    arrs, specs, seg_lens = [proj], [q_spec], [seqlen]
    if past_kv is None:
        arrs += [proj, proj]
        specs += [k_new, v_new]
    else:
        past = past_kv[0].shape[2]
        cached = pl.BlockSpec((None, None, past, wcols), lambda b, h, i: (b, l, 0, h))
        arrs += [past_kv[0], proj, past_kv[1], proj]
        specs += [cached, k_new, cached, v_new]
        seg_lens = [past, seqlen]
        cos, sin = _rope_tables(seqlen)
        arrs += [cos, sin, cos, sin]
        specs += [pl.BlockSpec((tq, DIFF_DV), lambda b, h, i: (i, 0))] * 2 + [
            pl.BlockSpec((seqlen, DIFF_DV), lambda b, h, i: (0, 0))] * 2
    arrs += [lam_vec, g.reshape(1, DIFF_DV)]
    specs += [pl.BlockSpec((4, DIFF_DQK), lambda b, h, i: (0, 0)),
              pl.BlockSpec((1, DIFF_DV), lambda b, h, i: (0, 0))]
    out_specs = [pl.BlockSpec((tq, wcols), lambda b, h, i: (b * nqb + i, h))]
    out_shape = [jax.ShapeDtypeStruct((nseq * seqlen, DIFF_W), BF16)]
    aliases, n_alias = {}, 0
    if cache_out is not None:
        depth, prev = cache_out
        entry = pl.BlockSpec((None, None, seqlen, wcols), lambda b, h, i: (b, l, 0, h))
        out_specs += [entry, entry]
        out_shape += [jax.ShapeDtypeStruct((nseq, depth, seqlen, DIFF_W), F32)] * 2
        if prev is not None:
            aliases = {len(arrs): 1, len(arrs) + 1: 2}
            n_alias = 2
            arrs += list(prev)
            specs += [pl.BlockSpec(memory_space=pl.ANY)] * 2
    lk = sum(seg_lens)
    outs = pl.pallas_call(
        functools.partial(_diff_attn_kernel, seg_lens=tuple(seg_lens), lam_init=_lambda_init(l),
                          rope=past_kv is not None, hp=hp, n_alias=n_alias, emit_cache=cache_out is not None),
        grid=(nseq, DIFF_HEADS // hp, nqb),
        in_specs=specs,
        out_specs=out_specs,
        out_shape=out_shape,
        input_output_aliases=aliases,
        scratch_shapes=[pltpu.VMEM((lk, wcols), BF16), pltpu.VMEM((lk, 2 * wcols), BF16)],
        compiler_params=_cparams("parallel", "parallel", "arbitrary"),
        name="diff_attn",
    )(*arrs)
    return outs[0], (tuple(outs[1:]) if cache_out is not None else None)


def _pool_kernel(x_ref, w_ref, sc_ref, o_ref):
    n = x_ref.shape[0]
    row = lax.broadcasted_iota(jnp.int32, (n, POOL_GW), 0)
    for gi, win in enumerate(POOL_WINDOWS):
        cols = slice(gi * POOL_GW, (gi + 1) * POOL_GW)
        x = x_ref[:, cols]
        a = win // 2
        b = win - a - 1
        acc = x
        for off in range(-a, b + 1):
            if off != 0:
                acc = acc + _shift_rows(x, off, row)
        cnt = (jnp.minimum(row + b + 1, n) - jnp.maximum(row - a, 0)).astype(F32)
        pooled = acc / cnt - x
        y = _dot(pooled.astype(BF16), w_ref[gi].astype(BF16))
        o_ref[:, cols] = (y * sc_ref[:, cols]).astype(o_ref.dtype)


def _pool(proj, pool_w, pool_scale, row0, nseq, seqlen):
    blk0 = row0 // seqlen
    return pl.pallas_call(
        _pool_kernel,
        grid=(nseq,),
        in_specs=[
            pl.BlockSpec((seqlen, POOL_W), lambda s: (blk0 + s, CB_PIN // len(POOL_WINDOWS))),
            pl.BlockSpec((len(POOL_WINDOWS), POOL_GW, POOL_GW), lambda s: (0, 0, 0)),
            pl.BlockSpec((1, POOL_W), lambda s: (0, 0)),
        ],
        out_specs=pl.BlockSpec((seqlen, POOL_W), lambda s: (s, 0)),
        out_shape=jax.ShapeDtypeStruct((nseq * seqlen, POOL_W), BF16),
        compiler_params=_cparams("parallel"),
        name="pool",
    )(proj, pool_w, pool_scale.reshape(1, POOL_W))


def _mix_group(proj, proj_ab, lp, l, row0, nseq, seqlen, ctx, cache_out):
    qkv = _gdn_prep(proj, lp["gdn_conv"], row0, nseq, seqlen)
    u, w, qg, kd, a, gl = _gdn_intra(qkv, proj_ab, lp["alog_row"], lp["dt_row"], row0)
    o_f, o_b, s_fin = _gdn_scan(None if ctx is None else ctx[2], u, w, qg, kd, a, gl, nseq, seqlen)

    o_diff, caches = _diff_attn(proj, row0, nseq, seqlen, None if ctx is None else ctx[:2], l,
                                lp["diff_lam"], lp["diff_norm"], cache_out)
    o_pool = _pool(proj, lp["pool_w"], lp["pool_scale"], row0, nseq, seqlen)
    return (o_f, o_b, o_diff, o_pool), s_fin, caches


def _split_w_in(w_in):
    depth, d, _ = w_in.shape
    o_ab = 4 * GDN_W
    o_dq = o_ab + 4 * GDN_HEADS
    pad = jnp.zeros((depth, d, LANE - 4 * GDN_HEADS), w_in.dtype)
    main = jnp.concatenate([w_in[..., :o_ab], w_in[..., o_dq:]], axis=-1)
    ab = jnp.concatenate([w_in[..., o_ab:o_dq], pad], axis=-1)
    return main.astype(BF16), ab.astype(BF16)


def kernel(x_prompt, x_sample, c, cache_k, cache_v, state_gdn, c_ctx, w_ada, b_ada, norm_ffn1, ffn1_in,
           ffn1_out, norm_mix, w_in, gdn_conv, gdn_a_log, gdn_dt_bias, gdn_norm, diff_lam, diff_norm,
           pool_w, pool_scale, w_out, norm_ffn2, ffn2_in, ffn2_out, final_norm):
    batch, seq, d = x_prompt.shape
    dec_batch, dec_seq, _ = x_sample.shape
    depth = w_ada.shape[0]
    n_ctx = batch * seq
    n_dec = dec_batch * dec_seq
    layout = (n_ctx, dec_seq)
    past = cache_k.shape[2]

    x = (x_prompt.reshape(n_ctx, d), x_sample.reshape(n_dec, d))
    cond8 = jnp.concatenate([c_ctx[None, :], c, jnp.zeros((8 - 1 - dec_batch, d), F32)], axis=0)
    ada = _ada(cond8, w_ada, b_ada)
    cache_k4 = cache_k.reshape(dec_batch, depth, past, DIFF_W)
    cache_v4 = cache_v.reshape(dec_batch, depth, past, DIFF_W)
    w_in_main, w_in_ab = _split_w_in(w_in)
    w_out_b = w_out.astype(BF16)
    ffn_w = [(ffn1_in, ffn1_out.astype(BF16)), (ffn2_in, ffn2_out.astype(BF16))]

    caches, ss = None, []
    for l in range(depth):
        ada3 = ada[l, :1 + dec_batch].reshape(1 + dec_batch, 1, N_MOD * d)
        lane_pad = jnp.zeros((LANE - 2 * GDN_HEADS,), F32)
        lp = dict(
            gdn_conv=gdn_conv[l], gdn_norm=gdn_norm[l], diff_lam=diff_lam[l], diff_norm=diff_norm[l],
            pool_w=pool_w[l], pool_scale=pool_scale[l],
            alog_row=jnp.concatenate([gdn_a_log[l].reshape(-1), lane_pad]).reshape(1, LANE),
            dt_row=jnp.concatenate([gdn_dt_bias[l].reshape(-1), lane_pad]).reshape(1, LANE),
        )
        act = _ffn_in(x, ada3, (0, 1), norm_ffn1[l], ffn_w[0][0], l, layout=layout)
        x = _ffn_out(act, ffn_w[0][1], l, x, ada3, 2, layout=layout)

        proj, proj_ab = _mix_in(x, ada3, (3, 4), norm_mix[l], w_in_main, w_in_ab, l, layout=layout)
        acts_c, s_c, caches = _mix_group(proj, proj_ab, lp, l, 0, batch, seq, None, (depth, caches))
        acts_d, _, _ = _mix_group(proj, proj_ab, lp, l, n_ctx, dec_batch, dec_seq,
                                  (cache_k4, cache_v4, state_gdn[:, l]), None)
        x = _mix_out(acts_c, proj, gdn_norm[l], w_out_b, l, x, ada3, 5, 0, layout=layout)
        x = _mix_out(acts_d, proj, gdn_norm[l], w_out_b, l, x, ada3, 5, n_ctx, layout=layout)
        ss.append(s_c)

        act = _ffn_in(x, ada3, (6, 7), norm_ffn2[l], ffn_w[1][0], l, layout=layout)
        x = _ffn_out(act, ffn_w[1][1], l, x, ada3, 8, layout=layout)

    y_prompt = _final_norm(x, final_norm, 0, n_ctx).reshape(batch, seq, d)
    y_sample = _final_norm(x, final_norm, n_ctx, n_dec).reshape(dec_batch, dec_seq, d)
    new_k, new_v = (t.reshape(batch, depth, seq, DIFF_HEADS, DIFF_DV) for t in caches)
    return (y_prompt, y_sample, new_k, new_v, jnp.stack(ss, axis=1))
```

```python
import functools
import math

import jax
import jax.numpy as jnp
from jax import lax
from jax.experimental import pallas as pl
from jax.experimental.pallas import tpu as pltpu

F32 = jnp.float32
BF16 = jnp.bfloat16

GRID_W = 64
GDN_HEADS = 4
GDN_DK = 128
GDN_DV = 128
GDN_W = GDN_HEADS * GDN_DV
CONV_K = 4
CHUNK = 64
DIFF_HEADS = 8
DIFF_DQK = 64
DIFF_DV = 2 * DIFF_DQK
DIFF_W = DIFF_HEADS * DIFF_DV
ROPE_BASE = 10000.0
POOL_WINDOWS = (2, 4, 8, 16)
POOL_GW = 128
POOL_W = len(POOL_WINDOWS) * POOL_GW
MIX_W = GDN_W + DIFF_W + POOL_W
N_MOD = 9
EPS = 1e-6

LANE = 128
GDN_BLK = 4 * CHUNK
ATTN_SUB = 256
ATTN_STEP_ROWS = 2048
PRO_SUB = 256
PREP_STEP_ROWS = 1024
VMEM_LIMIT = 60 * 1024 * 1024

CB_Q, CB_K, CB_V, CB_Z = 0, 4, 8, 12
CB_DQ, CB_DK, CB_DV, CB_PIN = 16, 24, 32, 40
PROJ_MAIN = 44 * LANE


def _lambda_init(l):
    return 0.8 - 0.6 * math.exp(-0.3 * l)


def _cparams(*sem):
    return pltpu.CompilerParams(dimension_semantics=sem, vmem_limit_bytes=VMEM_LIMIT)


def _dot(a, b):
    return lax.dot_general(a, b, (((1,), (0,)), ((), ())), preferred_element_type=F32)


def _dot_nt(a, b):
    return lax.dot_general(a, b, (((1,), (1,)), ((), ())), preferred_element_type=F32)


def _dot_tn(a, b):
    return lax.dot_general(a, b, (((0,), (0,)), ((), ())), preferred_element_type=F32)


def _mm(a, b, dot=_dot):
    return dot(a.astype(BF16), b.astype(BF16))


def _dot_exact_lhs(m, b):
    b0 = b.astype(BF16)
    r = b - b0.astype(F32)
    b1 = r.astype(BF16)
    b2 = (r - b1.astype(F32)).astype(BF16)
    return _dot(m, b0) + (_dot(m, b1) + _dot(m, b2))


def _sigmoid(x):
    return 0.5 * (1.0 + jnp.tanh(0.5 * x))


def _silu(x):
    return x * _sigmoid(x)


def _ada_kernel(cond_ref, w_ref, b_ref, o_ref):
    s = _silu(cond_ref[...]).astype(BF16)
    o_ref[...] = _dot(s, w_ref[...].astype(BF16)) + b_ref[...]


def _ada(cond8, w_ada, b_ada):
    depth, d, n = w_ada.shape
    tn = 1024
    return pl.pallas_call(
        _ada_kernel,
        grid=(depth, n // tn),
        in_specs=[
            pl.BlockSpec((8, d), lambda l, j: (0, 0)),
            pl.BlockSpec((None, d, tn), lambda l, j: (l, 0, j)),
            pl.BlockSpec((None, 1, tn), lambda l, j: (l, 0, j)),
        ],
        out_specs=pl.BlockSpec((None, 8, tn), lambda l, j: (l, 0, j)),
        out_shape=jax.ShapeDtypeStruct((depth, 8, n), F32),
        compiler_params=_cparams("parallel", "parallel"),
        name="ada",
    )(cond8, w_ada, b_ada.reshape(depth, 1, n))


def _norm_mod(x, g, shift, scale):
    y = x * lax.rsqrt(jnp.mean(x * x, axis=-1, keepdims=True) + EPS) * g
    return y * (1.0 + scale) + shift


def _read_x(x_refs, n_first, idx):
    if len(x_refs) == 1:
        return x_refs[0][idx]
    return jnp.where(pl.program_id(0) < n_first, x_refs[0][idx], x_refs[1][idx])


def _norm_mod_rows(x_refs, n_first, sh_ref, sc_ref, g_ref, rows):
    x = _read_x(x_refs, n_first, (rows, slice(None)))
    return _norm_mod(x, g_ref[...], sh_ref[0], sc_ref[0]).astype(BF16)


def _row_subs(n):
    return [slice(r * PRO_SUB, (r + 1) * PRO_SUB) for r in range(n // PRO_SUB)]


def _mix_in_kernel(x_ref, sh_ref, sc_ref, g_ref, w_ref, wab_ref, o_ref, ab_ref, h_ref):
    @pl.when(pl.program_id(1) == 0)
    def _():
        for rows in _row_subs(x_ref.shape[0]):
            h = _norm_mod_rows((x_ref,), None, sh_ref, sc_ref, g_ref, rows)
            h_ref[rows, :] = h
            ab_ref[rows, :] = _dot(h, wab_ref[...])
            o_ref[rows, :] = _dot(h, w_ref[...])

    @pl.when(pl.program_id(1) != 0)
    def _():
        o_ref[...] = _dot(h_ref[...], w_ref[...])


def _ffn_in_kernel(*refs, n_first):
    x_refs = refs[:-7]
    sh_ref, sc_ref, g_ref, wg_ref, wu_ref, o_ref, h_ref = refs[-7:]

    def swiglu(h):
        gate = _dot(h, wg_ref[...].astype(BF16))
        up = _dot(h, wu_ref[...].astype(BF16))
        return (_silu(gate) * up).astype(o_ref.dtype)

    @pl.when(pl.program_id(1) == 0)
    def _():
        for rows in _row_subs(h_ref.shape[0]):
            h = _norm_mod_rows(x_refs, n_first, sh_ref, sc_ref, g_ref, rows)
            h_ref[rows, :] = h
            o_ref[rows, :] = swiglu(h)

    @pl.when(pl.program_id(1) != 0)
    def _():
        o_ref[...] = swiglu(h_ref[...])


def _row_tile(groups, cap):
    tm = cap
    while any(g % tm for g in groups):
        tm //= 2
    return tm


def _mod_row(n_ctx, dec_seq, tm, row0=0):
    def f(i):
        r = row0 + i * tm
        return jnp.where(r < n_ctx, 0, 1 + (r - n_ctx) // dec_seq)
    return f


def _x_parts(x, tm):
    if isinstance(x, tuple):
        return list(x), x[0].shape[0] // tm, sum(p.shape[0] for p in x)
    return [x], None, x.shape[0]


def _x_specs(n_first, tm, tn):
    if n_first is None:
        return [pl.BlockSpec((tm, tn), lambda i, j: (i, j))]
    return [pl.BlockSpec((tm, tn), lambda i, j: (jnp.where(i < n_first, i, n_first - 1), jnp.where(i < n_first, j, 0))),
            pl.BlockSpec((tm, tn), lambda i, j: (jnp.where(i < n_first, 0, i - n_first), jnp.where(i < n_first, 0, j)))]


def _pm_specs(d, tm, row, mod_cols, n_first=None):
    sh_c, sc_c = mod_cols
    x_specs = [pl.BlockSpec((tm, d), lambda i, j: (i, 0))] if n_first is None else [
        pl.BlockSpec((tm, d), lambda i, j: (jnp.minimum(i, n_first - 1), 0)),
        pl.BlockSpec((tm, d), lambda i, j: (jnp.maximum(i - n_first, 0), 0))]
    return x_specs + [
        pl.BlockSpec((1, 1, d), lambda i, j: (row(i), 0, sh_c)),
        pl.BlockSpec((1, 1, d), lambda i, j: (row(i), 0, sc_c)),
        pl.BlockSpec((1, d), lambda i, j: (0, 0)),
    ]


def _ffn_in(x, ada3, mod_cols, g, w, l, *, layout):
    d = w.shape[1]
    n_ctx, dec_seq = layout
    tm = _row_tile((n_ctx, dec_seq), 1024)
    xs, n_first, t = _x_parts(x, tm)
    row = _mod_row(n_ctx, dec_seq, tm)
    n = w.shape[2] // 2
    tn = 512 if n % 512 == 0 else n
    nj = n // tn
    return pl.pallas_call(
        functools.partial(_ffn_in_kernel, n_first=n_first),
        grid=(t // tm, nj),
        in_specs=_pm_specs(d, tm, row, mod_cols, n_first) + [
            pl.BlockSpec((None, d, tn), lambda i, j: (l, 0, j)),
            pl.BlockSpec((None, d, tn), lambda i, j: (l, 0, j + nj)),
        ],
        out_specs=pl.BlockSpec((tm, tn), lambda i, j: (i, j)),
        out_shape=jax.ShapeDtypeStruct((t, n), BF16),
        scratch_shapes=[pltpu.VMEM((tm, d), BF16)],
        compiler_params=_cparams("parallel", "arbitrary"),
        name="ffn_in",
    )(*xs, ada3, ada3, g.reshape(1, d), w, w)


def _mix_in(x, ada3, mod_cols, g, w_main, w_ab, l, *, layout):
    t, d = x.shape
    n_ctx, dec_seq = layout
    tm = _row_tile((n_ctx, dec_seq), 1024)
    row = _mod_row(n_ctx, dec_seq, tm)
    tn = 512
    return pl.pallas_call(
        _mix_in_kernel,
        grid=(t // tm, PROJ_MAIN // tn),
        in_specs=_pm_specs(d, tm, row, mod_cols) + [
            pl.BlockSpec((None, d, tn), lambda i, j: (l, 0, j)),
            pl.BlockSpec((None, d, LANE), lambda i, j: (l, 0, 0)),
        ],
        out_specs=[pl.BlockSpec((tm, tn), lambda i, j: (i, j)),
                   pl.BlockSpec((tm, LANE), lambda i, j: (i, 0))],
        out_shape=[jax.ShapeDtypeStruct((t, PROJ_MAIN), F32), jax.ShapeDtypeStruct((t, LANE), F32)],
        scratch_shapes=[pltpu.VMEM((tm, d), BF16)],
        compiler_params=_cparams("parallel", "arbitrary"),
        name="mix_in",
    )(x, ada3, ada3, g.reshape(1, d), w_main, w_ab)


def _ffn_out_kernel(*refs, n_first):
    a_ref, w_ref = refs[:2]
    x_refs = refs[2:-2]
    gate_ref, o_ref = refs[-2:]
    y = _dot(a_ref[...], w_ref[...])
    o_ref[...] = _read_x(x_refs, n_first, Ellipsis) + (0.5 * gate_ref[0]) * y


def _ffn_out(a, w, l, x, ada3, gate_col, *, layout):
    t, k = a.shape
    d = w.shape[2]
    n_ctx, dec_seq = layout
    tm = _row_tile((n_ctx, dec_seq), 1024)
    tn = min(512, d)
    xs, n_first, _ = _x_parts(x, tm)
    row = _mod_row(n_ctx, dec_seq, tm)
    gpb = d // tn
    return pl.pallas_call(
        functools.partial(_ffn_out_kernel, n_first=n_first),
        grid=(t // tm, d // tn),
        in_specs=[
            pl.BlockSpec((tm, k), lambda i, j: (i, 0)),
            pl.BlockSpec((None, k, tn), lambda i, j: (l, 0, j)),
            *_x_specs(n_first, tm, tn),
            pl.BlockSpec((1, 1, tn), lambda i, j: (row(i), 0, gate_col * gpb + j)),
        ],
        out_specs=pl.BlockSpec((tm, tn), lambda i, j: (i, j)),
        out_shape=jax.ShapeDtypeStruct((t, d), F32),
        compiler_params=pltpu.CompilerParams(
            dimension_semantics=("parallel", "parallel"), vmem_limit_bytes=VMEM_LIMIT,
            allow_input_fusion=[False, True] + [False] * (len(xs) + 1)),
        name="ffn_out",
    )(a, w, *xs, ada3)


def _mix_out_kernel(a1_ref, a2_ref, a3_ref, w_ref, x_ref, gate_ref, o_ref):
    y = (_dot(a1_ref[...], w_ref[0:GDN_W, :])
         + _dot(a2_ref[...], w_ref[GDN_W:GDN_W + DIFF_W, :])
         + _dot(a3_ref[...], w_ref[GDN_W + DIFF_W:MIX_W, :]))
    o_ref[...] = x_ref[...] + gate_ref[0] * y


def _mix_out(acts, w, l, x, ada3, gate_col, row0, *, layout):
    a1, a2, a3 = acts
    n = a1.shape[0]
    t, d = x.shape
    n_ctx, dec_seq = layout
    tm = _row_tile((n_ctx, dec_seq), 1024)
    tn = min(512, d)
    row = _mod_row(n_ctx, dec_seq, tm, row0)
    rb0 = row0 // tm
    gpb = d // tn
    return pl.pallas_call(
        _mix_out_kernel,
        grid=(n // tm, d // tn),
        in_specs=[
            pl.BlockSpec((tm, GDN_W), lambda i, j: (i, 0)),
            pl.BlockSpec((tm, DIFF_W), lambda i, j: (i, 0)),
            pl.BlockSpec((tm, POOL_W), lambda i, j: (i, 0)),
            pl.BlockSpec((None, MIX_W, tn), lambda i, j: (l, 0, j)),
            pl.BlockSpec((tm, tn), lambda i, j: (rb0 + i, j)),
            pl.BlockSpec((1, 1, tn), lambda i, j: (row(i), 0, gate_col * gpb + j)),
        ],
        out_specs=pl.BlockSpec((tm, tn), lambda i, j: (rb0 + i, j)),
        out_shape=jax.ShapeDtypeStruct((t, d), F32),
        input_output_aliases={4: 0},
        compiler_params=_cparams("parallel", "parallel"),
        name="mix_out",
    )(a1, a2, a3, w, x, ada3)


def _final_norm_kernel(x_ref, g_ref, o_ref):
    x = x_ref[...]
    o_ref[...] = x * lax.rsqrt(jnp.mean(x * x, axis=-1, keepdims=True) + EPS) * g_ref[...]


def _final_norm(x, g, row0, nrows):
    d = x.shape[1]
    tm = 256
    rb0 = row0 // tm
    return pl.pallas_call(
        _final_norm_kernel,
        grid=(nrows // tm,),
        in_specs=[pl.BlockSpec((tm, d), lambda i: (rb0 + i, 0)), pl.BlockSpec((1, d), lambda i: (0, 0))],
        out_specs=pl.BlockSpec((tm, d), lambda i: (i, 0)),
        out_shape=jax.ShapeDtypeStruct((nrows, d), F32),
        compiler_params=_cparams("parallel"),
        name="final_norm",
    )(x, g.reshape(1, d))


def _shift_rows(x, off, row):
    if off == 0:
        return x
    n = x.shape[0]
    y = pltpu.roll(x, (-off) % n, 0)
    ok = (row + off >= 0) & (row + off < n)
    return jnp.where(ok, y, 0.0)


def _gdn_prep_kernel(x_ref, w_ref, o_ref, *, seqlen):
    left = CONV_K // 2
    edge = 16
    row = lax.broadcasted_iota(jnp.int32, (edge, LANE), 0)
    kind = pl.program_id(1)
    for s in range(x_ref.shape[0] // seqlen):
        for h in range(GDN_HEADS):
            rows = slice(s * seqlen, (s + 1) * seqlen)
            cols = slice(h * LANE, (h + 1) * LANE)
            x = x_ref[rows, cols]

            def conv(xb, shift):
                acc = jnp.zeros_like(xb)
                for j in range(CONV_K):
                    acc = acc + shift(xb, j - left) * w_ref[j:j + 1, cols]
                return acc

            acc = conv(x, lambda xb, off: xb if off == 0 else pltpu.roll(xb, (-off) % seqlen, 0))
            head = conv(x[0:edge], lambda xb, off: _shift_rows(xb, off, row))
            tail = conv(x[seqlen - edge:seqlen], lambda xb, off: _shift_rows(xb, off, row))
            acc = jnp.concatenate([head[0:8], acc[8:seqlen - 8], tail[edge - 8:edge]], axis=0)
            y = _silu(acc)
            inv = lax.rsqrt(jnp.sum(y * y, axis=-1, keepdims=True) + EPS)
            inv = inv * jnp.where(kind == 0, GDN_DK ** -0.5, 1.0)
            o_ref[rows, cols] = jnp.where(kind < 2, y * inv, y)


def _gdn_prep(proj, conv_w, row0, nseq, seqlen):
    ns = max(1, PREP_STEP_ROWS // seqlen)
    while nseq % ns or (row0 // seqlen) % ns:
        ns -= 1
    blk0 = row0 // (ns * seqlen)
    return pl.pallas_call(
        functools.partial(_gdn_prep_kernel, seqlen=seqlen),
        grid=(nseq // ns, 3),
        in_specs=[
            pl.BlockSpec((ns * seqlen, GDN_W), lambda s, c: (blk0 + s, CB_Q // GDN_HEADS + c)),
            pl.BlockSpec((CONV_K, GDN_W), lambda s, c: (0, c)),
        ],
        out_specs=pl.BlockSpec((ns * seqlen, GDN_W), lambda s, c: (s, c)),
        out_shape=jax.ShapeDtypeStruct((nseq * seqlen, 3 * GDN_W), F32),
        compiler_params=_cparams("parallel", "parallel"),
        name="gdn_prep",
    )(proj, conv_w)


def _unit_tri_solve_many(lmats, rhss, eye):
    xs = [eye - l for l in lmats]
    ps = [_mm(l, l) for l in lmats]
    steps = int(math.log2(CHUNK)) - 1
    for s in range(steps):
        xs = [x + _mm(x, p) for x, p in zip(xs, ps)]
        if s + 1 < steps:
            ps = [_mm(p, p) for p in ps]
    x0s = [_mm(x, r) for x, r in zip(xs, rhss)]
    resids = [r - (x0 + _mm(l, x0)) for l, r, x0 in zip(lmats, rhss, x0s)]
    return [x0 + _mm(x, rs) for x, x0, rs in zip(xs, x0s, resids)]


def _gdn_intra_kernel(q_ref, k_ref, v_ref, ab_ref, alog_ref, dt_ref,
                      u_ref, w_ref, qg_ref, kd_ref, a_ref, gl_ref):
    n = GDN_BLK
    ab = ab_ref[...]
    lane = lax.broadcasted_iota(jnp.int32, ab.shape, 1)
    sp = jnp.maximum(ab + dt_ref[...], 0.0) + jnp.log(1.0 + jnp.exp(-jnp.abs(ab + dt_ref[...])))
    gmat = -jnp.exp(alog_ref[...]) * sp
    bmat = _sigmoid(ab)

    ri = lax.broadcasted_iota(jnp.int32, (n, n), 0)
    ci = lax.broadcasted_iota(jnp.int32, (n, n), 1)
    same = (ri // CHUNK) == (ci // CHUNK)
    eye = (ri == ci).astype(F32)
    r8 = lax.broadcasted_iota(jnp.int32, (8, LANE), 0)
    gtot_all = _dot_exact_lhs(same.astype(BF16), gmat)
    gcf_all = _dot_exact_lhs((same & (ci <= ri)).astype(BF16), gmat)
    gc_dirs = (gcf_all, (gtot_all - gcf_all) + gmat)
    incl_dirs = (same & (ci <= ri), same & (ci >= ri))
    strict_dirs = (same & (ci < ri), same & (ci > ri))

    lmats, rhss, dests = [], [], []
    for h in range(GDN_HEADS):
        cols = slice(h * LANE, (h + 1) * LANE)
        q = q_ref[:, cols]
        k = k_ref[:, cols]
        v = v_ref[:, cols]
        kk = _mm(k, k, _dot_nt)
        qk = _mm(q, k, _dot_nt)
        for d in range(2):
            incl = incl_dirs[d]
            sel_g = lane == d * GDN_HEADS + h
            sel_b = lane == 2 * GDN_HEADS + d * GDN_HEADS + h
            gc = jnp.sum(jnp.where(sel_g, gc_dirs[d], 0.0), axis=1, keepdims=True)
            gtot = jnp.sum(jnp.where(sel_g, gtot_all, 0.0), axis=1, keepdims=True)
            beta = jnp.sum(jnp.where(sel_b, bmat, 0.0), axis=1, keepdims=True)
            gc_b = jnp.broadcast_to(gc, (n, LANE))
            gc_row = jnp.transpose(gc_b)[0:1, :]
            diff = jnp.where(incl, gc - gc_row, 0.0)
            decay = jnp.where(incl, jnp.exp(diff), 0.0)
            eg = jnp.exp(gc_b)
            lmats.append(jnp.where(strict_dirs[d], kk * beta * decay, 0.0))
            rhss.append(jnp.concatenate([v * beta, (k * beta) * eg], axis=1))
            dests.append((d, cols))
            qg_ref[d, :, cols] = (q * eg).astype(qg_ref.dtype)
            kd_ref[d, :, cols] = (k * jnp.exp(jnp.broadcast_to(gtot - gc, (n, LANE)))).astype(kd_ref.dtype)
            amat = qk * decay
            a_ref[d, h] = ((amat[:, 0:CHUNK] + amat[:, CHUNK:2 * CHUNK]) + (
                amat[:, 2 * CHUNK:3 * CHUNK] + amat[:, 3 * CHUNK:4 * CHUNK])).astype(a_ref.dtype)
            egt = jnp.exp(jnp.broadcast_to(gtot, (n, LANE)))
            gl8 = jnp.zeros((8, LANE), F32)
            for c in range(n // CHUNK):
                gl8 = jnp.where(r8 == c, egt[c * CHUNK:c * CHUNK + 8, :], gl8)
            gl_ref[d, :, cols] = gl8

    for (d, cols), uw in zip(dests, _unit_tri_solve_many(lmats, rhss, eye)):
        u_ref[d, :, cols] = uw[:, :GDN_DV]
        w_ref[d, :, cols] = uw[:, GDN_DV:].astype(w_ref.dtype)


def _gdn_intra(qkv, proj_ab, alog_row, dt_row, row0):
    t = qkv.shape[0]
    nb = t // GDN_BLK
    pb0 = row0 // GDN_BLK
    big = lambda dt: jax.ShapeDtypeStruct((2, t, GDN_W), dt)
    big_spec = pl.BlockSpec((2, GDN_BLK, GDN_W), lambda i: (0, i, 0))
    return pl.pallas_call(
        _gdn_intra_kernel,
        grid=(nb,),
        in_specs=[
            pl.BlockSpec((GDN_BLK, GDN_W), lambda i: (i, 0)),
            pl.BlockSpec((GDN_BLK, GDN_W), lambda i: (i, 1)),
            pl.BlockSpec((GDN_BLK, GDN_W), lambda i: (i, 2)),
            pl.BlockSpec((GDN_BLK, LANE), lambda i: (pb0 + i, 0)),
            pl.BlockSpec((1, LANE), lambda i: (0, 0)),
            pl.BlockSpec((1, LANE), lambda i: (0, 0)),
        ],
        out_specs=[big_spec, big_spec, big_spec, big_spec,
                   pl.BlockSpec((2, GDN_HEADS, GDN_BLK, CHUNK), lambda i: (0, 0, i, 0)),
                   pl.BlockSpec((2, 8, GDN_W), lambda i: (0, i, 0))],
        out_shape=[big(F32), big(BF16), big(BF16), big(BF16),
                   jax.ShapeDtypeStruct((2, GDN_HEADS, t, CHUNK), BF16),
                   jax.ShapeDtypeStruct((2, nb * 8, GDN_W), F32)],
        compiler_params=_cparams("parallel"),
        name="gdn_intra",
    )(qkv, qkv, qkv, proj_ab, alog_row, dt_row)


def _gdn_scan_kernel(*refs, has_init):
    if has_init:
        s0_ref, refs = refs[0], refs[1:]
    ins = (refs[0:6], refs[6:12])
    o_refs, sf_ref, s_ref = refs[12:14], refs[14], refs[15]
    blk = pl.program_id(1)
    nchunk = GDN_BLK // CHUNK

    @pl.when(blk == 0)
    def _():
        if has_init:
            s_ref[...] = s0_ref[...]
        else:
            s_ref[...] = jnp.zeros_like(s_ref)

    U, W, QG, KD, A, GL = range(6)
    chains = [(d, h) for d in range(2) for h in range(GDN_HEADS)]
    states = [s_ref[d, h] for d, h in chains]
    for c in range(nchunk):
        ccs = [c if d == 0 else nchunk - 1 - c for d, _ in chains]
        rows = [slice(cc * CHUNK, (cc + 1) * CHUNK) for cc in ccs]
        cols = [slice(h * LANE, (h + 1) * LANE) for _, h in chains]
        n = range(len(chains))
        ws = [_mm(ins[chains[i][0]][W][rows[i], cols[i]], states[i]) for i in n]
        qs = [_mm(ins[chains[i][0]][QG][rows[i], cols[i]], states[i]) for i in n]
        v_news = [ins[chains[i][0]][U][rows[i], cols[i]] - ws[i] for i in n]
        for i in n:
            d, h = chains[i]
            o_refs[d][rows[i], cols[i]] = qs[i] + _mm(ins[d][A][h, rows[i], :], v_news[i])
        states = [states[i] * ins[chains[i][0]][GL][ccs[i]:ccs[i] + 1, cols[i]]
                  + _mm(ins[chains[i][0]][KD][rows[i], cols[i]], v_news[i], _dot_tn) for i in n]
    for (d, h), s in zip(chains, states):
        s_ref[d, h] = s

    @pl.when(blk == pl.num_programs(1) - 1)
    def _():
        sf_ref[...] = s_ref[...]


def _gdn_scan(s0, u, w, qg, kd, a, gl, nseq, seqlen):
    t = u.shape[1]
    nblk = seqlen // GDN_BLK
    in_specs, args = [], []
    for d in range(2):
        rb = (lambda s, b: s * nblk + b) if d == 0 else (lambda s, b: s * nblk + nblk - 1 - b)
        big = pl.BlockSpec((None, GDN_BLK, GDN_W), lambda s, b, d=d, rb=rb: (d, rb(s, b), 0))
        in_specs += [big, big, big, big,
                     pl.BlockSpec((None, GDN_HEADS, GDN_BLK, CHUNK), lambda s, b, d=d, rb=rb: (d, 0, rb(s, b), 0)),
                     pl.BlockSpec((None, 8, GDN_W), lambda s, b, d=d, rb=rb: (d, rb(s, b), 0))]
        args += [u, w, qg, kd, a, gl]
    state_spec = pl.BlockSpec((None, 2, GDN_HEADS, GDN_DK, GDN_DV), lambda s, b: (s, 0, 0, 0, 0))
    if s0 is not None:
        in_specs.insert(0, state_spec)
        args.insert(0, s0)
    out_specs = [pl.BlockSpec((GDN_BLK, GDN_W), lambda s, b: (s * nblk + b, 0)),
                 pl.BlockSpec((GDN_BLK, GDN_W), lambda s, b: (s * nblk + nblk - 1 - b, 0)),
                 state_spec]
    return pl.pallas_call(
        functools.partial(_gdn_scan_kernel, has_init=s0 is not None),
        grid=(nseq, nblk),
        in_specs=in_specs,
        out_specs=out_specs,
        out_shape=[jax.ShapeDtypeStruct((t, GDN_W), F32), jax.ShapeDtypeStruct((t, GDN_W), F32),
                   jax.ShapeDtypeStruct((nseq, 2, GDN_HEADS, GDN_DK, GDN_DV), F32)],
        scratch_shapes=[pltpu.VMEM((2, GDN_HEADS, GDN_DK, GDN_DV), F32)],
        compiler_params=_cparams("parallel", "arbitrary"),
        name="gdn_scan",
    )(*args)


def _gdn_post_kernel(of_ref, ob_ref, z_ref, g_ref, out_ref):
    for h in range(GDN_HEADS):
        cols = slice(h * LANE, (h + 1) * LANE)
        o = of_ref[:, cols] + ob_ref[:, cols]
        y = o * lax.rsqrt(jnp.mean(o * o, axis=-1, keepdims=True) + EPS) * g_ref[...]
        out_ref[:, cols] = (y * _silu(z_ref[:, cols])).astype(out_ref.dtype)


def _gdn_post(o_f, o_b, proj, g, row0):
    t = o_f.shape[0]
    tm = _row_tile((t, row0 + t), 1024)
    pb0 = row0 // tm
    return pl.pallas_call(
        _gdn_post_kernel,
        grid=(t // tm,),
        in_specs=[
            pl.BlockSpec((tm, GDN_W), lambda i: (i, 0)),
            pl.BlockSpec((tm, GDN_W), lambda i: (i, 0)),
            pl.BlockSpec((tm, GDN_W), lambda i: (pb0 + i, CB_Z // GDN_HEADS)),
            pl.BlockSpec((1, GDN_DV), lambda i: (0, 0)),
        ],
        out_specs=pl.BlockSpec((tm, GDN_W), lambda i: (i, 0)),
        out_shape=jax.ShapeDtypeStruct((t, GDN_W), BF16),
        compiler_params=_cparams("parallel"),
        name="gdn_post",
    )(o_f, o_b, proj, g.reshape(1, GDN_DV))


def _rope_rotate(x, cos, sin_signed):
    lane = lax.broadcasted_iota(jnp.int32, x.shape, 1)
    first = (lane % 32) < 16
    partner = jnp.where(first, pltpu.roll(x, LANE - 16, 1), pltpu.roll(x, 16, 1))
    return x * cos + partner * sin_signed


def _rope_tables(seqlen):
    pos = jnp.arange(seqlen)
    rowp = (pos // GRID_W).astype(F32)
    colp = (pos % GRID_W).astype(F32)
    half = DIFF_DQK // 2
    inv_freq = ROPE_BASE ** (-jnp.arange(0, half, 2, dtype=F32) / half)
    c = jnp.arange(LANE)
    c64 = c % DIFF_DQK
    p = jnp.where((c64 // half)[None, :] == 0, rowp[:, None], colp[:, None])
    ang = p * inv_freq[c64 % (half // 2)][None, :]
    sign = jnp.where((c64 % half) < half // 2, -1.0, 1.0).astype(F32)
    return jnp.cos(ang), jnp.sin(ang) * sign[None, :]


def _diff_attn_kernel(*refs, seg_lens, lam_init, rope, hp, n_alias, emit_cache):
    nseg = len(seg_lens)
    q_ref = refs[0]
    k_refs = refs[1:1 + nseg]
    v_refs = refs[1 + nseg:1 + 2 * nseg]
    pos = 1 + 2 * nseg
    if rope:
        cosq_ref, sinq_ref, cosk_ref, sink_ref = refs[pos:pos + 4]
        pos += 4
    lam_ref, g_ref = refs[pos:pos + 2]
    pos += 2 + n_alias
    o_ref = refs[pos]
    if emit_cache:
        kc_ref, vc_ref = refs[pos + 1:pos + 3]
    kb_ref, vb_ref = refs[-2:]
    offs = [sum(seg_lens[:s]) for s in range(nseg)]
    hcols = [slice(hh * DIFF_DV, (hh + 1) * DIFF_DV) for hh in range(hp)]

    @pl.when(pl.program_id(2) == 0)
    def _():
        if emit_cache:
            kc_ref[...] = k_refs[-1][...]
            vc_ref[...] = v_refs[-1][...]
        for hh in range(hp):
            vb_ref[:, (2 * hh + 1) * DIFF_DV:(2 * hh + 2) * DIFF_DV] = jnp.ones((vb_ref.shape[0], DIFF_DV), BF16)
            for s in range(nseg):
                rows = slice(offs[s], offs[s] + seg_lens[s])
                k = k_refs[s][:, hcols[hh]]
                if rope and s == nseg - 1:
                    k = _rope_rotate(k, cosk_ref[...], sink_ref[...])
                kb_ref[rows, hcols[hh]] = k.astype(BF16)
                vb_ref[rows, 2 * hh * DIFF_DV:(2 * hh + 1) * DIFF_DV] = v_refs[s][:, hcols[hh]].astype(BF16)

    lv = lam_ref[...]
    lam = (jnp.exp(jnp.sum(lv[0:1] * lv[1:2], axis=1, keepdims=True))
           - jnp.exp(jnp.sum(lv[2:3] * lv[3:4], axis=1, keepdims=True)) + lam_init)
    lane = lax.broadcasted_iota(jnp.int32, (ATTN_SUB, DIFF_DV), 1)

    def scores(hh, rows):
        q = q_ref[rows, hcols[hh]]
        if rope:
            q = _rope_rotate(q, cosq_ref[rows, :], sinq_ref[rows, :])
        q = q * (DIFF_DQK ** -0.5)
        kb = kb_ref[:, hcols[hh]]
        return [_dot_nt(jnp.where((lane // DIFF_DQK) == m, q, 0.0).astype(BF16), kb) for m in range(2)]

    units = [(hh, slice(r * ATTN_SUB, (r + 1) * ATTN_SUB))
             for hh in range(hp) for r in range(q_ref.shape[0] // ATTN_SUB)]
    ss_next = scores(*units[0])
    for n, (hh, rows) in enumerate(units):
        ss = ss_next
        if n + 1 < len(units):
            ss_next = scores(*units[n + 1])
        vb = vb_ref[:, 2 * hh * DIFF_DV:(2 * hh + 2) * DIFF_DV]
        es = [jnp.exp(s - jnp.max(s, axis=1, keepdims=True)).astype(BF16) for s in ss]
        accs = [_dot(e, vb) for e in es]
        outs = [acc[:, 0:DIFF_DV] / acc[:, DIFF_DV:] for acc in accs]
        o = outs[0] - lam * outs[1]
        y = o * lax.rsqrt(jnp.mean(o * o, axis=-1, keepdims=True) + EPS) * g_ref[...]
        o_ref[rows, hcols[hh]] = (y * (1.0 - lam_init)).astype(o_ref.dtype)


def _diff_attn(proj, row0, nseq, seqlen, past_kv, l, lam_vec, g, cache_out):
    tq = min(4 * ATTN_SUB, seqlen)
    hp = min(DIFF_HEADS, max(1, ATTN_STEP_ROWS // seqlen))
    nqb = seqlen // tq
    qb0 = row0 // tq
    sb0 = row0 // seqlen
    wcols = hp * DIFF_DV
    q_spec = pl.BlockSpec((tq, wcols), lambda b, h, i: (qb0 + b * nqb + i, CB_DQ // hp + h))
    k_new = pl.BlockSpec((seqlen, wcols), lambda b, h, i: (sb0 + b, CB_DK // hp + h))
    v_new = pl.BlockSpec((seqlen, wcols), lambda b, h, i: (sb0 + b, CB_DV // hp + h))
    arrs, specs, seg_lens = [proj], [q_spec], [seqlen]
    if past_kv is None:
        arrs += [proj, proj]
        specs += [k_new, v_new]
    else:
        past = past_kv[0].shape[2]
        cached = pl.BlockSpec((None, None, past, wcols), lambda b, h, i: (b, l, 0, h))
        arrs += [past_kv[0], proj, past_kv[1], proj]
        specs += [cached, k_new, cached, v_new]
        seg_lens = [past, seqlen]
        cos, sin = _rope_tables(seqlen)
        arrs += [cos, sin, cos, sin]
        specs += [pl.BlockSpec((tq, DIFF_DV), lambda b, h, i: (i, 0))] * 2 + [
            pl.BlockSpec((seqlen, DIFF_DV), lambda b, h, i: (0, 0))] * 2
    arrs += [lam_vec, g.reshape(1, DIFF_DV)]
    specs += [pl.BlockSpec((4, DIFF_DQK), lambda b, h, i: (0, 0)),
              pl.BlockSpec((1, DIFF_DV), lambda b, h, i: (0, 0))]
    out_specs = [pl.BlockSpec((tq, wcols), lambda b, h, i: (b * nqb + i, h))]
    out_shape = [jax.ShapeDtypeStruct((nseq * seqlen, DIFF_W), BF16)]
    aliases, n_alias = {}, 0
    if cache_out is not None:
        depth, prev = cache_out
        entry = pl.BlockSpec((None, None, seqlen, wcols), lambda b, h, i: (b, l, 0, h))
        out_specs += [entry, entry]
        out_shape += [jax.ShapeDtypeStruct((nseq, depth, seqlen, DIFF_W), F32)] * 2
        if prev is not None:
            aliases = {len(arrs): 1, len(arrs) + 1: 2}
            n_alias = 2
            arrs += list(prev)
            specs += [pl.BlockSpec(memory_space=pl.ANY)] * 2
    lk = sum(seg_lens)
    outs = pl.pallas_call(
        functools.partial(_diff_attn_kernel, seg_lens=tuple(seg_lens), lam_init=_lambda_init(l),
                          rope=past_kv is not None, hp=hp, n_alias=n_alias, emit_cache=cache_out is not None),
        grid=(nseq, DIFF_HEADS // hp, nqb),
        in_specs=specs,
        out_specs=out_specs,
        out_shape=out_shape,
        input_output_aliases=aliases,
        scratch_shapes=[pltpu.VMEM((lk, wcols), BF16), pltpu.VMEM((lk, 2 * wcols), BF16)],
        compiler_params=_cparams("parallel", "parallel", "arbitrary"),
        name="diff_attn",
    )(*arrs)
    return outs[0], (tuple(outs[1:]) if cache_out is not None else None)


def _pool_kernel(x_ref, w_ref, sc_ref, o_ref):
    n = x_ref.shape[0]
    row = lax.broadcasted_iota(jnp.int32, (n, POOL_GW), 0)
    for gi, win in enumerate(POOL_WINDOWS):
        cols = slice(gi * POOL_GW, (gi + 1) * POOL_GW)
        x = x_ref[:, cols]
        a = win // 2
        b = win - a - 1
        acc = x
        for off in range(-a, b + 1):
            if off != 0:
                acc = acc + _shift_rows(x, off, row)
        cnt = (jnp.minimum(row + b + 1, n) - jnp.maximum(row - a, 0)).astype(F32)
        pooled = acc / cnt - x
        y = _dot(pooled.astype(BF16), w_ref[gi].astype(BF16))
        o_ref[:, cols] = (y * sc_ref[:, cols]).astype(o_ref.dtype)


def _pool(proj, pool_w, pool_scale, row0, nseq, seqlen):
    blk0 = row0 // seqlen
    return pl.pallas_call(
        _pool_kernel,
        grid=(nseq,),
        in_specs=[
            pl.BlockSpec((seqlen, POOL_W), lambda s: (blk0 + s, CB_PIN // len(POOL_WINDOWS))),
            pl.BlockSpec((len(POOL_WINDOWS), POOL_GW, POOL_GW), lambda s: (0, 0, 0)),
            pl.BlockSpec((1, POOL_W), lambda s: (0, 0)),
        ],
        out_specs=pl.BlockSpec((seqlen, POOL_W), lambda s: (s, 0)),
        out_shape=jax.ShapeDtypeStruct((nseq * seqlen, POOL_W), BF16),
        compiler_params=_cparams("parallel"),
        name="pool",
    )(proj, pool_w, pool_scale.reshape(1, POOL_W))


def _mix_group(proj, proj_ab, lp, l, row0, nseq, seqlen, ctx, cache_out):
    qkv = _gdn_prep(proj, lp["gdn_conv"], row0, nseq, seqlen)
    u, w, qg, kd, a, gl = _gdn_intra(qkv, proj_ab, lp["alog_row"], lp["dt_row"], row0)
    o_f, o_b, s_fin = _gdn_scan(None if ctx is None else ctx[2], u, w, qg, kd, a, gl, nseq, seqlen)
    o_gdn = _gdn_post(o_f, o_b, proj, lp["gdn_norm"], row0)

    o_diff, caches = _diff_attn(proj, row0, nseq, seqlen, None if ctx is None else ctx[:2], l,
                                lp["diff_lam"], lp["diff_norm"], cache_out)
    o_pool = _pool(proj, lp["pool_w"], lp["pool_scale"], row0, nseq, seqlen)
    return (o_gdn, o_diff, o_pool), s_fin, caches


def _split_w_in(w_in):
    depth, d, _ = w_in.shape
    o_ab = 4 * GDN_W
    o_dq = o_ab + 4 * GDN_HEADS
    pad = jnp.zeros((depth, d, LANE - 4 * GDN_HEADS), w_in.dtype)
    main = jnp.concatenate([w_in[..., :o_ab], w_in[..., o_dq:]], axis=-1)
    ab = jnp.concatenate([w_in[..., o_ab:o_dq], pad], axis=-1)
    return main.astype(BF16), ab.astype(BF16)


def kernel(x_prompt, x_sample, c, cache_k, cache_v, state_gdn, c_ctx, w_ada, b_ada, norm_ffn1, ffn1_in,
           ffn1_out, norm_mix, w_in, gdn_conv, gdn_a_log, gdn_dt_bias, gdn_norm, diff_lam, diff_norm,
           pool_w, pool_scale, w_out, norm_ffn2, ffn2_in, ffn2_out, final_norm):
    batch, seq, d = x_prompt.shape
    dec_batch, dec_seq, _ = x_sample.shape
    depth = w_ada.shape[0]
    n_ctx = batch * seq
    n_dec = dec_batch * dec_seq
    layout = (n_ctx, dec_seq)
    past = cache_k.shape[2]

    x = (x_prompt.reshape(n_ctx, d), x_sample.reshape(n_dec, d))
    cond8 = jnp.concatenate([c_ctx[None, :], c, jnp.zeros((8 - 1 - dec_batch, d), F32)], axis=0)
    ada = _ada(cond8, w_ada, b_ada)
    cache_k4 = cache_k.reshape(dec_batch, depth, past, DIFF_W)
    cache_v4 = cache_v.reshape(dec_batch, depth, past, DIFF_W)
    w_in_main, w_in_ab = _split_w_in(w_in)
    w_out_b = w_out.astype(BF16)
    ffn_w = [(ffn1_in, ffn1_out.astype(BF16)), (ffn2_in, ffn2_out.astype(BF16))]

    caches, ss = None, []
    for l in range(depth):
        ada3 = ada[l, :1 + dec_batch].reshape(1 + dec_batch, 1, N_MOD * d)
        lane_pad = jnp.zeros((LANE - 2 * GDN_HEADS,), F32)
        lp = dict(
            gdn_conv=gdn_conv[l], gdn_norm=gdn_norm[l], diff_lam=diff_lam[l], diff_norm=diff_norm[l],
            pool_w=pool_w[l], pool_scale=pool_scale[l],
            alog_row=jnp.concatenate([gdn_a_log[l].reshape(-1), lane_pad]).reshape(1, LANE),
            dt_row=jnp.concatenate([gdn_dt_bias[l].reshape(-1), lane_pad]).reshape(1, LANE),
        )
        act = _ffn_in(x, ada3, (0, 1), norm_ffn1[l], ffn_w[0][0], l, layout=layout)
        x = _ffn_out(act, ffn_w[0][1], l, x, ada3, 2, layout=layout)

        proj, proj_ab = _mix_in(x, ada3, (3, 4), norm_mix[l], w_in_main, w_in_ab, l, layout=layout)
        acts_c, s_c, caches = _mix_group(proj, proj_ab, lp, l, 0, batch, seq, None, (depth, caches))
        acts_d, _, _ = _mix_group(proj, proj_ab, lp, l, n_ctx, dec_batch, dec_seq,
                                  (cache_k4, cache_v4, state_gdn[:, l]), None)
        x = _mix_out(acts_c, w_out_b, l, x, ada3, 5, 0, layout=layout)
        x = _mix_out(acts_d, w_out_b, l, x, ada3, 5, n_ctx, layout=layout)
        ss.append(s_c)

        act = _ffn_in(x, ada3, (6, 7), norm_ffn2[l], ffn_w[1][0], l, layout=layout)
        x = _ffn_out(act, ffn_w[1][1], l, x, ada3, 8, layout=layout)

    y_prompt = _final_norm(x, final_norm, 0, n_ctx).reshape(batch, seq, d)
    y_sample = _final_norm(x, final_norm, n_ctx, n_dec).reshape(dec_batch, dec_seq, d)
    new_k, new_v = (t.reshape(batch, depth, seq, DIFF_HEADS, DIFF_DV) for t in caches)
    return (y_prompt, y_sample, new_k, new_v, jnp.stack(ss, axis=1))
```
